```python
import jax, jax.numpy as jnp
from jax import lax
import numpy as np

D_MODEL = 2048
BATCH = 16
SEQ = 256
DEPTH = 4
DEC_BATCH = 4
DEC_SEQ = 4096
PAST_LEN = 512

GRID_W = 64
BLK = 128
ROPE_BASE = 10000.0
EPS = 1e-6
NEG_INF = -1e30
N_BRANCH = 4
BR_W = 512

MLA_H = 4
MLA_Q_LORA = 512
MLA_KV_LORA = 256
MLA_NOPE = 128
MLA_ROPE = 64
MLA_V = 128

RET_H = 4
RET_DK = 128
RET_DV = 128

NAT_H = 4
NAT_D = 128
NAT_ROWS = 8
NAT_COLS = 16

SWA_H = 8
SWA_KVH = 2
SWA_D = 64
SWA_WINDOW = 128

MIX_SPLITS = (MLA_Q_LORA, MLA_KV_LORA, MLA_ROPE,
              RET_H * RET_DK, RET_H * RET_DK, RET_H * RET_DV,
              NAT_H * NAT_D, NAT_H * NAT_D, NAT_H * NAT_D,
              SWA_H * SWA_D, SWA_KVH * SWA_D, SWA_KVH * SWA_D,
              N_BRANCH * BR_W)
MIX_COLS = sum(MIX_SPLITS)
IN_COLS = MIX_COLS + N_BRANCH * D_MODEL

kernel_name = "hybrid_dit_mla_ret_nat_swa_step"


def rmsnorm(x, g):
    xf = x.astype(jnp.float32)
    y = xf * lax.rsqrt(jnp.mean(xf * xf, axis=-1, keepdims=True) + EPS)
    return (y * g.astype(jnp.float32)).astype(x.dtype)


def softmax_sink(s, sink):
    m = jnp.max(s, axis=-1, keepdims=True)
    if sink is None:
        e = jnp.exp(s - m)
        return e / jnp.sum(e, axis=-1, keepdims=True)
    m = jnp.maximum(m, sink)
    e = jnp.exp(s - m)
    return e / (jnp.sum(e, axis=-1, keepdims=True) + jnp.exp(sink - m))


def axial_rope_tables(n_tok, rot_dim):
    pos = jnp.arange(n_tok)
    row = (pos // GRID_W).astype(jnp.float32)
    col = (pos % GRID_W).astype(jnp.float32)
    n_freq = rot_dim // 4
    inv = ROPE_BASE ** (-jnp.arange(n_freq, dtype=jnp.float32) / n_freq)
    ang = jnp.concatenate([row[:, None] * inv[None], col[:, None] * inv[None]], axis=-1)
    return jnp.cos(ang), jnp.sin(ang)


def apply_rope(x, cos, sin):
    half = x.shape[-1] // 2
    x1, x2 = x[..., :half], x[..., half:]
    c = cos[:, None, :].astype(x.dtype)
    s = sin[:, None, :].astype(x.dtype)
    return jnp.concatenate([x1 * c - x2 * s, x1 * s + x2 * c], axis=-1)


def dense_attention(q, k, v, sink=None):
    B, Sq, H, d = q.shape
    kvh = k.shape[2]
    g = H // kvh
    dv = v.shape[-1]
    scale = d ** -0.5
    nb = Sq // BLK
    qb = jnp.moveaxis(q.reshape(B, nb, BLK, kvh, g, d), 1, 0)
    sk = None if sink is None else sink.astype(jnp.float32).reshape(kvh, g, 1, 1)

    def one_block(qi):
        s = jnp.einsum('bqkgd,bskd->bkgqs', qi, k).astype(jnp.float32) * scale
        p = softmax_sink(s, sk).astype(v.dtype)
        return jnp.einsum('bkgqs,bskd->bqkgd', p, v)

    o = lax.map(one_block, qb)
    return jnp.moveaxis(o, 0, 1).reshape(B, Sq, H * dv)


def retention_scan(q, k, v, log_g, s0, strict):
    f32 = jnp.float32
    B, S, H, dk = q.shape
    dv = v.shape[-1]
    nc = S // BLK
    lg = log_g.astype(f32)

    def chunks(t):
        return jnp.transpose(t.astype(f32).reshape(B, nc, BLK, H, t.shape[-1]), (1, 0, 3, 2, 4))

    idx = jnp.arange(BLK, dtype=f32)
    diff = idx[:, None] - idx[None, :]
    mask = (diff > 0) if strict else (diff >= 0)
    dmat = jnp.where(mask[None], jnp.exp(lg[:, None, None] * jnp.maximum(diff, 0.0)[None]), 0.0)
    q_dec = jnp.exp(lg[:, None] * (idx[None] + 1.0))[..., None]
    k_dec = jnp.exp(lg[:, None] * (BLK - 1.0 - idx[None]))[..., None]
    c_dec = jnp.exp(lg * BLK)[:, None, None]

    def step(state, inp):
        qi, ki, vi = inp
        inner = jnp.einsum('bhqd,bhkd->bhqk', qi, ki) * dmat
        o = jnp.einsum('bhqk,bhkv->bhqv', inner, vi) + jnp.einsum('bhqd,bhdv->bhqv', qi * q_dec, state)
        state = state * c_dec + jnp.einsum('bhkd,bhkv->bhdv', ki * k_dec, vi)
        return state, o

    s_fin, o = lax.scan(step, s0.astype(f32), (chunks(q), chunks(k), chunks(v)))
    o = jnp.transpose(o, (1, 0, 3, 2, 4)).reshape(B, S, H, dv)
    return o, s_fin


def retention_bidir(q, k, v, log_g2, s0_fwd, s0_bwd):
    o_f, s_f = retention_scan(q, k, v, log_g2[0], s0_fwd, strict=False)
    o_b, s_b = retention_scan(q[:, ::-1], k[:, ::-1], v[:, ::-1], log_g2[1], s0_bwd, strict=True)
    return o_f + o_b[:, ::-1], s_f, s_b


def ret_mixer(rq, rk, rv, p, s0_fwd, s0_bwd):
    B, S, _ = rq.shape
    q = rq.reshape(B, S, RET_H, RET_DK)
    k = rk.reshape(B, S, RET_H, RET_DK) * (RET_DK ** -0.5)
    v = rv.reshape(B, S, RET_H, RET_DV)
    log_g2 = jax.nn.log_sigmoid(p['ret_decay'].astype(jnp.float32))
    o, s_f, s_b = retention_bidir(q, k, v, log_g2, s0_fwd, s0_bwd)
    o = rmsnorm(o, p['ret_norm'].reshape(RET_H, RET_DV))
    return o.reshape(B, S, BR_W).astype(rq.dtype), s_f, s_b


def mla_queries(q_a, p):
    B, S, _ = q_a.shape
    cq = rmsnorm(q_a, p['mla_q_norm'])
    return (cq @ p['mla_w_q_up']).reshape(B, S, MLA_H, MLA_NOPE + MLA_ROPE)


def mla_expand(ckv, k_rope, w_kv_up):
    B, S, _ = ckv.shape
    kv = (ckv @ w_kv_up).reshape(B, S, MLA_H, MLA_NOPE + MLA_V)
    kr = jnp.broadcast_to(k_rope[:, :, None, :], (B, S, MLA_H, MLA_ROPE))
    return jnp.concatenate([kv[..., :MLA_NOPE], kr], axis=-1), kv[..., MLA_NOPE:]


def nat_latent(q, k, v, k_ctx, v_ctx, rpb):
    f32 = jnp.float32
    B, S, H, d = q.shape
    rows = S // GRID_W
    kr = min(NAT_ROWS, rows)
    scale = d ** -0.5
    qg = jnp.moveaxis(q.reshape(B, rows, GRID_W, H, d), 1, 0)
    kg = k.reshape(B, rows, GRID_W, H, d)
    vg = v.reshape(B, rows, GRID_W, H, d)
    qcol = jnp.arange(GRID_W)
    cstart = jnp.clip(qcol - NAT_COLS // 2, 0, GRID_W - NAT_COLS)
    cidx = cstart[:, None] + jnp.arange(NAT_COLS)[None, :]
    col_off = cidx - qcol[:, None] + (NAT_COLS - 1)
    n_loc = kr * NAT_COLS

    def one_row(args):
        r, qr = args
        rs = jnp.clip(r - kr // 2, 0, rows - kr)
        k_win = lax.dynamic_slice_in_dim(kg, rs, kr, axis=1)[:, :, cidx]
        v_win = lax.dynamic_slice_in_dim(vg, rs, kr, axis=1)[:, :, cidx]
        row_off = rs + jnp.arange(kr) - r + (NAT_ROWS - 1)
        bias = rpb[:, row_off[None, :, None], col_off[:, None, :]]
        s_loc = jnp.einsum('bqhd,bmqnhd->bhqmn', qr, k_win).astype(f32) * scale + bias.astype(f32)[None]
        s_loc = s_loc.reshape(B, H, GRID_W, n_loc)
        s_ctx = jnp.einsum('bqhd,blhd->bhql', qr, k_ctx).astype(f32) * scale
        pr = softmax_sink(jnp.concatenate([s_loc, s_ctx], axis=-1), None).astype(v.dtype)
        p_loc = pr[..., :n_loc].reshape(B, H, GRID_W, kr, NAT_COLS)
        return (jnp.einsum('bhqmn,bmqnhd->bqhd', p_loc, v_win)
                + jnp.einsum('bhql,blhd->bqhd', pr[..., n_loc:], v_ctx))

    o = lax.map(one_row, (jnp.arange(rows), qg))
    return jnp.moveaxis(o, 0, 1).reshape(B, S, H * d)


def swa_latent(q, k, v, k_ctx, v_ctx, sink):
    f32 = jnp.float32
    B, S, H, d = q.shape
    kvh = k.shape[2]
    g = H // kvh
    nb = S // BLK
    scale = d ** -0.5
    pad = jnp.zeros((B, BLK, kvh, d), k.dtype)
    kp = jnp.concatenate([pad, k, pad], axis=1)
    vp = jnp.concatenate([pad.astype(v.dtype), v, pad.astype(v.dtype)], axis=1)
    qb = jnp.moveaxis(q.reshape(B, nb, BLK, kvh, g, d), 1, 0)
    sk = sink.astype(f32).reshape(kvh, g, 1, 1)
    rel = jnp.arange(BLK)[:, None] - (jnp.arange(3 * BLK)[None, :] - BLK)
    band = jnp.abs(rel) <= SWA_WINDOW

    def one_block(args):
        i, qi = args
        kw = lax.dynamic_slice_in_dim(kp, i * BLK, 3 * BLK, axis=1)
        vw = lax.dynamic_slice_in_dim(vp, i * BLK, 3 * BLK, axis=1)
        kpos = (i - 1) * BLK + jnp.arange(3 * BLK)
        valid = band & ((kpos >= 0) & (kpos < S))[None, :]
        s_loc = jnp.einsum('bqkgd,bskd->bkgqs', qi, kw).astype(f32) * scale
        s_loc = jnp.where(valid, s_loc, NEG_INF)
        s_ctx = jnp.einsum('bqkgd,bskd->bkgqs', qi, k_ctx).astype(f32) * scale
        pr = softmax_sink(jnp.concatenate([s_loc, s_ctx], axis=-1), sk).astype(v.dtype)
        return (jnp.einsum('bkgqs,bskd->bqkgd', pr[..., :3 * BLK], vw)
                + jnp.einsum('bkgqs,bskd->bqkgd', pr[..., 3 * BLK:], v_ctx))

    o = lax.map(one_block, (jnp.arange(nb), qb))
    return jnp.moveaxis(o, 0, 1).reshape(B, S, H * d)


def modulation(cond, p):
    m = jax.nn.silu(cond) @ p['w_mod'] + p['b_mod']
    return jnp.split(m, 3, axis=-1)


def in_projection(x, shift, scale, p):
    h = rmsnorm(x, p['norm_pre']) * (1 + scale[:, None, :]) + shift[:, None, :]
    idx = [int(i) for i in np.cumsum(MIX_SPLITS)[:-1]]
    parts = jnp.split(h @ p['w_in'][:, :MIX_COLS], idx, axis=-1)
    return h, parts


def merge_branches(x, h, outs, gpath, gate, p):
    w_gates = p['w_in'][:, MIX_COLS:]
    y = None
    for n in range(N_BRANCH):
        o_n = outs[n] * jax.nn.silu(gpath[..., n * BR_W:(n + 1) * BR_W])
        g_n = jax.nn.sigmoid(h @ w_gates[:, n * D_MODEL:(n + 1) * D_MODEL])
        t = g_n * (o_n @ p['w_branch'][n])
        y = t if y is None else y + t
    y = y @ p['w_out']
    return x + gate[:, None, :] * rmsnorm(y, p['norm_post'])


def layer_context(x, cond, p):
    B, L, _ = x.shape
    shift, scale, gate = modulation(cond, p)
    h, (q_a, kv_a, k_rope, rq, rk, rv, nq, nk, nv, sq, sk, sv, gpath) = in_projection(x, shift, scale, p)
    q = mla_queries(q_a, p)
    ckv = rmsnorm(kv_a, p['mla_kv_norm'])
    k_m, v_m = mla_expand(ckv, k_rope, p['mla_w_kv_up'])
    o_mla = dense_attention(q, k_m, v_m)
    s0 = jnp.zeros((B, RET_H, RET_DK, RET_DV), jnp.float32)
    o_ret, s_f, s_b = ret_mixer(rq, rk, rv, p, s0, s0)
    nk_h = nk.reshape(B, L, NAT_H, NAT_D)
    nv_h = nv.reshape(B, L, NAT_H, NAT_D)
    o_nat = dense_attention(nq.reshape(B, L, NAT_H, NAT_D), nk_h, nv_h)
    sk_h = sk.reshape(B, L, SWA_KVH, SWA_D)
    sv_h = sv.reshape(B, L, SWA_KVH, SWA_D)
    o_swa = dense_attention(sq.reshape(B, L, SWA_H, SWA_D), sk_h, sv_h, sink=p['swa_sink'])
    x = merge_branches(x, h, (o_mla, o_ret, o_nat, o_swa), gpath, gate, p)
    st = jnp.stack([s_f, s_b], axis=1).astype(x.dtype)
    return x, (ckv, k_rope, st, nk_h, nv_h, sk_h, sv_h)


def layer_latent(x, cond, ckv_c, kr_c, st_c, nk_c, nv_c, sk_c, sv_c, p):
    B, S, _ = x.shape
    shift, scale, gate = modulation(cond, p)
    h, (q_a, kv_a, k_rope, rq, rk, rv, nq, nk, nv, sq, sk, sv, gpath) = in_projection(x, shift, scale, p)
    cos_m, sin_m = axial_rope_tables(S, MLA_ROPE)
    cos_s, sin_s = axial_rope_tables(S, SWA_D)
    q = mla_queries(q_a, p)
    q = jnp.concatenate([q[..., :MLA_NOPE], apply_rope(q[..., MLA_NOPE:], cos_m, sin_m)], axis=-1)
    kr = apply_rope(k_rope[:, :, None, :], cos_m, sin_m)[:, :, 0]
    ckv = rmsnorm(kv_a, p['mla_kv_norm'])
    k_lat, v_lat = mla_expand(ckv, kr, p['mla_w_kv_up'])
    k_ctx, v_ctx = mla_expand(ckv_c, kr_c, p['mla_w_kv_up'])
    o_mla = dense_attention(q, jnp.concatenate([k_lat, k_ctx], axis=1), jnp.concatenate([v_lat, v_ctx], axis=1))
    o_ret, _, _ = ret_mixer(rq, rk, rv, p, st_c[:, 0], st_c[:, 1])
    o_nat = nat_latent(nq.reshape(B, S, NAT_H, NAT_D), nk.reshape(B, S, NAT_H, NAT_D),
                       nv.reshape(B, S, NAT_H, NAT_D), nk_c, nv_c, p['nat_rpb'])
    qs = apply_rope(sq.reshape(B, S, SWA_H, SWA_D), cos_s, sin_s)
    ks = apply_rope(sk.reshape(B, S, SWA_KVH, SWA_D), cos_s, sin_s)
    o_swa = swa_latent(qs, ks, sv.reshape(B, S, SWA_KVH, SWA_D), sk_c, sv_c, p['swa_sink'])
    return merge_branches(x, h, (o_mla, o_ret, o_nat, o_swa), gpath, gate, p)


def layer_params(l, w_mod, b_mod, norm_pre, norm_post, w_in, mla_q_norm, mla_kv_norm, mla_w_q_up,
                 mla_w_kv_up, ret_decay, ret_norm, nat_rpb, swa_sink, w_branch, w_out):
    return dict(w_mod=w_mod[l], b_mod=b_mod[l], norm_pre=norm_pre[l], norm_post=norm_post[l],
                w_in=w_in[l], mla_q_norm=mla_q_norm[l], mla_kv_norm=mla_kv_norm[l],
                mla_w_q_up=mla_w_q_up[l], mla_w_kv_up=mla_w_kv_up[l], ret_decay=ret_decay[l],
                ret_norm=ret_norm[l], nat_rpb=nat_rpb[l], swa_sink=swa_sink[l],
                w_branch=w_branch[l], w_out=w_out[l])


def setup_inputs(seed: int = 0) -> dict:
    key = jax.random.key(seed)
    ks = jax.random.split(key, 28)
    f32 = jnp.float32

    def nrm(k, shape, s):
        return jax.random.normal(k, shape, f32) * s

    ret_init = jnp.log(2.0 ** (5.0 + jnp.arange(RET_H, dtype=f32)) - 1.0)
    return {
        'x_prompt': nrm(ks[0], (BATCH, SEQ, D_MODEL), 1.0),
        'x_sample': nrm(ks[1], (DEC_BATCH, DEC_SEQ, D_MODEL), 1.0),
        'cache_mla_ckv': nrm(ks[2], (DEC_BATCH, DEPTH, PAST_LEN, MLA_KV_LORA), 1.0),
        'cache_mla_krope': nrm(ks[3], (DEC_BATCH, DEPTH, PAST_LEN, MLA_ROPE), 1.0),
        'state_ret': nrm(ks[4], (DEC_BATCH, DEPTH, 2, RET_H, RET_DK, RET_DV), 0.3),
        'cache_nat_k': nrm(ks[5], (DEC_BATCH, DEPTH, PAST_LEN, NAT_H, NAT_D), 1.0),
        'cache_nat_v': nrm(ks[6], (DEC_BATCH, DEPTH, PAST_LEN, NAT_H, NAT_D), 1.0),
        'cache_swa_k': nrm(ks[7], (DEC_BATCH, DEPTH, PAST_LEN, SWA_KVH, SWA_D), 1.0),
        'cache_swa_v': nrm(ks[8], (DEC_BATCH, DEPTH, PAST_LEN, SWA_KVH, SWA_D), 1.0),
        'c': nrm(ks[9], (DEC_BATCH, D_MODEL), 1.0),
        'c_ctx': nrm(ks[10], (D_MODEL,), 1.0),
        'w_mod': nrm(ks[11], (DEPTH, D_MODEL, 3 * D_MODEL), 0.5 * D_MODEL ** -0.5),
        'b_mod': nrm(ks[12], (DEPTH, 3 * D_MODEL), 0.02),
        'norm_pre': 1.0 + nrm(ks[13], (DEPTH, D_MODEL), 0.05),
        'norm_post': 1.0 + nrm(ks[14], (DEPTH, D_MODEL), 0.05),
        'w_in': nrm(ks[15], (DEPTH, D_MODEL, IN_COLS), D_MODEL ** -0.5),
        'mla_q_norm': 1.0 + nrm(ks[16], (DEPTH, MLA_Q_LORA), 0.05),
        'mla_kv_norm': 1.0 + nrm(ks[17], (DEPTH, MLA_KV_LORA), 0.05),
        'mla_w_q_up': nrm(ks[18], (DEPTH, MLA_Q_LORA, MLA_H * (MLA_NOPE + MLA_ROPE)), MLA_Q_LORA ** -0.5),
        'mla_w_kv_up': nrm(ks[19], (DEPTH, MLA_KV_LORA, MLA_H * (MLA_NOPE + MLA_V)), MLA_KV_LORA ** -0.5),
        'ret_decay': ret_init[None, None, :] + nrm(ks[20], (DEPTH, 2, RET_H), 0.1),
        'ret_norm': 1.0 + nrm(ks[21], (DEPTH, RET_H * RET_DV), 0.05),
        'nat_rpb': nrm(ks[22], (DEPTH, NAT_H, 2 * NAT_ROWS - 1, 2 * NAT_COLS - 1), 0.5),
        'swa_sink': nrm(ks[23], (DEPTH, SWA_H), 0.5),
        'w_branch': nrm(ks[24], (DEPTH, N_BRANCH, BR_W, D_MODEL), BR_W ** -0.5),
        'w_out': nrm(ks[25], (DEPTH, D_MODEL, D_MODEL), D_MODEL ** -0.5),
    }


def reference(x_prompt, x_sample, cache_mla_ckv, cache_mla_krope, state_ret, cache_nat_k, cache_nat_v,
              cache_swa_k, cache_swa_v, c, c_ctx, w_mod, b_mod, norm_pre, norm_post, w_in, mla_q_norm,
              mla_kv_norm, mla_w_q_up, mla_w_kv_up, ret_decay, ret_norm, nat_rpb, swa_sink, w_branch, w_out):
    yp = x_prompt
    cond_ctx = c_ctx[None, :]
    ckv_l, kr_l, st_l, nk_l, nv_l, sk_l, sv_l = [], [], [], [], [], [], []
    for l in range(DEPTH):
        p = layer_params(l, w_mod, b_mod, norm_pre, norm_post, w_in, mla_q_norm, mla_kv_norm, mla_w_q_up,
                         mla_w_kv_up, ret_decay, ret_norm, nat_rpb, swa_sink, w_branch, w_out)
        yp, (ckv, kr, st, nk, nv, sk, sv) = layer_context(yp, cond_ctx, p)
        ckv_l.append(ckv); kr_l.append(kr); st_l.append(st)
        nk_l.append(nk); nv_l.append(nv); sk_l.append(sk); sv_l.append(sv)
    ys = x_sample
    for l in range(DEPTH):
        p = layer_params(l, w_mod, b_mod, norm_pre, norm_post, w_in, mla_q_norm, mla_kv_norm, mla_w_q_up,
                         mla_w_kv_up, ret_decay, ret_norm, nat_rpb, swa_sink, w_branch, w_out)
        ys = layer_latent(ys, c, cache_mla_ckv[:, l], cache_mla_krope[:, l], state_ret[:, l],
                          cache_nat_k[:, l], cache_nat_v[:, l], cache_swa_k[:, l], cache_swa_v[:, l], p)
    new_mla_ckv = jnp.stack(ckv_l, axis=1)
    new_mla_krope = jnp.stack(kr_l, axis=1)
    new_state_ret = jnp.stack(st_l, axis=1)
    new_nat_k = jnp.stack(nk_l, axis=1)
    new_nat_v = jnp.stack(nv_l, axis=1)
    new_swa_k = jnp.stack(sk_l, axis=1)
    new_swa_v = jnp.stack(sv_l, axis=1)
    return (yp, ys, new_mla_ckv, new_mla_krope, new_state_ret, new_nat_k, new_nat_v, new_swa_k, new_swa_v)
```

```python
import functools

import jax
import jax.numpy as jnp
from jax import lax
from jax.experimental import pallas as pl
from jax.experimental.pallas import tpu as pltpu

F32 = jnp.float32
BF16 = jnp.bfloat16

GRID_W = 64
BLK = 128
ROPE_BASE = 10000.0
EPS = 1e-6
NEG_INF = -1e30
N_BRANCH = 4
BR_W = 512

MLA_H = 4
MLA_Q_LORA = 512
MLA_KV_LORA = 256
MLA_NOPE = 128
MLA_ROPE = 64
MLA_V = 128
MLA_DQ = 256

RET_H = 4
RET_DK = 128
RET_DV = 128

NAT_H = 4
NAT_D = 128
NAT_ROWS = 8
NAT_COLS = 16

SWA_H = 8
SWA_KVH = 2
SWA_D = 64
SWA_WINDOW = 128

OFF_QA = 0
OFF_RQ = 512
OFF_RK = 1024
OFF_RV = 1536
OFF_NQ = 2048
OFF_NK = 2560
OFF_NV = 3072
OFF_SQ = 3584
OFF_GP = 4096
OFF_KVA = 6144
OFF_SK = 6400
OFF_SV = 6528
OFF_KR = 6656
N_MIX_USED = 6784
N_MIX = 6912

V7X_VMEM_BYTES = 64 * 1024 * 1024


def _cp(sem, vmem_mb):
    assert vmem_mb * 1024 * 1024 < V7X_VMEM_BYTES
    return pltpu.CompilerParams(dimension_semantics=sem, vmem_limit_bytes=vmem_mb * 1024 * 1024)


def _sigmoid(x):
    return 1.0 / (1.0 + jnp.exp(-x))


def _silu(x):
    return x * _sigmoid(x)


def _swap_halves(x):
    n = x.shape[-1]
    ax = x.ndim - 1
    lane = lax.broadcasted_iota(jnp.int32, x.shape, ax)
    return jnp.where((lane & 32) == 0, pltpu.roll(x, n - 32, ax), pltpu.roll(x, 32, ax))


def _dot_nt(a, b):
    return lax.dot_general(a, b, (((1,), (1,)), ((), ())), preferred_element_type=F32)


def _dot(a, b):
    return jnp.dot(a, b, preferred_element_type=F32)


def _mod_body(c_ref, w_ref, b_ref, o_ref):
    a = _silu(c_ref[...]).astype(BF16)
    o_ref[0] = _dot(a, w_ref[0].astype(BF16)) + b_ref[0]


def _modulation(conds, w_mod, b_mod):
    depth, d, n = w_mod.shape
    rows = conds.shape[0]
    tn = 1024 if n % 1024 == 0 else n
    return pl.pallas_call(
        _mod_body,
        grid=(depth, n // tn),
        in_specs=[pl.BlockSpec((rows, d), lambda l, j: (0, 0)),
                  pl.BlockSpec((1, d, tn), lambda l, j: (l, 0, j)),
                  pl.BlockSpec((1, 1, tn), lambda l, j: (l, 0, j))],
        out_specs=pl.BlockSpec((1, rows, tn), lambda l, j: (l, 0, j)),
        out_shape=jax.ShapeDtypeStruct((depth, rows, n), F32),
        compiler_params=_cp(("parallel", "parallel"), 40),
        name="modulation",
    )(conds, w_mod, b_mod.reshape(depth, 1, n))


def _modnorm_body(x_ref, g_ref, sh_ref, sc_ref, h_ref):
    x = x_ref[0]
    y = x * lax.rsqrt(jnp.mean(x * x, axis=-1, keepdims=True) + EPS)
    y = y * g_ref[...]
    h_ref[0] = (y * (1.0 + sc_ref[0]) + sh_ref[0]).astype(BF16)


def _modnorm(x, g, shift, scale):
    b, s, d = x.shape
    ts = min(s, 512)
    per_b = shift.shape[0] > 1
    cidx = (lambda bi, i: (bi, 0, 0)) if per_b else (lambda bi, i: (0, 0, 0))
    return pl.pallas_call(
        _modnorm_body,
        grid=(b, s // ts),
        in_specs=[pl.BlockSpec((1, ts, d), lambda bi, i: (bi, i, 0)),
                  pl.BlockSpec((1, d), lambda bi, i: (0, 0)),
                  pl.BlockSpec((1, 1, d), cidx),
                  pl.BlockSpec((1, 1, d), cidx)],
        out_specs=pl.BlockSpec((1, ts, d), lambda bi, i: (bi, i, 0)),
        out_shape=jax.ShapeDtypeStruct((b, s, d), BF16),
        compiler_params=_cp(("parallel", "parallel"), 32),
        name="modnorm",
    )(x, g.reshape(1, d), shift, scale)


def _mm_body(a_ref, b_ref, o_ref):
    o_ref[...] = _dot(a_ref[...], b_ref[...])


def _matmul(a, b, tm, tn):
    m, k = a.shape
    n = b.shape[1]
    tm = min(tm, m)
    return pl.pallas_call(
        _mm_body,
        grid=(m // tm, n // tn),
        in_specs=[pl.BlockSpec((tm, k), lambda i, j: (i, 0)),
                  pl.BlockSpec((k, tn), lambda i, j: (0, j))],
        out_specs=pl.BlockSpec((tm, tn), lambda i, j: (i, j)),
        out_shape=jax.ShapeDtypeStruct((m, n), F32),
        compiler_params=_cp(("parallel", "arbitrary"), 40),
        name="in_proj",
    )(a, b)


def _mla_prep_body(*refs, rope):
    if rope:
        (qa_ref, kva_ref, kr_ref, gq_ref, gkv_ref, wq_ref, wk_ref, wv_ref, p_ref,
         cq_ref, sq_ref, ck_ref, sk_ref, q_out, k_out, v_out, ckv_out) = refs
    else:
        (qa_ref, kva_ref, kr_ref, gq_ref, gkv_ref, wq_ref, wk_ref, wv_ref, p_ref,
         q_out, k_out, v_out, ckv_out) = refs
    qa = qa_ref[0]
    cq = qa * lax.rsqrt(jnp.mean(qa * qa, axis=-1, keepdims=True) + EPS) * gq_ref[...]
    q = _dot(cq.astype(BF16), wq_ref[...])
    kva = kva_ref[0]
    ckv = kva * lax.rsqrt(jnp.mean(kva * kva, axis=-1, keepdims=True) + EPS) * gkv_ref[...]
    ckv_out[0] = ckv
    kr = kr_ref[0]
    if rope:
        cq_t = jnp.concatenate([cq_ref[...]] * MLA_H, axis=-1)
        sq_t = jnp.concatenate([sq_ref[...]] * MLA_H, axis=-1)
        q = q * cq_t + _swap_halves(q) * sq_t
        kr = kr * ck_ref[...] + _swap_halves(kr) * sk_ref[...]
    q_out[0] = q.astype(BF16)
    cb = ckv.astype(BF16)
    k_out[0] = (_dot(cb, wk_ref[...]) + _dot(kr.astype(BF16), p_ref[...])).astype(BF16)
    v_out[0] = _dot(cb, wv_ref[...]).astype(BF16)


def _mla_prep(z, gq, gkv, wq, wk, wv, p128, tabs):
    b, s, _ = z.shape
    tm = min(s, 512)
    rope = tabs is not None
    full = lambda shape: pl.BlockSpec(shape, lambda bi, i: (0,) * len(shape))
    in_specs = [pl.BlockSpec((1, tm, MLA_Q_LORA), lambda bi, i: (bi, i, OFF_QA // MLA_Q_LORA)),
                pl.BlockSpec((1, tm, MLA_KV_LORA), lambda bi, i: (bi, i, OFF_KVA // MLA_KV_LORA)),
                pl.BlockSpec((1, tm, 128), lambda bi, i: (bi, i, OFF_KR // 128)),
                full((1, MLA_Q_LORA)), full((1, MLA_KV_LORA)),
                full(wq.shape), full(wk.shape), full(wv.shape), full(p128.shape)]
    args = [z, z, z, gq.reshape(1, -1), gkv.reshape(1, -1), wq, wk, wv, p128]
    if rope:
        cq, sq, ck, sk = tabs
        in_specs += [pl.BlockSpec((tm, MLA_DQ), lambda bi, i: (i, 0))] * 2
        in_specs += [pl.BlockSpec((tm, 128), lambda bi, i: (i, 0))] * 2
        args += [cq, sq, ck, sk]
    hq = MLA_H * MLA_DQ
    hv = MLA_H * MLA_V
    return pl.pallas_call(
        functools.partial(_mla_prep_body, rope=rope),
        grid=(b, s // tm),
        in_specs=in_specs,
        out_specs=[pl.BlockSpec((1, tm, hq), lambda bi, i: (bi, i, 0)),
                   pl.BlockSpec((1, tm, hq), lambda bi, i: (bi, i, 0)),
                   pl.BlockSpec((1, tm, hv), lambda bi, i: (bi, i, 0)),
                   pl.BlockSpec((1, tm, MLA_KV_LORA), lambda bi, i: (bi, i, 0))],
        out_shape=[jax.ShapeDtypeStruct((b, s, hq), BF16),
                   jax.ShapeDtypeStruct((b, s, hq), BF16),
                   jax.ShapeDtypeStruct((b, s, hv), BF16),
                   jax.ShapeDtypeStruct((b, s, MLA_KV_LORA), F32)],
        compiler_params=_cp(("parallel", "parallel"), 40),
        name="mla_prep",
    )(*args)


def _mla_expand_body(ckv_ref, kr_ref, wk_ref, wv_ref, p_ref, k_out, v_out):
    cb = ckv_ref[0, 0].astype(BF16)
    k_out[0] = (_dot(cb, wk_ref[...]) + _dot(kr_ref[0, 0].astype(BF16), p_ref[...])).astype(BF16)
    v_out[0] = _dot(cb, wv_ref[...]).astype(BF16)


def _mla_expand(cache_ckv, cache_kr, layer, wk, wv, p64):
    b, _, s, _ = cache_ckv.shape
    full = lambda shape: pl.BlockSpec(shape, lambda bi: (0,) * len(shape))
    hq = MLA_H * MLA_DQ
    hv = MLA_H * MLA_V
    return pl.pallas_call(
        _mla_expand_body,
        grid=(b,),
        in_specs=[pl.BlockSpec((1, 1, s, MLA_KV_LORA), lambda bi: (bi, layer, 0, 0)),
                  pl.BlockSpec((1, 1, s, MLA_ROPE), lambda bi: (bi, layer, 0, 0)),
                  full(wk.shape), full(wv.shape), full(p64.shape)],
        out_specs=[pl.BlockSpec((1, s, hq), lambda bi: (bi, 0, 0)),
                   pl.BlockSpec((1, s, hv), lambda bi: (bi, 0, 0))],
        out_shape=[jax.ShapeDtypeStruct((b, s, hq), BF16),
                   jax.ShapeDtypeStruct((b, s, hv), BF16)],
        compiler_params=_cp(("parallel",), 32),
        name="mla_expand",
    )(cache_ckv, cache_kr, wk, wv, p64)


def _flash_body(*refs, n_h, n_kvh, dq, dv, scale, has_sink, has_ctx, nkv):
    it = iter(refs)
    q_ref, k_ref, v_ref = next(it), next(it), next(it)
    kc_ref = vc_ref = sink_ref = None
    if has_ctx:
        kc_ref, vc_ref = next(it), next(it)
    if has_sink:
        sink_ref = next(it)
    gp_ref, o_ref, m_scr, l_scr, acc_scr = next(it), next(it), next(it), next(it), next(it)
    j = pl.program_id(2)
    group = n_h // n_kvh
    tq = q_ref.shape[1]

    def process(kr, vr):
        for h in range(n_h):
            g = h // group
            q = q_ref[0, :, h * dq:(h + 1) * dq].astype(BF16)
            k = kr[0, :, g * dq:(g + 1) * dq].astype(BF16)
            v = vr[0, :, g * dv:(g + 1) * dv].astype(BF16)
            s = _dot_nt(q, k) * scale
            m_prev = m_scr[h]
            m_new = jnp.maximum(m_prev, jnp.max(s, axis=-1, keepdims=True))
            alpha = jnp.exp(m_prev - m_new)
            p = jnp.exp(s - m_new)
            l_scr[h] = alpha * l_scr[h] + jnp.sum(p, axis=-1, keepdims=True)
            acc_scr[h] = alpha * acc_scr[h] + _dot(p.astype(BF16), v)
            m_scr[h] = m_new

    @pl.when(j == 0)
    def _():
        if has_sink:
            for h in range(n_h):
                m_scr[h] = jnp.broadcast_to(sink_ref[0:1, h:h + 1], (tq, 1))
            l_scr[...] = jnp.ones(l_scr.shape, F32)
        else:
            m_scr[...] = jnp.full(m_scr.shape, NEG_INF, F32)
            l_scr[...] = jnp.zeros(l_scr.shape, F32)
        acc_scr[...] = jnp.zeros(acc_scr.shape, F32)
        if has_ctx:
            process(kc_ref, vc_ref)

    process(k_ref, v_ref)

    @pl.when(j == nkv - 1)
    def _():
        o = jnp.concatenate([acc_scr[h] / l_scr[h] for h in range(n_h)], axis=-1)
        o_ref[0] = (o * _silu(gp_ref[0])).astype(BF16)


def _flash(q, qcol, k, kcol, v, vcol, gp, gpcol, *, n_h, n_kvh, dq, dv, scale,
           sink=None, ctx=None, tq=512, tk=512):
    b, sq, _ = q.shape
    sk = k.shape[1]
    tq = min(tq, sq)
    tk = min(tk, sk)
    nkv = sk // tk
    in_specs = [pl.BlockSpec((1, tq, n_h * dq), lambda bi, i, j: (bi, i, qcol)),
                pl.BlockSpec((1, tk, n_kvh * dq), lambda bi, i, j: (bi, j, kcol)),
                pl.BlockSpec((1, tk, n_kvh * dv), lambda bi, i, j: (bi, j, vcol))]
    args = [q, k, v]
    if ctx is not None:
        kc, vc = ctx
        in_specs += [pl.BlockSpec((1,) + kc.shape[1:], lambda bi, i, j: (bi, 0, 0)),
                     pl.BlockSpec((1,) + vc.shape[1:], lambda bi, i, j: (bi, 0, 0))]
        args += [kc, vc]
    if sink is not None:
        in_specs += [pl.BlockSpec((1, n_h), lambda bi, i, j: (0, 0))]
        args += [sink.reshape(1, n_h)]
    in_specs += [pl.BlockSpec((1, tq, n_h * dv), lambda bi, i, j: (bi, i, gpcol))]
    args += [gp]
    body = functools.partial(_flash_body, n_h=n_h, n_kvh=n_kvh, dq=dq, dv=dv, scale=scale,
                             has_sink=sink is not None, has_ctx=ctx is not None, nkv=nkv)
    return pl.pallas_call(
        body,
        grid=(b, sq // tq, nkv),
        in_specs=in_specs,
        out_specs=pl.BlockSpec((1, tq, n_h * dv), lambda bi, i, j: (bi, i, 0)),
        out_shape=jax.ShapeDtypeStruct((b, sq, n_h * dv), BF16),
        scratch_shapes=[pltpu.VMEM((n_h, tq, 1), F32), pltpu.VMEM((n_h, tq, 1), F32),
                        pltpu.VMEM((n_h, tq, dv), F32)],
        compiler_params=_cp(("parallel", "parallel", "arbitrary"), 48),
        name="flash_attn",
    )(*args)


def _ret_body(*refs, nc, has_s0, has_st):
    it = iter(refs)
    dec_ref, norm_ref, q_ref, k_ref, v_ref = next(it), next(it), next(it), next(it), next(it)
    s0_ref = next(it) if has_s0 else None
    gp_ref, o_ref = next(it), next(it)
    st_ref = next(it) if has_st else None
    of_scr, ob_scr = next(it), next(it)
    head = pl.program_id(1)

    d = dec_ref[0]
    lg_all = jnp.minimum(d, 0.0) - jnp.log1p(jnp.exp(-jnp.abs(d)))
    hsel = lax.broadcasted_iota(jnp.int32, d.shape, 1) == head
    lg = jnp.sum(jnp.where(hsel, lg_all, 0.0), axis=1, keepdims=True)
    lg_f, lg_b = lg[0:1, :], lg[1:2, :]

    row = lax.broadcasted_iota(jnp.int32, (BLK, BLK), 0).astype(F32)
    col = lax.broadcasted_iota(jnp.int32, (BLK, BLK), 1).astype(F32)
    a = lax.broadcasted_iota(jnp.int32, (BLK, 1), 0).astype(F32)
    dm_f = jnp.where(row - col >= 0, jnp.exp(lg_f * jnp.maximum(row - col, 0.0)), 0.0)
    dm_b = jnp.where(col - row > 0, jnp.exp(lg_b * jnp.maximum(col - row, 0.0)), 0.0)
    qd_f, kd_f, cd_f = jnp.exp(lg_f * (a + 1.0)), jnp.exp(lg_f * (BLK - 1.0 - a)), jnp.exp(lg_f * BLK)
    qd_b, kd_b, cd_b = jnp.exp(lg_b * (BLK - a)), jnp.exp(lg_b * a), jnp.exp(lg_b * BLK)
    kscale = RET_DK ** -0.5

    def chunk(off, st, dm, qd, kd, cd):
        q = q_ref[0, pl.ds(off, BLK), :]
        k = k_ref[0, pl.ds(off, BLK), :] * kscale
        v = v_ref[0, pl.ds(off, BLK), :].astype(BF16)
        inner = _dot_nt(q.astype(BF16), k.astype(BF16)) * dm
        o = _dot(inner.astype(BF16), v) + _dot((q * qd).astype(BF16), st.astype(BF16))
        st = st * cd + _dot(jnp.transpose(k * kd).astype(BF16), v)
        return o, st

    def step(j, carry):
        sf, sb = carry
        off_f = pl.multiple_of(j * BLK, BLK)
        o, sf = chunk(off_f, sf, dm_f, qd_f, kd_f, cd_f)
        of_scr[pl.ds(off_f, BLK), :] = o
        off_b = pl.multiple_of((nc - 1 - j) * BLK, BLK)
        o, sb = chunk(off_b, sb, dm_b, qd_b, kd_b, cd_b)
        ob_scr[pl.ds(off_b, BLK), :] = o
        return sf, sb

    if has_s0:
        init = (s0_ref[0, 0, 0, 0], s0_ref[0, 0, 1, 0])
    else:
        init = (jnp.zeros((RET_DK, RET_DV), F32), jnp.zeros((RET_DK, RET_DV), F32))
    sf, sb = lax.fori_loop(0, nc, step, init)
    if has_st:
        st_ref[0, 0, 0] = sf
        st_ref[0, 1, 0] = sb
    o = of_scr[...] + ob_scr[...]
    o = o * lax.rsqrt(jnp.mean(o * o, axis=-1, keepdims=True) + EPS) * norm_ref[...]
    o_ref[0] = (o * _silu(gp_ref[0])).astype(BF16)


def _retention(z, decay, norm, s0, layer, want_state):
    b, s, _ = z.shape
    nc = s // BLK
    colspec = lambda blk: pl.BlockSpec((1, s, 128), lambda bi, h: (bi, 0, blk + h))
    in_specs = [pl.BlockSpec((1, 2, RET_H), lambda bi, h: (0, 0, 0)),
                pl.BlockSpec((1, RET_DV), lambda bi, h: (0, h)),
                colspec(OFF_RQ // 128), colspec(OFF_RK // 128), colspec(OFF_RV // 128)]
    args = [decay.reshape(1, 2, RET_H), norm.reshape(1, -1), z, z, z]
    if s0 is not None:
        in_specs.append(pl.BlockSpec((1, 1, 2, 1, RET_DK, RET_DV), lambda bi, h: (bi, layer, 0, h, 0, 0)))
        args.append(s0)
    in_specs.append(colspec((OFF_GP + BR_W) // 128))
    args.append(z)
    out_specs = [pl.BlockSpec((1, s, RET_DV), lambda bi, h: (bi, 0, h))]
    out_shape = [jax.ShapeDtypeStruct((b, s, RET_H * RET_DV), BF16)]
    if want_state:
        out_specs.append(pl.BlockSpec((1, 2, 1, RET_DK, RET_DV), lambda bi, h: (bi, 0, h, 0, 0)))
        out_shape.append(jax.ShapeDtypeStruct((b, 2, RET_H, RET_DK, RET_DV), F32))
    res = pl.pallas_call(
        functools.partial(_ret_body, nc=nc, has_s0=s0 is not None, has_st=want_state),
        grid=(b, RET_H),
        in_specs=in_specs,
        out_specs=out_specs,
        out_shape=out_shape,
        scratch_shapes=[pltpu.VMEM((s, RET_DV), F32), pltpu.VMEM((s, RET_DV), F32)],
        compiler_params=_cp(("parallel", "parallel"), 48),
        name="retention",
    )(*args)
    return (res[0], res[1]) if want_state else (res[0], None)


def _nat_body(q_ref, k_ref, v_ref, kc_ref, vc_ref, bias_ref, gp_ref, o_ref, kb_scr, vb_scr, o_scr, *, rows):
    scale = NAT_D ** -0.5
    kb_scr[...] = k_ref[0].astype(BF16)
    vb_scr[...] = v_ref[0].astype(BF16)
    kc = kc_ref[0, 0].astype(BF16)
    vc = vc_ref[0, 0].astype(BF16)
    win = NAT_ROWS * GRID_W

    def step(r, _):
        rs = jnp.clip(r - NAT_ROWS // 2, 0, rows - NAT_ROWS)
        qoff = pl.multiple_of(r * GRID_W, GRID_W)
        koff = pl.multiple_of(rs * GRID_W, GRID_W)
        q = q_ref[0, pl.ds(qoff, GRID_W), :].astype(BF16)
        kw = kb_scr[pl.ds(koff, win), :]
        vw = vb_scr[pl.ds(koff, win), :]
        s_loc = _dot_nt(q, kw) * scale + bias_ref[0, rs - r + (NAT_ROWS - 1)]
        s_ctx = _dot_nt(q, kc) * scale
        m = jnp.maximum(jnp.max(s_loc, axis=-1, keepdims=True), jnp.max(s_ctx, axis=-1, keepdims=True))
        p_loc = jnp.exp(s_loc - m)
        p_ctx = jnp.exp(s_ctx - m)
        den = jnp.sum(p_loc, axis=-1, keepdims=True) + jnp.sum(p_ctx, axis=-1, keepdims=True)
        o = _dot(p_loc.astype(BF16), vw) + _dot(p_ctx.astype(BF16), vc)
        o_scr[pl.ds(qoff, GRID_W), :] = o / den
        return 0

    lax.fori_loop(0, rows, step, 0)
    o_ref[0] = (o_scr[...] * _silu(gp_ref[0])).astype(BF16)


def _nat_bias_table(rpb):
    c = jnp.arange(GRID_W)[:, None]
    kc = jnp.arange(GRID_W)[None, :]
    cs = jnp.clip(c - NAT_COLS // 2, 0, GRID_W - NAT_COLS)
    valid = (kc >= cs) & (kc < cs + NAT_COLS)
    co = jnp.clip(kc - c + (NAT_COLS - 1), 0, 2 * NAT_COLS - 2)
    ro = jnp.arange(NAT_ROWS)[:, None] + jnp.arange(NAT_ROWS)[None, :]
    t = rpb[:, ro[:, :, None, None], co[None, None, :, :]]
    t = jnp.where(valid[None, None, None], t.astype(F32), NEG_INF)
    t = jnp.transpose(t, (0, 1, 3, 2, 4))
    return t.reshape(rpb.shape[0], NAT_ROWS, GRID_W, NAT_ROWS * GRID_W)


def _nat(z, cache_k, cache_v, layer, bias):
    b, s, _ = z.shape
    rows = s // GRID_W
    lc = cache_k.shape[2]
    colspec = lambda blk: pl.BlockSpec((1, s, NAT_D), lambda bi, h: (bi, 0, blk + h))
    cspec = pl.BlockSpec((1, 1, lc, NAT_D), lambda bi, h: (bi, layer, 0, h))
    return pl.pallas_call(
        functools.partial(_nat_body, rows=rows),
        grid=(b, NAT_H),
        in_specs=[colspec(OFF_NQ // 128), colspec(OFF_NK // 128), colspec(OFF_NV // 128), cspec, cspec,
                  pl.BlockSpec((1,) + bias.shape[1:], lambda bi, h: (h, 0, 0, 0)),
                  colspec((OFF_GP + 2 * BR_W) // 128)],
        out_specs=pl.BlockSpec((1, s, NAT_D), lambda bi, h: (bi, 0, h)),
        out_shape=jax.ShapeDtypeStruct((b, s, NAT_H * NAT_D), BF16),
        scratch_shapes=[pltpu.VMEM((s, NAT_D), BF16), pltpu.VMEM((s, NAT_D), BF16),
                        pltpu.VMEM((s, NAT_D), F32)],
        compiler_params=_cp(("parallel", "parallel"), 48),
        name="nat_attn",
    )(z, z, z, cache_k, cache_v, bias, z)


def _swa_body(q_ref, k_ref, v_ref, kc_ref, vc_ref, sink_ref, cos_ref, sin_ref, gp_ref, o_ref, *, seq):
    i = pl.program_id(1)
    scale = SWA_D ** -0.5
    group = SWA_H // SWA_KVH
    win = 3 * BLK
    start = pl.multiple_of(jnp.clip((i - 1) * BLK, 0, seq - win), BLK)
    qoff = pl.multiple_of(i * BLK, BLK)

    cq = cos_ref[pl.ds(qoff, BLK), :]
    sq = sin_ref[pl.ds(qoff, BLK), :]
    q = q_ref[0]
    q = q * jnp.concatenate([cq] * (SWA_H // 2), axis=-1) + _swap_halves(q) * jnp.concatenate([sq] * (SWA_H // 2), axis=-1)
    q = q.astype(BF16)
    kw = k_ref[0, pl.ds(start, win), :]
    kw = (kw * cos_ref[pl.ds(start, win), :] + _swap_halves(kw) * sin_ref[pl.ds(start, win), :]).astype(BF16)
    vw = v_ref[0, pl.ds(start, win), :].astype(BF16)
    kc = kc_ref[0, 0].astype(BF16)
    vc = vc_ref[0, 0].astype(BF16)

    qpos = qoff + lax.broadcasted_iota(jnp.int32, (BLK, win), 0)
    kpos = start + lax.broadcasted_iota(jnp.int32, (BLK, win), 1)
    valid = jnp.abs(qpos - kpos) <= SWA_WINDOW

    outs = []
    for h in range(SWA_H):
        g = h // group
        qh = q[:, h * SWA_D:(h + 1) * SWA_D]
        s_loc = jnp.where(valid, _dot_nt(qh, kw[:, g * SWA_D:(g + 1) * SWA_D]) * scale, NEG_INF)
        s_ctx = _dot_nt(qh, kc[:, g * SWA_D:(g + 1) * SWA_D]) * scale
        sink = sink_ref[0:1, h:h + 1]
        m = jnp.maximum(jnp.max(s_loc, axis=-1, keepdims=True), jnp.max(s_ctx, axis=-1, keepdims=True))
        m = jnp.maximum(m, sink)
        p_loc = jnp.exp(s_loc - m)
        p_ctx = jnp.exp(s_ctx - m)
        den = (jnp.sum(p_loc, axis=-1, keepdims=True) + jnp.sum(p_ctx, axis=-1, keepdims=True)
               + jnp.exp(sink - m))
        o = (_dot(p_loc.astype(BF16), vw[:, g * SWA_D:(g + 1) * SWA_D])
             + _dot(p_ctx.astype(BF16), vc[:, g * SWA_D:(g + 1) * SWA_D]))
        outs.append(o / den)
    o = jnp.concatenate(outs, axis=-1)
    o_ref[0] = (o * _silu(gp_ref[0])).astype(BF16)


def _swa(z, cache_k, cache_v, layer, sink, cos2, sin2):
    b, s, _ = z.shape
    lc = cache_k.shape[2]
    hd = SWA_H * SWA_D
    kd = SWA_KVH * SWA_D
    cspec = pl.BlockSpec((1, 1, lc, kd), lambda bi, i: (bi, layer, 0, 0))
    return pl.pallas_call(
        functools.partial(_swa_body, seq=s),
        grid=(b, s // BLK),
        in_specs=[pl.BlockSpec((1, BLK, hd), lambda bi, i: (bi, i, OFF_SQ // hd)),
                  pl.BlockSpec((1, s, kd), lambda bi, i: (bi, 0, OFF_SK // kd)),
                  pl.BlockSpec((1, s, kd), lambda bi, i: (bi, 0, OFF_SV // kd)),
                  cspec, cspec,
                  pl.BlockSpec((1, SWA_H), lambda bi, i: (0, 0)),
                  pl.BlockSpec((s, 128), lambda bi, i: (0, 0)),
                  pl.BlockSpec((s, 128), lambda bi, i: (0, 0)),
                  pl.BlockSpec((1, BLK, hd), lambda bi, i: (bi, i, (OFF_GP + 3 * BR_W) // hd))],
        out_specs=pl.BlockSpec((1, BLK, hd), lambda bi, i: (bi, i, 0)),
        out_shape=jax.ShapeDtypeStruct((b, s, hd), BF16),
        compiler_params=_cp(("parallel", "arbitrary"), 48),
        name="swa_attn",
    )(z, z, z, cache_k, cache_v, sink.reshape(1, SWA_H), cos2, sin2, z)


def _merge_body(h_ref, o0_ref, o1_ref, o2_ref, o3_ref, g0_ref, g1_ref, g2_ref, g3_ref, wb_ref, y_ref):
    h = h_ref[...]
    o_refs = (o0_ref, o1_ref, o2_ref, o3_ref)
    g_refs = (g0_ref, g1_ref, g2_ref, g3_ref)
    y = None
    for n in range(N_BRANCH):
        t = _sigmoid(_dot(h, g_refs[n][...])) * _dot(o_refs[n][...], wb_ref[n])
        y = t if y is None else y + t
    y_ref[...] = y.astype(BF16)


def _merge(h, ogs, w_gate, w_branch, tm, tn):
    m, d = h.shape
    tm = min(tm, m)
    nj = d // tn
    gspec = lambda n: pl.BlockSpec((d, tn), lambda i, j: (0, n * nj + j))
    return pl.pallas_call(
        _merge_body,
        grid=(m // tm, nj),
        in_specs=[pl.BlockSpec((tm, d), lambda i, j: (i, 0))]
        + [pl.BlockSpec((tm, BR_W), lambda i, j: (i, 0))] * N_BRANCH
        + [gspec(n) for n in range(N_BRANCH)]
        + [pl.BlockSpec((N_BRANCH, BR_W, tn), lambda i, j: (0, 0, j))],
        out_specs=pl.BlockSpec((tm, tn), lambda i, j: (i, j)),
        out_shape=jax.ShapeDtypeStruct((m, d), BF16),
        compiler_params=_cp(("parallel", "arbitrary"), 48),
        name="merge",
    )(h, *ogs, w_gate, w_gate, w_gate, w_gate, w_branch)


def _out_body(y_ref, w_ref, x_ref, gate_ref, g_ref, o_ref):
    t = _dot(y_ref[0], w_ref[...])
    t = t * lax.rsqrt(jnp.mean(t * t, axis=-1, keepdims=True) + EPS) * g_ref[...]
    o_ref[0] = x_ref[0] + gate_ref[0] * t


def _out_proj(y, w_out, x, gate, g):
    b, s, d = x.shape
    ts = min(s, 256)
    per_b = gate.shape[0] > 1
    cidx = (lambda bi, i: (bi, 0, 0)) if per_b else (lambda bi, i: (0, 0, 0))
    return pl.pallas_call(
        _out_body,
        grid=(b, s // ts),
        in_specs=[pl.BlockSpec((1, ts, d), lambda bi, i: (bi, i, 0)),
                  pl.BlockSpec((d, d), lambda bi, i: (0, 0)),
                  pl.BlockSpec((1, ts, d), lambda bi, i: (bi, i, 0)),
                  pl.BlockSpec((1, 1, d), cidx),
                  pl.BlockSpec((1, d), lambda bi, i: (0, 0))],
        out_specs=pl.BlockSpec((1, ts, d), lambda bi, i: (bi, i, 0)),
        out_shape=jax.ShapeDtypeStruct((b, s, d), F32),
        compiler_params=_cp(("parallel", "parallel"), 48),
        name="out_proj",
    )(y, w_out, x, gate, g.reshape(1, d))


def _mix_weight(w_in):
    c = {}
    o = 0
    for name, w in (("qa", MLA_Q_LORA), ("kva", MLA_KV_LORA), ("kr", MLA_ROPE),
                    ("rq", 512), ("rk", 512), ("rv", 512), ("nq", 512), ("nk", 512), ("nv", 512),
                    ("sq", 512), ("sk", 128), ("sv", 128), ("gp", N_BRANCH * BR_W)):
        c[name] = w_in[:, :, o:o + w]
        o += w
    depth, d, _ = w_in.shape
    order = ["qa", "rq", "rk", "rv", "nq", "nk", "nv", "sq", "gp", "kva", "sk", "sv", "kr"]
    parts = [c[n] for n in order] + [jnp.zeros((depth, d, N_MIX - N_MIX_USED + MLA_ROPE), w_in.dtype)]
    return jnp.concatenate(parts, axis=-1).astype(BF16), o


def _mla_weights(w_q_up, w_kv_up):
    depth = w_q_up.shape[0]
    wq = w_q_up.reshape(depth, MLA_Q_LORA, MLA_H, MLA_NOPE + MLA_ROPE)
    wq = jnp.pad(wq, ((0, 0), (0, 0), (0, 0), (0, MLA_DQ - MLA_NOPE - MLA_ROPE)))
    wq = wq.reshape(depth, MLA_Q_LORA, MLA_H * MLA_DQ).astype(BF16)
    wkv = w_kv_up.reshape(depth, MLA_KV_LORA, MLA_H, MLA_NOPE + MLA_V)
    wk = jnp.pad(wkv[..., :MLA_NOPE], ((0, 0), (0, 0), (0, 0), (0, MLA_DQ - MLA_NOPE)))
    wk = wk.reshape(depth, MLA_KV_LORA, MLA_H * MLA_DQ).astype(BF16)
    wv = wkv[..., MLA_NOPE:].reshape(depth, MLA_KV_LORA, MLA_H * MLA_V).astype(BF16)
    j = jnp.arange(128)[:, None]
    cidx = jnp.arange(MLA_H * MLA_DQ)[None, :]
    p128 = ((cidx % MLA_DQ == MLA_NOPE + j) & (j < MLA_ROPE)).astype(BF16)
    return wq, wk, wv, p128


def _rope_tables(n_tok):
    pos = jnp.arange(n_tok)
    row = (pos // GRID_W).astype(F32)
    col = (pos % GRID_W).astype(F32)
    n_freq = MLA_ROPE // 4
    inv = ROPE_BASE ** (-jnp.arange(n_freq, dtype=F32) / n_freq)
    ang = jnp.concatenate([row[:, None] * inv[None], col[:, None] * inv[None]], axis=-1)
    cos, sin = jnp.cos(ang), jnp.sin(ang)
    c64 = jnp.concatenate([cos, cos], axis=-1)
    s64 = jnp.concatenate([-sin, sin], axis=-1)
    one, zero = jnp.ones_like(c64), jnp.zeros_like(c64)
    cq = jnp.concatenate([one, one, c64, one], axis=-1)
    sq = jnp.concatenate([zero, zero, s64, zero], axis=-1)
    ck = jnp.concatenate([c64, one], axis=-1)
    sk = jnp.concatenate([s64, zero], axis=-1)
    c2 = jnp.concatenate([c64, c64], axis=-1)
    s2 = jnp.concatenate([s64, s64], axis=-1)
    return (cq, sq, ck, sk), (c2, s2)


def _layer_common(x, h_args, w_mix_l, tm_in):
    b, s, d = x.shape
    h = _modnorm(x, *h_args)
    z = _matmul(h.reshape(b * s, d), w_mix_l, tm_in, 768).reshape(b, s, N_MIX)
    return h, z


def kernel(x_prompt, x_sample, cache_mla_ckv, cache_mla_krope, state_ret, cache_nat_k, cache_nat_v,
           cache_swa_k, cache_swa_v, c, c_ctx, w_mod, b_mod, norm_pre, norm_post, w_in, mla_q_norm,
           mla_kv_norm, mla_w_q_up, mla_w_kv_up, ret_decay, ret_norm, nat_rpb, swa_sink, w_branch, w_out):
    depth = w_mod.shape[0]
    d = x_prompt.shape[-1]
    nb = x_sample.shape[0]
    s_lat = x_sample.shape[1]
    l_ctx = cache_mla_ckv.shape[2]

    w_mix, mix_cols = _mix_weight(w_in)
    w_gate = w_in[:, :, mix_cols:].astype(BF16)
    w_br = w_branch.astype(BF16)
    w_o = w_out.astype(BF16)
    wq, wk, wv, p128 = _mla_weights(mla_w_q_up, mla_w_kv_up)
    p64 = p128[:MLA_ROPE]
    mla_tabs, (cos2, sin2) = _rope_tables(s_lat)
    nat_k_c = cache_nat_k.reshape(nb, depth, l_ctx, NAT_H * NAT_D)
    nat_v_c = cache_nat_v.reshape(nb, depth, l_ctx, NAT_H * NAT_D)
    swa_k_c = cache_swa_k.reshape(nb, depth, l_ctx, SWA_KVH * SWA_D)
    swa_v_c = cache_swa_v.reshape(nb, depth, l_ctx, SWA_KVH * SWA_D)

    rows = 8
    conds = jnp.concatenate([c_ctx[None, :], c, jnp.zeros((rows - 1 - nb, d), F32)], axis=0)
    mod = _modulation(conds, w_mod, b_mod)

    def mod_parts(l, lo, hi):
        m = mod[l, lo:hi]
        return m[:, None, :d], m[:, None, d:2 * d], m[:, None, 2 * d:]

    yp = x_prompt
    bc, sc, _ = yp.shape
    ckv_l, kr_l, st_l, nk_l, nv_l, sk_l, sv_l = [], [], [], [], [], [], []
    for l in range(depth):
        shift, scale, gate = mod_parts(l, 0, 1)
        h, z = _layer_common(yp, (norm_pre[l], shift, scale), w_mix[l], 1024)
        q, kf, vf, ckv = _mla_prep(z, mla_q_norm[l], mla_kv_norm[l], wq[l], wk[l], wv[l], p128, None)
        og_mla = _flash(q, 0, kf, 0, vf, 0, z, OFF_GP // BR_W, n_h=MLA_H, n_kvh=MLA_H, dq=MLA_DQ, dv=MLA_V,
                        scale=(MLA_NOPE + MLA_ROPE) ** -0.5)
        og_ret, st = _retention(z, ret_decay[l], ret_norm[l], None, 0, True)
        og_nat = _flash(z, OFF_NQ // 512, z, OFF_NK // 512, z, OFF_NV // 512, z, (OFF_GP + 2 * BR_W) // BR_W,
                        n_h=NAT_H, n_kvh=NAT_H, dq=NAT_D, dv=NAT_D, scale=NAT_D ** -0.5)
        og_swa = _flash(z, OFF_SQ // 512, z, OFF_SK // 128, z, OFF_SV // 128, z, (OFF_GP + 3 * BR_W) // BR_W,
                        n_h=SWA_H, n_kvh=SWA_KVH, dq=SWA_D, dv=SWA_D, scale=SWA_D ** -0.5, sink=swa_sink[l])
        ogs = [o.reshape(bc * sc, BR_W) for o in (og_mla, og_ret, og_nat, og_swa)]
        y = _merge(h.reshape(bc * sc, d), ogs, w_gate[l], w_br[l], 1024, 256)
        yp = _out_proj(y.reshape(bc, sc, d), w_o[l], yp, gate, norm_post[l])
        ckv_l.append(ckv)
        kr_l.append(z[:, :, OFF_KR:OFF_KR + MLA_ROPE])
        st_l.append(st)
        nk_l.append(z[:, :, OFF_NK:OFF_NK + 512].reshape(bc, sc, NAT_H, NAT_D))
        nv_l.append(z[:, :, OFF_NV:OFF_NV + 512].reshape(bc, sc, NAT_H, NAT_D))
        sk_l.append(z[:, :, OFF_SK:OFF_SK + 128].reshape(bc, sc, SWA_KVH, SWA_D))
        sv_l.append(z[:, :, OFF_SV:OFF_SV + 128].reshape(bc, sc, SWA_KVH, SWA_D))

    ys = x_sample
    for l in range(depth):
        shift, scale, gate = mod_parts(l, 1, 1 + nb)
        h, z = _layer_common(ys, (norm_pre[l], shift, scale), w_mix[l], 1024)
        q, kf, vf, _ = _mla_prep(z, mla_q_norm[l], mla_kv_norm[l], wq[l], wk[l], wv[l], p128, mla_tabs)
        kc, vc = _mla_expand(cache_mla_ckv, cache_mla_krope, l, wk[l], wv[l], p64)
        og_mla = _flash(q, 0, kf, 0, vf, 0, z, OFF_GP // BR_W, n_h=MLA_H, n_kvh=MLA_H, dq=MLA_DQ, dv=MLA_V,
                        scale=(MLA_NOPE + MLA_ROPE) ** -0.5, ctx=(kc, vc))
        og_ret, _ = _retention(z, ret_decay[l], ret_norm[l], state_ret, l, False)
        og_nat = _nat(z, nat_k_c, nat_v_c, l, _nat_bias_table(nat_rpb[l]))
        og_swa = _swa(z, swa_k_c, swa_v_c, l, swa_sink[l], cos2, sin2)
        ogs = [o.reshape(nb * s_lat, BR_W) for o in (og_mla, og_ret, og_nat, og_swa)]
        y = _merge(h.reshape(nb * s_lat, d), ogs, w_gate[l], w_br[l], 1024, 256)
        ys = _out_proj(y.reshape(nb, s_lat, d), w_o[l], ys, gate, norm_post[l])

    return (yp, ys, jnp.stack(ckv_l, axis=1), jnp.stack(kr_l, axis=1), jnp.stack(st_l, axis=1),
            jnp.stack(nk_l, axis=1), jnp.stack(nv_l, axis=1), jnp.stack(sk_l, axis=1), jnp.stack(sv_l, axis=1))
```

```python
import functools

import jax
import jax.numpy as jnp
from jax import lax
from jax.experimental import pallas as pl
from jax.experimental.pallas import tpu as pltpu

F32 = jnp.float32
BF16 = jnp.bfloat16

GRID_W = 64
BLK = 128
ROPE_BASE = 10000.0
EPS = 1e-6
NEG_INF = -1e30
N_BRANCH = 4
BR_W = 512

MLA_H = 4
MLA_Q_LORA = 512
MLA_KV_LORA = 256
MLA_NOPE = 128
MLA_ROPE = 64
MLA_V = 128
MLA_DQ = 256
MLA_DV = 256
MLA_QSCALE = (MLA_NOPE + MLA_ROPE) ** -0.5 * 1.4426950408889634

RET_H = 4
RET_DK = 128
RET_DV = 128

NAT_H = 4
NAT_D = 128
NAT_ROWS = 8
NAT_COLS = 16
NAT_QR = 4
NAT_WR = NAT_ROWS + NAT_QR

SWA_H = 8
SWA_KVH = 2
SWA_D = 64
SWA_WINDOW = 128

OFF_QA = 0
OFF_RQ = 512
OFF_RK = 1024
OFF_RV = 1536
OFF_NQ = 2048
OFF_NK = 2560
OFF_NV = 3072
OFF_SQ = 3584
OFF_GP = 4096
OFF_KVA = 6144
OFF_SK = 6400
OFF_SV = 6528
OFF_KR = 6656
N_MIX_USED = 6784
N_MIX = 6912

V7X_VMEM_BYTES = 64 * 1024 * 1024


def _cp(sem, vmem_mb):
    assert vmem_mb * 1024 * 1024 < V7X_VMEM_BYTES
    return pltpu.CompilerParams(dimension_semantics=sem, vmem_limit_bytes=vmem_mb * 1024 * 1024)


def _sigmoid(x):
    return 1.0 / (1.0 + jnp.exp(-x))


def _silu(x):
    return x * _sigmoid(x)


def _swap_halves(x):
    n = x.shape[-1]
    ax = x.ndim - 1
    lane = lax.broadcasted_iota(jnp.int32, x.shape, ax)
    return jnp.where((lane & 32) == 0, pltpu.roll(x, n - 32, ax), pltpu.roll(x, 32, ax))


def _dot_nt(a, b):
    return lax.dot_general(a, b, (((1,), (1,)), ((), ())), preferred_element_type=F32)


def _dot(a, b):
    return jnp.dot(a, b, preferred_element_type=F32)


def _mod_body(c_ref, w_ref, b_ref, o_ref):
    a = _silu(c_ref[...]).astype(BF16)
    o_ref[0] = _dot(a, w_ref[0].astype(BF16)) + b_ref[0]


def _modulation(conds, w_mod, b_mod):
    depth, d, n = w_mod.shape
    rows = conds.shape[0]
    tn = 1024 if n % 1024 == 0 else n
    return pl.pallas_call(
        _mod_body,
        grid=(depth, n // tn),
        in_specs=[pl.BlockSpec((rows, d), lambda l, j: (0, 0)),
                  pl.BlockSpec((1, d, tn), lambda l, j: (l, 0, j)),
                  pl.BlockSpec((1, 1, tn), lambda l, j: (l, 0, j))],
        out_specs=pl.BlockSpec((1, rows, tn), lambda l, j: (l, 0, j)),
        out_shape=jax.ShapeDtypeStruct((depth, rows, n), F32),
        compiler_params=_cp(("parallel", "parallel"), 40),
        name="modulation",
    )(conds, w_mod, b_mod.reshape(depth, 1, n))


def _modnorm_body(x_ref, g_ref, sh_ref, sc_ref, h_ref):
    x = x_ref[0]
    y = x * lax.rsqrt(jnp.mean(x * x, axis=-1, keepdims=True) + EPS)
    y = y * g_ref[...]
    h_ref[0] = (y * (1.0 + sc_ref[0]) + sh_ref[0]).astype(BF16)


def _modnorm(x, g, shift, scale):
    b, s, d = x.shape
    ts = min(s, 512)
    per_b = shift.shape[0] > 1
    cidx = (lambda bi, i: (bi, 0, 0)) if per_b else (lambda bi, i: (0, 0, 0))
    return pl.pallas_call(
        _modnorm_body,
        grid=(b, s // ts),
        in_specs=[pl.BlockSpec((1, ts, d), lambda bi, i: (bi, i, 0)),
                  pl.BlockSpec((1, d), lambda bi, i: (0, 0)),
                  pl.BlockSpec((1, 1, d), cidx),
                  pl.BlockSpec((1, 1, d), cidx)],
        out_specs=pl.BlockSpec((1, ts, d), lambda bi, i: (bi, i, 0)),
        out_shape=jax.ShapeDtypeStruct((b, s, d), BF16),
        compiler_params=_cp(("parallel", "parallel"), 32),
        name="modnorm",
    )(x, g.reshape(1, d), shift, scale)


def _mm_body(a_ref, b_ref, o_ref):
    o_ref[...] = _dot(a_ref[...], b_ref[...])


def _matmul(a, b, tm, tn):
    m, k = a.shape
    n = b.shape[1]
    tm = min(tm, m)
    return pl.pallas_call(
        _mm_body,
        grid=(m // tm, n // tn),
        in_specs=[pl.BlockSpec((tm, k), lambda i, j: (i, 0)),
                  pl.BlockSpec((k, tn), lambda i, j: (0, j))],
        out_specs=pl.BlockSpec((tm, tn), lambda i, j: (i, j)),
        out_shape=jax.ShapeDtypeStruct((m, n), F32),
        compiler_params=_cp(("parallel", "arbitrary"), 40),
        name="in_proj",
    )(a, b)


def _mla_prep_body(*refs, rope):
    if rope:
        (qa_ref, kva_ref, kr_ref, gq_ref, gkv_ref, wq_ref, wk_ref, wv_ref, p_ref, one_ref,
         cq_ref, sq_ref, ck_ref, sk_ref, q_out, k_out, v_out, ckv_out) = refs
    else:
        (qa_ref, kva_ref, kr_ref, gq_ref, gkv_ref, wq_ref, wk_ref, wv_ref, p_ref, one_ref,
         q_out, k_out, v_out, ckv_out) = refs
    qa = qa_ref[0]
    cq = qa * lax.rsqrt(jnp.mean(qa * qa, axis=-1, keepdims=True) + EPS) * gq_ref[...]
    q = _dot(cq.astype(BF16), wq_ref[...])
    kva = kva_ref[0]
    ckv = kva * lax.rsqrt(jnp.mean(kva * kva, axis=-1, keepdims=True) + EPS) * gkv_ref[...]
    ckv_out[0] = ckv
    kr = kr_ref[0]
    if rope:
        cq_t = jnp.concatenate([cq_ref[...]] * MLA_H, axis=-1)
        sq_t = jnp.concatenate([sq_ref[...]] * MLA_H, axis=-1)
        q = q * cq_t + _swap_halves(q) * sq_t
        kr = kr * ck_ref[...] + _swap_halves(kr) * sk_ref[...]
    q_out[0] = (q * MLA_QSCALE).astype(BF16)
    cb = ckv.astype(BF16)
    k_out[0] = (_dot(cb, wk_ref[...]) + _dot(kr.astype(BF16), p_ref[...])).astype(BF16)
    v_out[0] = (_dot(cb, wv_ref[...]) + one_ref[...]).astype(BF16)


def _mla_prep(z, gq, gkv, wq, wk, wv, p128, vones, tabs):
    b, s, _ = z.shape
    tm = min(s, 512)
    rope = tabs is not None
    full = lambda shape: pl.BlockSpec(shape, lambda bi, i: (0,) * len(shape))
    in_specs = [pl.BlockSpec((1, tm, MLA_Q_LORA), lambda bi, i: (bi, i, OFF_QA // MLA_Q_LORA)),
                pl.BlockSpec((1, tm, MLA_KV_LORA), lambda bi, i: (bi, i, OFF_KVA // MLA_KV_LORA)),
                pl.BlockSpec((1, tm, 128), lambda bi, i: (bi, i, OFF_KR // 128)),
                full((1, MLA_Q_LORA)), full((1, MLA_KV_LORA)),
                full(wq.shape), full(wk.shape), full(wv.shape), full(p128.shape), full(vones.shape)]
    args = [z, z, z, gq.reshape(1, -1), gkv.reshape(1, -1), wq, wk, wv, p128, vones]
    if rope:
        cq, sq, ck, sk = tabs
        in_specs += [pl.BlockSpec((tm, MLA_DQ), lambda bi, i: (i, 0))] * 2
        in_specs += [pl.BlockSpec((tm, 128), lambda bi, i: (i, 0))] * 2
        args += [cq, sq, ck, sk]
    hq = MLA_H * MLA_DQ
    hv = MLA_H * MLA_DV
    return pl.pallas_call(
        functools.partial(_mla_prep_body, rope=rope),
        grid=(b, s // tm),
        in_specs=in_specs,
        out_specs=[pl.BlockSpec((1, tm, hq), lambda bi, i: (bi, i, 0)),
                   pl.BlockSpec((1, tm, hq), lambda bi, i: (bi, i, 0)),
                   pl.BlockSpec((1, tm, hv), lambda bi, i: (bi, i, 0)),
                   pl.BlockSpec((1, tm, MLA_KV_LORA), lambda bi, i: (bi, i, 0))],
        out_shape=[jax.ShapeDtypeStruct((b, s, hq), BF16),
                   jax.ShapeDtypeStruct((b, s, hq), BF16),
                   jax.ShapeDtypeStruct((b, s, hv), BF16),
                   jax.ShapeDtypeStruct((b, s, MLA_KV_LORA), F32)],
        compiler_params=_cp(("parallel", "parallel"), 40),
        name="mla_prep",
    )(*args)


def _mla_expand_body(ckv_ref, kr_ref, wk_ref, wv_ref, p_ref, one_ref, k_out, v_out):
    cb = ckv_ref[0, 0].astype(BF16)
    k_out[0] = (_dot(cb, wk_ref[...]) + _dot(kr_ref[0, 0].astype(BF16), p_ref[...])).astype(BF16)
    v_out[0] = (_dot(cb, wv_ref[...]) + one_ref[...]).astype(BF16)


def _mla_expand(cache_ckv, cache_kr, layer, wk, wv, p64, vones):
    b, _, s, _ = cache_ckv.shape
    full = lambda shape: pl.BlockSpec(shape, lambda bi: (0,) * len(shape))
    hq = MLA_H * MLA_DQ
    hv = MLA_H * MLA_DV
    return pl.pallas_call(
        _mla_expand_body,
        grid=(b,),
        in_specs=[pl.BlockSpec((1, 1, s, MLA_KV_LORA), lambda bi: (bi, layer, 0, 0)),
                  pl.BlockSpec((1, 1, s, MLA_ROPE), lambda bi: (bi, layer, 0, 0)),
                  full(wk.shape), full(wv.shape), full(p64.shape), full(vones.shape)],
        out_specs=[pl.BlockSpec((1, s, hq), lambda bi: (bi, 0, 0)),
                   pl.BlockSpec((1, s, hv), lambda bi: (bi, 0, 0))],
        out_shape=[jax.ShapeDtypeStruct((b, s, hq), BF16),
                   jax.ShapeDtypeStruct((b, s, hv), BF16)],
        compiler_params=_cp(("parallel",), 32),
        name="mla_expand",
    )(cache_ckv, cache_kr, wk, wv, p64, vones)


def _mla_attn_body(*refs, has_ctx, nkv):
    it = iter(refs)
    q_ref, k_ref, v_ref = next(it), next(it), next(it)
    kc_ref = vc_ref = None
    if has_ctx:
        kc_ref, vc_ref = next(it), next(it)
    gp_ref, o_ref, m_scr, acc_scr = next(it), next(it), next(it), next(it)
    j = pl.program_id(2)

    def process(kr, vr):
        tk = kr.shape[1]
        for h in range(MLA_H):
            qk = slice(h * MLA_DQ, (h + 1) * MLA_DQ)
            s = _dot_nt(q_ref[0, :, qk], kr[0, :, qk])
            m_prev = m_scr[h]
            m_new = jnp.maximum(m_prev, jnp.max(s, axis=-1, keepdims=True))
            alpha = jnp.exp2(m_prev - m_new)
            p = jnp.exp2(s - jnp.concatenate([m_new] * (tk // 128), axis=-1))
            pv = _dot(p.astype(BF16), vr[0, :, h * MLA_DV:(h + 1) * MLA_DV])
            acc_scr[h] = jnp.concatenate([alpha] * (MLA_DV // 128), axis=-1) * acc_scr[h] + pv
            m_scr[h] = m_new

    @pl.when(j == 0)
    def _():
        m_scr[...] = jnp.full(m_scr.shape, NEG_INF, F32)
        acc_scr[...] = jnp.zeros(acc_scr.shape, F32)
        if has_ctx:
            process(kc_ref, vc_ref)

    process(k_ref, v_ref)

    @pl.when(j == nkv - 1)
    def _():
        outs = []
        for h in range(MLA_H):
            a = acc_scr[h]
            outs.append(a[:, :MLA_V] / a[:, MLA_V:MLA_V + 1])
        o_ref[0] = (jnp.concatenate(outs, axis=-1) * _silu(gp_ref[0])).astype(BF16)


def _mla_attn(q, k, v, gp, gpcol, ctx=None, tq=1024, tk=512):
    b, sq, _ = q.shape
    sk = k.shape[1]
    tq = min(tq, sq)
    tk = min(tk, sk)
    nkv = sk // tk
    hq = MLA_H * MLA_DQ
    hv = MLA_H * MLA_DV
    in_specs = [pl.BlockSpec((1, tq, hq), lambda bi, i, j: (bi, i, 0)),
                pl.BlockSpec((1, tk, hq), lambda bi, i, j: (bi, j, 0)),
                pl.BlockSpec((1, tk, hv), lambda bi, i, j: (bi, j, 0))]
    args = [q, k, v]
    if ctx is not None:
        kc, vc = ctx
        in_specs += [pl.BlockSpec((1,) + kc.shape[1:], lambda bi, i, j: (bi, 0, 0)),
                     pl.BlockSpec((1,) + vc.shape[1:], lambda bi, i, j: (bi, 0, 0))]
        args += [kc, vc]
    in_specs += [pl.BlockSpec((1, tq, MLA_H * MLA_V), lambda bi, i, j: (bi, i, gpcol))]
    args += [gp]
    return pl.pallas_call(
        functools.partial(_mla_attn_body, has_ctx=ctx is not None, nkv=nkv),
        grid=(b, sq // tq, nkv),
        in_specs=in_specs,
        out_specs=pl.BlockSpec((1, tq, MLA_H * MLA_V), lambda bi, i, j: (bi, i, 0)),
        out_shape=jax.ShapeDtypeStruct((b, sq, MLA_H * MLA_V), BF16),
        scratch_shapes=[pltpu.VMEM((MLA_H, tq, 128), F32), pltpu.VMEM((MLA_H, tq, MLA_DV), F32)],
        compiler_params=_cp(("parallel", "parallel", "arbitrary"), 48),
        name="mla_attn",
    )(*args)


def _flash_body(*refs, n_h, n_kvh, dq, dv, scale, has_sink, has_ctx, nkv):
    it = iter(refs)
    q_ref, k_ref, v_ref = next(it), next(it), next(it)
    kc_ref = vc_ref = sink_ref = None
    if has_ctx:
        kc_ref, vc_ref = next(it), next(it)
    if has_sink:
        sink_ref = next(it)
    gp_ref, o_ref, m_scr, l_scr, acc_scr = next(it), next(it), next(it), next(it), next(it)
    j = pl.program_id(2)
    group = n_h // n_kvh
    tq = q_ref.shape[1]

    def process(kr, vr):
        for h in range(n_h):
            g = h // group
            q = q_ref[0, :, h * dq:(h + 1) * dq].astype(BF16)
            k = kr[0, :, g * dq:(g + 1) * dq].astype(BF16)
            v = vr[0, :, g * dv:(g + 1) * dv].astype(BF16)
            s = _dot_nt(q, k) * scale
            m_prev = m_scr[h]
            m_new = jnp.maximum(m_prev, jnp.max(s, axis=-1, keepdims=True))
            alpha = jnp.exp(m_prev - m_new)
            p = jnp.exp(s - m_new)
            l_scr[h] = alpha * l_scr[h] + jnp.sum(p, axis=-1, keepdims=True)
            acc_scr[h] = alpha * acc_scr[h] + _dot(p.astype(BF16), v)
            m_scr[h] = m_new

    @pl.when(j == 0)
    def _():
        if has_sink:
            for h in range(n_h):
                m_scr[h] = jnp.broadcast_to(sink_ref[0:1, h:h + 1], (tq, 1))
            l_scr[...] = jnp.ones(l_scr.shape, F32)
        else:
            m_scr[...] = jnp.full(m_scr.shape, NEG_INF, F32)
            l_scr[...] = jnp.zeros(l_scr.shape, F32)
        acc_scr[...] = jnp.zeros(acc_scr.shape, F32)
        if has_ctx:
            process(kc_ref, vc_ref)

    process(k_ref, v_ref)

    @pl.when(j == nkv - 1)
    def _():
        o = jnp.concatenate([acc_scr[h] / l_scr[h] for h in range(n_h)], axis=-1)
        o_ref[0] = (o * _silu(gp_ref[0])).astype(BF16)


def _flash(q, qcol, k, kcol, v, vcol, gp, gpcol, *, n_h, n_kvh, dq, dv, scale,
           sink=None, ctx=None, tq=512, tk=512):
    b, sq, _ = q.shape
    sk = k.shape[1]
    tq = min(tq, sq)
    tk = min(tk, sk)
    nkv = sk // tk
    in_specs = [pl.BlockSpec((1, tq, n_h * dq), lambda bi, i, j: (bi, i, qcol)),
                pl.BlockSpec((1, tk, n_kvh * dq), lambda bi, i, j: (bi, j, kcol)),
                pl.BlockSpec((1, tk, n_kvh * dv), lambda bi, i, j: (bi, j, vcol))]
    args = [q, k, v]
    if ctx is not None:
        kc, vc = ctx
        in_specs += [pl.BlockSpec((1,) + kc.shape[1:], lambda bi, i, j: (bi, 0, 0)),
                     pl.BlockSpec((1,) + vc.shape[1:], lambda bi, i, j: (bi, 0, 0))]
        args += [kc, vc]
    if sink is not None:
        in_specs += [pl.BlockSpec((1, n_h), lambda bi, i, j: (0, 0))]
        args += [sink.reshape(1, n_h)]
    in_specs += [pl.BlockSpec((1, tq, n_h * dv), lambda bi, i, j: (bi, i, gpcol))]
    args += [gp]
    body = functools.partial(_flash_body, n_h=n_h, n_kvh=n_kvh, dq=dq, dv=dv, scale=scale,
                             has_sink=sink is not None, has_ctx=ctx is not None, nkv=nkv)
    return pl.pallas_call(
        body,
        grid=(b, sq // tq, nkv),
        in_specs=in_specs,
        out_specs=pl.BlockSpec((1, tq, n_h * dv), lambda bi, i, j: (bi, i, 0)),
        out_shape=jax.ShapeDtypeStruct((b, sq, n_h * dv), BF16),
        scratch_shapes=[pltpu.VMEM((n_h, tq, 1), F32), pltpu.VMEM((n_h, tq, 1), F32),
                        pltpu.VMEM((n_h, tq, dv), F32)],
        compiler_params=_cp(("parallel", "parallel", "arbitrary"), 48),
        name="flash_attn",
    )(*args)


def _ret_body(*refs, nc, has_s0, has_st):
    it = iter(refs)
    dec_ref, norm_ref, q_ref, k_ref, v_ref = next(it), next(it), next(it), next(it), next(it)
    s0_ref = next(it) if has_s0 else None
    gp_ref, o_ref = next(it), next(it)
    st_ref = next(it) if has_st else None
    of_scr, ob_scr = next(it), next(it)
    head = pl.program_id(1)

    d = dec_ref[0]
    lg_all = jnp.minimum(d, 0.0) - jnp.log1p(jnp.exp(-jnp.abs(d)))
    hsel = lax.broadcasted_iota(jnp.int32, d.shape, 1) == head
    lg = jnp.sum(jnp.where(hsel, lg_all, 0.0), axis=1, keepdims=True)
    lg_f, lg_b = lg[0:1, :], lg[1:2, :]

    row = lax.broadcasted_iota(jnp.int32, (BLK, BLK), 0).astype(F32)
    col = lax.broadcasted_iota(jnp.int32, (BLK, BLK), 1).astype(F32)
    a = lax.broadcasted_iota(jnp.int32, (BLK, 1), 0).astype(F32)
    dm_f = jnp.where(row - col >= 0, jnp.exp(lg_f * jnp.maximum(row - col, 0.0)), 0.0)
    dm_b = jnp.where(col - row > 0, jnp.exp(lg_b * jnp.maximum(col - row, 0.0)), 0.0)
    qd_f, kd_f, cd_f = jnp.exp(lg_f * (a + 1.0)), jnp.exp(lg_f * (BLK - 1.0 - a)), jnp.exp(lg_f * BLK)
    qd_b, kd_b, cd_b = jnp.exp(lg_b * (BLK - a)), jnp.exp(lg_b * a), jnp.exp(lg_b * BLK)
    kscale = RET_DK ** -0.5

    def chunk(off, st, dm, qd, kd, cd):
        q = q_ref[0, pl.ds(off, BLK), :]
        k = k_ref[0, pl.ds(off, BLK), :] * kscale
        v = v_ref[0, pl.ds(off, BLK), :].astype(BF16)
        inner = _dot_nt(q.astype(BF16), k.astype(BF16)) * dm
        o = _dot(inner.astype(BF16), v) + _dot((q * qd).astype(BF16), st.astype(BF16))
        st = st * cd + _dot(jnp.transpose(k * kd).astype(BF16), v)
        return o, st

    def step(j, carry):
        sf, sb = carry
        off_f = pl.multiple_of(j * BLK, BLK)
        o, sf = chunk(off_f, sf, dm_f, qd_f, kd_f, cd_f)
        of_scr[pl.ds(off_f, BLK), :] = o
        off_b = pl.multiple_of((nc - 1 - j) * BLK, BLK)
        o, sb = chunk(off_b, sb, dm_b, qd_b, kd_b, cd_b)
        ob_scr[pl.ds(off_b, BLK), :] = o
        return sf, sb

    if has_s0:
        init = (s0_ref[0, 0, 0, 0], s0_ref[0, 0, 1, 0])
    else:
        init = (jnp.zeros((RET_DK, RET_DV), F32), jnp.zeros((RET_DK, RET_DV), F32))
    sf, sb = lax.fori_loop(0, nc, step, init, unroll=2)
    if has_st:
        st_ref[0, 0, 0] = sf
        st_ref[0, 1, 0] = sb
    o = of_scr[...] + ob_scr[...]
    o = o * lax.rsqrt(jnp.mean(o * o, axis=-1, keepdims=True) + EPS) * norm_ref[...]
    o_ref[0] = (o * _silu(gp_ref[0])).astype(BF16)


def _retention(z, decay, norm, s0, layer, want_state):
    b, s, _ = z.shape
    nc = s // BLK
    colspec = lambda blk: pl.BlockSpec((1, s, 128), lambda bi, h: (bi, 0, blk + h))
    in_specs = [pl.BlockSpec((1, 2, RET_H), lambda bi, h: (0, 0, 0)),
                pl.BlockSpec((1, RET_DV), lambda bi, h: (0, h)),
                colspec(OFF_RQ // 128), colspec(OFF_RK // 128), colspec(OFF_RV // 128)]
    args = [decay.reshape(1, 2, RET_H), norm.reshape(1, -1), z, z, z]
    if s0 is not None:
        in_specs.append(pl.BlockSpec((1, 1, 2, 1, RET_DK, RET_DV), lambda bi, h: (bi, layer, 0, h, 0, 0)))
        args.append(s0)
    in_specs.append(colspec((OFF_GP + BR_W) // 128))
    args.append(z)
    out_specs = [pl.BlockSpec((1, s, RET_DV), lambda bi, h: (bi, 0, h))]
    out_shape = [jax.ShapeDtypeStruct((b, s, RET_H * RET_DV), BF16)]
    if want_state:
        out_specs.append(pl.BlockSpec((1, 2, 1, RET_DK, RET_DV), lambda bi, h: (bi, 0, h, 0, 0)))
        out_shape.append(jax.ShapeDtypeStruct((b, 2, RET_H, RET_DK, RET_DV), F32))
    res = pl.pallas_call(
        functools.partial(_ret_body, nc=nc, has_s0=s0 is not None, has_st=want_state),
        grid=(b, RET_H),
        in_specs=in_specs,
        out_specs=out_specs,
        out_shape=out_shape,
        scratch_shapes=[pltpu.VMEM((s, RET_DV), F32), pltpu.VMEM((s, RET_DV), F32)],
        compiler_params=_cp(("parallel", "parallel"), 48),
        name="retention",
    )(*args)
    return (res[0], res[1]) if want_state else (res[0], None)


def _nat_body(q_ref, k_ref, v_ref, kc_ref, vc_ref, bias_ref, gp_ref, o_ref, kb_scr, vb_scr, o_scr, *, rows):
    scale = NAT_D ** -0.5
    kb_scr[...] = k_ref[0].astype(BF16)
    vb_scr[...] = v_ref[0].astype(BF16)
    kc = kc_ref[0, 0].astype(BF16)
    vc = vc_ref[0, 0].astype(BF16)
    nq = NAT_QR * GRID_W
    win = NAT_WR * GRID_W
    nblk = rows // NAT_QR

    def step(t, _):
        r0 = t * NAT_QR
        ws = jnp.clip(r0 - NAT_ROWS // 2, 0, rows - NAT_WR)
        pat = jnp.where(t == 0, 0, jnp.where(t == nblk - 1, 2, 1))
        qoff = pl.multiple_of(r0 * GRID_W, nq)
        koff = pl.multiple_of(ws * GRID_W, GRID_W)
        q = q_ref[0, pl.ds(qoff, nq), :].astype(BF16)
        kw = kb_scr[pl.ds(koff, win), :]
        vw = vb_scr[pl.ds(koff, win), :]
        s_loc = _dot_nt(q, kw) * scale + bias_ref[0, 0, pat]
        s_ctx = _dot_nt(q, kc) * scale
        m = jnp.maximum(jnp.max(s_loc, axis=-1, keepdims=True), jnp.max(s_ctx, axis=-1, keepdims=True))
        p_loc = jnp.exp(s_loc - m)
        p_ctx = jnp.exp(s_ctx - m)
        den = jnp.sum(p_loc, axis=-1, keepdims=True) + jnp.sum(p_ctx, axis=-1, keepdims=True)
        o = _dot(p_loc.astype(BF16), vw) + _dot(p_ctx.astype(BF16), vc)
        o_scr[pl.ds(qoff, nq), :] = o / den
        return 0

    lax.fori_loop(0, nblk, step, 0)
    o_ref[0] = (o_scr[...] * _silu(gp_ref[0])).astype(BF16)


def _nat_bias_tables(rpb, rows):
    depth, n_h = rpb.shape[:2]
    c = jnp.arange(GRID_W)[:, None]
    kc = jnp.arange(GRID_W)[None, :]
    cs = jnp.clip(c - NAT_COLS // 2, 0, GRID_W - NAT_COLS)
    valid = (kc >= cs) & (kc < cs + NAT_COLS)
    onehot = ((kc - c + (NAT_COLS - 1))[:, :, None] == jnp.arange(2 * NAT_COLS - 1)[None, None, :]).astype(F32)
    toep = jnp.einsum("dhrj,ckj->dhrck", rpb.astype(F32), onehot, precision=lax.Precision.HIGHEST)
    toep = jnp.where(valid, toep, NEG_INF)
    neg = jnp.full((depth, n_h, GRID_W, GRID_W), NEG_INF, F32)
    pats = []
    for r0, ws in ((0, 0), (NAT_QR, 0), (rows - NAT_QR, rows - NAT_WR)):
        qrows = []
        for qi in range(NAT_QR):
            r = r0 + qi
            rs = min(max(r - NAT_ROWS // 2, 0), rows - NAT_ROWS)
            blocks = []
            for wm in range(NAT_WR):
                kr = ws + wm
                blocks.append(toep[:, :, kr - r + NAT_ROWS - 1] if rs <= kr < rs + NAT_ROWS else neg)
            qrows.append(jnp.concatenate(blocks, axis=-1))
        pats.append(jnp.concatenate(qrows, axis=-2))
    return jnp.stack(pats, axis=2)


def _nat(z, cache_k, cache_v, layer, bias):
    b, s, _ = z.shape
    rows = s // GRID_W
    assert rows % NAT_QR == 0 and rows >= 2 * NAT_WR - NAT_ROWS
    lc = cache_k.shape[2]
    colspec = lambda blk: pl.BlockSpec((1, s, NAT_D), lambda bi, h: (bi, 0, blk + h))
    cspec = pl.BlockSpec((1, 1, lc, NAT_D), lambda bi, h: (bi, layer, 0, h))
    return pl.pallas_call(
        functools.partial(_nat_body, rows=rows),
        grid=(b, NAT_H),
        in_specs=[colspec(OFF_NQ // 128), colspec(OFF_NK // 128), colspec(OFF_NV // 128), cspec, cspec,
                  pl.BlockSpec((1, 1) + bias.shape[2:], lambda bi, h: (layer, h, 0, 0, 0)),
                  colspec((OFF_GP + 2 * BR_W) // 128)],
        out_specs=pl.BlockSpec((1, s, NAT_D), lambda bi, h: (bi, 0, h)),
        out_shape=jax.ShapeDtypeStruct((b, s, NAT_H * NAT_D), BF16),
        scratch_shapes=[pltpu.VMEM((s, NAT_D), BF16), pltpu.VMEM((s, NAT_D), BF16),
                        pltpu.VMEM((s, NAT_D), F32)],
        compiler_params=_cp(("parallel", "parallel"), 48),
        name="nat_attn",
    )(z, z, z, cache_k, cache_v, bias, z)


def _swa_body(q_ref, k_ref, v_ref, kc_ref, vc_ref, sink_ref, cos_ref, sin_ref, gp_ref, o_ref, *, seq):
    i = pl.program_id(1)
    scale = SWA_D ** -0.5
    group = SWA_H // SWA_KVH
    win = 3 * BLK
    start = pl.multiple_of(jnp.clip((i - 1) * BLK, 0, seq - win), BLK)
    qoff = pl.multiple_of(i * BLK, BLK)

    cq = cos_ref[pl.ds(qoff, BLK), :]
    sq = sin_ref[pl.ds(qoff, BLK), :]
    q = q_ref[0]
    q = q * jnp.concatenate([cq] * (SWA_H // 2), axis=-1) + _swap_halves(q) * jnp.concatenate([sq] * (SWA_H // 2), axis=-1)
    q = q * scale
    kw = k_ref[0, pl.ds(start, win), :]
    kw = (kw * cos_ref[pl.ds(start, win), :] + _swap_halves(kw) * sin_ref[pl.ds(start, win), :]).astype(BF16)
    vw = v_ref[0, pl.ds(start, win), :].astype(BF16)
    kc = kc_ref[0, 0].astype(BF16)
    vc = vc_ref[0, 0].astype(BF16)

    qpos = qoff + lax.broadcasted_iota(jnp.int32, (BLK, win), 0)
    kpos = start + lax.broadcasted_iota(jnp.int32, (BLK, win), 1)
    valid = jnp.abs(qpos - kpos) <= SWA_WINDOW
    valid = jnp.concatenate([valid] * group, axis=0)

    outs = []
    for g in range(SWA_KVH):
        heads = range(g * group, (g + 1) * group)
        qg = jnp.concatenate([q[:, h * SWA_D:(h + 1) * SWA_D] for h in heads], axis=0).astype(BF16)
        sink = jnp.concatenate([jnp.broadcast_to(sink_ref[0:1, h:h + 1], (BLK, 1)) for h in heads], axis=0)
        ksl = slice(g * SWA_D, (g + 1) * SWA_D)
        s_loc = jnp.where(valid, _dot_nt(qg, kw[:, ksl]), NEG_INF)
        s_ctx = _dot_nt(qg, kc[:, ksl])
        m = jnp.maximum(jnp.max(s_loc, axis=-1, keepdims=True), jnp.max(s_ctx, axis=-1, keepdims=True))
        m = jnp.maximum(m, sink)
        p_loc = jnp.exp(s_loc - m)
        p_ctx = jnp.exp(s_ctx - m)
        den = (jnp.sum(p_loc, axis=-1, keepdims=True) + jnp.sum(p_ctx, axis=-1, keepdims=True)
               + jnp.exp(sink - m))
        o = (_dot(p_loc.astype(BF16), vw[:, ksl]) + _dot(p_ctx.astype(BF16), vc[:, ksl])) / den
        outs += [o[i * BLK:(i + 1) * BLK] for i in range(group)]
    o = jnp.concatenate(outs, axis=-1)
    o_ref[0] = (o * _silu(gp_ref[0])).astype(BF16)


def _swa(z, cache_k, cache_v, layer, sink, cos2, sin2):
    b, s, _ = z.shape
    lc = cache_k.shape[2]
    hd = SWA_H * SWA_D
    kd = SWA_KVH * SWA_D
    cspec = pl.BlockSpec((1, 1, lc, kd), lambda bi, i: (bi, layer, 0, 0))
    return pl.pallas_call(
        functools.partial(_swa_body, seq=s),
        grid=(b, s // BLK),
        in_specs=[pl.BlockSpec((1, BLK, hd), lambda bi, i: (bi, i, OFF_SQ // hd)),
                  pl.BlockSpec((1, s, kd), lambda bi, i: (bi, 0, OFF_SK // kd)),
                  pl.BlockSpec((1, s, kd), lambda bi, i: (bi, 0, OFF_SV // kd)),
                  cspec, cspec,
                  pl.BlockSpec((1, SWA_H), lambda bi, i: (0, 0)),
                  pl.BlockSpec((s, 128), lambda bi, i: (0, 0)),
                  pl.BlockSpec((s, 128), lambda bi, i: (0, 0)),
                  pl.BlockSpec((1, BLK, hd), lambda bi, i: (bi, i, (OFF_GP + 3 * BR_W) // hd))],
        out_specs=pl.BlockSpec((1, BLK, hd), lambda bi, i: (bi, i, 0)),
        out_shape=jax.ShapeDtypeStruct((b, s, hd), BF16),
        compiler_params=_cp(("parallel", "arbitrary"), 48),
        name="swa_attn",
    )(z, z, z, cache_k, cache_v, sink.reshape(1, SWA_H), cos2, sin2, z)


def _merge_body(h_ref, o0_ref, o1_ref, o2_ref, o3_ref, g0_ref, g1_ref, g2_ref, g3_ref, wb_ref, y_ref):
    h = h_ref[...]
    o_refs = (o0_ref, o1_ref, o2_ref, o3_ref)
    g_refs = (g0_ref, g1_ref, g2_ref, g3_ref)
    y = None
    for n in range(N_BRANCH):
        t = _sigmoid(_dot(h, g_refs[n][...])) * _dot(o_refs[n][...], wb_ref[n])
        y = t if y is None else y + t
    y_ref[...] = y.astype(BF16)


def _merge(h, ogs, w_gate, w_branch, tm, tn):
    m, d = h.shape
    tm = min(tm, m)
    nj = d // tn
    gspec = lambda n: pl.BlockSpec((d, tn), lambda i, j: (0, n * nj + j))
    return pl.pallas_call(
        _merge_body,
        grid=(m // tm, nj),
        in_specs=[pl.BlockSpec((tm, d), lambda i, j: (i, 0))]
        + [pl.BlockSpec((tm, BR_W), lambda i, j: (i, 0))] * N_BRANCH
        + [gspec(n) for n in range(N_BRANCH)]
        + [pl.BlockSpec((N_BRANCH, BR_W, tn), lambda i, j: (0, 0, j))],
        out_specs=pl.BlockSpec((tm, tn), lambda i, j: (i, j)),
        out_shape=jax.ShapeDtypeStruct((m, d), BF16),
        compiler_params=_cp(("parallel", "arbitrary"), 48),
        name="merge",
    )(h, *ogs, w_gate, w_gate, w_gate, w_gate, w_branch)


def _out_body(y_ref, w_ref, x_ref, gate_ref, g_ref, o_ref):
    t = _dot(y_ref[0], w_ref[...])
    t = t * lax.rsqrt(jnp.mean(t * t, axis=-1, keepdims=True) + EPS) * g_ref[...]
    o_ref[0] = x_ref[0] + gate_ref[0] * t


def _out_proj(y, w_out, x, gate, g):
    b, s, d = x.shape
    ts = min(s, 256)
    per_b = gate.shape[0] > 1
    cidx = (lambda bi, i: (bi, 0, 0)) if per_b else (lambda bi, i: (0, 0, 0))
    return pl.pallas_call(
        _out_body,
        grid=(b, s // ts),
        in_specs=[pl.BlockSpec((1, ts, d), lambda bi, i: (bi, i, 0)),
                  pl.BlockSpec((d, d), lambda bi, i: (0, 0)),
                  pl.BlockSpec((1, ts, d), lambda bi, i: (bi, i, 0)),
                  pl.BlockSpec((1, 1, d), cidx),
                  pl.BlockSpec((1, d), lambda bi, i: (0, 0))],
        out_specs=pl.BlockSpec((1, ts, d), lambda bi, i: (bi, i, 0)),
        out_shape=jax.ShapeDtypeStruct((b, s, d), F32),
        compiler_params=_cp(("parallel", "parallel"), 48),
        name="out_proj",
    )(y, w_out, x, gate, g.reshape(1, d))


def _mix_weight(w_in):
    c = {}
    o = 0
    for name, w in (("qa", MLA_Q_LORA), ("kva", MLA_KV_LORA), ("kr", MLA_ROPE),
                    ("rq", 512), ("rk", 512), ("rv", 512), ("nq", 512), ("nk", 512), ("nv", 512),
                    ("sq", 512), ("sk", 128), ("sv", 128), ("gp", N_BRANCH * BR_W)):
        c[name] = w_in[:, :, o:o + w]
        o += w
    depth, d, _ = w_in.shape
    order = ["qa", "rq", "rk", "rv", "nq", "nk", "nv", "sq", "gp", "kva", "sk", "sv", "kr"]
    parts = [c[n] for n in order] + [jnp.zeros((depth, d, N_MIX - N_MIX_USED + MLA_ROPE), w_in.dtype)]
    return jnp.concatenate(parts, axis=-1).astype(BF16), o


def _mla_weights(w_q_up, w_kv_up):
    depth = w_q_up.shape[0]
    wq = w_q_up.reshape(depth, MLA_Q_LORA, MLA_H, MLA_NOPE + MLA_ROPE)
    wq = jnp.pad(wq, ((0, 0), (0, 0), (0, 0), (0, MLA_DQ - MLA_NOPE - MLA_ROPE)))
    wq = wq.reshape(depth, MLA_Q_LORA, MLA_H * MLA_DQ).astype(BF16)
    wkv = w_kv_up.reshape(depth, MLA_KV_LORA, MLA_H, MLA_NOPE + MLA_V)
    wk = jnp.pad(wkv[..., :MLA_NOPE], ((0, 0), (0, 0), (0, 0), (0, MLA_DQ - MLA_NOPE)))
    wk = wk.reshape(depth, MLA_KV_LORA, MLA_H * MLA_DQ).astype(BF16)
    wv = jnp.pad(wkv[..., MLA_NOPE:], ((0, 0), (0, 0), (0, 0), (0, MLA_DV - MLA_V)))
    wv = wv.reshape(depth, MLA_KV_LORA, MLA_H * MLA_DV).astype(BF16)
    j = jnp.arange(128)[:, None]
    cidx = jnp.arange(MLA_H * MLA_DQ)[None, :]
    p128 = ((cidx % MLA_DQ == MLA_NOPE + j) & (j < MLA_ROPE)).astype(BF16)
    vones = (jnp.arange(MLA_H * MLA_DV)[None, :] % MLA_DV == MLA_V).astype(F32)
    return wq, wk, wv, p128, vones


def _rope_tables(n_tok):
    pos = jnp.arange(n_tok)
    row = (pos // GRID_W).astype(F32)
    col = (pos % GRID_W).astype(F32)
    n_freq = MLA_ROPE // 4
    inv = ROPE_BASE ** (-jnp.arange(n_freq, dtype=F32) / n_freq)
    ang = jnp.concatenate([row[:, None] * inv[None], col[:, None] * inv[None]], axis=-1)
    cos, sin = jnp.cos(ang), jnp.sin(ang)
    c64 = jnp.concatenate([cos, cos], axis=-1)
    s64 = jnp.concatenate([-sin, sin], axis=-1)
    one, zero = jnp.ones_like(c64), jnp.zeros_like(c64)
    cq = jnp.concatenate([one, one, c64, one], axis=-1)
    sq = jnp.concatenate([zero, zero, s64, zero], axis=-1)
    ck = jnp.concatenate([c64, one], axis=-1)
    sk = jnp.concatenate([s64, zero], axis=-1)
    c2 = jnp.concatenate([c64, c64], axis=-1)
    s2 = jnp.concatenate([s64, s64], axis=-1)
    return (cq, sq, ck, sk), (c2, s2)


def _layer_common(x, h_args, w_mix_l, tm_in):
    b, s, d = x.shape
    h = _modnorm(x, *h_args)
    z = _matmul(h.reshape(b * s, d), w_mix_l, tm_in, 768).reshape(b, s, N_MIX)
    return h, z


def kernel(x_prompt, x_sample, cache_mla_ckv, cache_mla_krope, state_ret, cache_nat_k, cache_nat_v,
           cache_swa_k, cache_swa_v, c, c_ctx, w_mod, b_mod, norm_pre, norm_post, w_in, mla_q_norm,
           mla_kv_norm, mla_w_q_up, mla_w_kv_up, ret_decay, ret_norm, nat_rpb, swa_sink, w_branch, w_out):
    depth = w_mod.shape[0]
    d = x_prompt.shape[-1]
    nb = x_sample.shape[0]
    s_lat = x_sample.shape[1]
    l_ctx = cache_mla_ckv.shape[2]

    w_mix, mix_cols = _mix_weight(w_in)
    w_gate = w_in[:, :, mix_cols:].astype(BF16)
    w_br = w_branch.astype(BF16)
    w_o = w_out.astype(BF16)
    wq, wk, wv, p128, vones = _mla_weights(mla_w_q_up, mla_w_kv_up)
    p64 = p128[:MLA_ROPE]
    mla_tabs, (cos2, sin2) = _rope_tables(s_lat)
    nat_bias = _nat_bias_tables(nat_rpb, s_lat // GRID_W)
    nat_k_c = cache_nat_k.reshape(nb, depth, l_ctx, NAT_H * NAT_D)
    nat_v_c = cache_nat_v.reshape(nb, depth, l_ctx, NAT_H * NAT_D)
    swa_k_c = cache_swa_k.reshape(nb, depth, l_ctx, SWA_KVH * SWA_D)
    swa_v_c = cache_swa_v.reshape(nb, depth, l_ctx, SWA_KVH * SWA_D)

    rows = 8
    conds = jnp.concatenate([c_ctx[None, :], c, jnp.zeros((rows - 1 - nb, d), F32)], axis=0)
    mod = _modulation(conds, w_mod, b_mod)

    def mod_parts(l, lo, hi):
        m = mod[l, lo:hi]
        return m[:, None, :d], m[:, None, d:2 * d], m[:, None, 2 * d:]

    yp = x_prompt
    bc, sc, _ = yp.shape
    ckv_l, kr_l, st_l, nk_l, nv_l, sk_l, sv_l = [], [], [], [], [], [], []
    for l in range(depth):
        shift, scale, gate = mod_parts(l, 0, 1)
        h, z = _layer_common(yp, (norm_pre[l], shift, scale), w_mix[l], 1024)
        q, kf, vf, ckv = _mla_prep(z, mla_q_norm[l], mla_kv_norm[l], wq[l], wk[l], wv[l], p128, vones, None)
        og_mla = _mla_attn(q, kf, vf, z, OFF_GP // BR_W)
        og_ret, st = _retention(z, ret_decay[l], ret_norm[l], None, 0, True)
        og_nat = _flash(z, OFF_NQ // 512, z, OFF_NK // 512, z, OFF_NV // 512, z, (OFF_GP + 2 * BR_W) // BR_W,
                        n_h=NAT_H, n_kvh=NAT_H, dq=NAT_D, dv=NAT_D, scale=NAT_D ** -0.5)
        og_swa = _flash(z, OFF_SQ // 512, z, OFF_SK // 128, z, OFF_SV // 128, z, (OFF_GP + 3 * BR_W) // BR_W,
                        n_h=SWA_H, n_kvh=SWA_KVH, dq=SWA_D, dv=SWA_D, scale=SWA_D ** -0.5, sink=swa_sink[l])
        ogs = [o.reshape(bc * sc, BR_W) for o in (og_mla, og_ret, og_nat, og_swa)]
        y = _merge(h.reshape(bc * sc, d), ogs, w_gate[l], w_br[l], 1024, 256)
        yp = _out_proj(y.reshape(bc, sc, d), w_o[l], yp, gate, norm_post[l])
        ckv_l.append(ckv)
        kr_l.append(z[:, :, OFF_KR:OFF_KR + MLA_ROPE])
        st_l.append(st)
        nk_l.append(z[:, :, OFF_NK:OFF_NK + 512].reshape(bc, sc, NAT_H, NAT_D))
        nv_l.append(z[:, :, OFF_NV:OFF_NV + 512].reshape(bc, sc, NAT_H, NAT_D))
        sk_l.append(z[:, :, OFF_SK:OFF_SK + 128].reshape(bc, sc, SWA_KVH, SWA_D))
        sv_l.append(z[:, :, OFF_SV:OFF_SV + 128].reshape(bc, sc, SWA_KVH, SWA_D))

    ys = x_sample
    for l in range(depth):
        shift, scale, gate = mod_parts(l, 1, 1 + nb)
        h, z = _layer_common(ys, (norm_pre[l], shift, scale), w_mix[l], 1024)
        q, kf, vf, _ = _mla_prep(z, mla_q_norm[l], mla_kv_norm[l], wq[l], wk[l], wv[l], p128, vones, mla_tabs)
        kc, vc = _mla_expand(cache_mla_ckv, cache_mla_krope, l, wk[l], wv[l], p64, vones)
        og_mla = _mla_attn(q, kf, vf, z, OFF_GP // BR_W, ctx=(kc, vc))
        og_ret, _ = _retention(z, ret_decay[l], ret_norm[l], state_ret, l, False)
        og_nat = _nat(z, nat_k_c, nat_v_c, l, nat_bias)
        og_swa = _swa(z, swa_k_c, swa_v_c, l, swa_sink[l], cos2, sin2)
        ogs = [o.reshape(nb * s_lat, BR_W) for o in (og_mla, og_ret, og_nat, og_swa)]
        y = _merge(h.reshape(nb * s_lat, d), ogs, w_gate[l], w_br[l], 1024, 256)
        ys = _out_proj(y.reshape(nb, s_lat, d), w_o[l], ys, gate, norm_post[l])

    return (yp, ys, jnp.stack(ckv_l, axis=1), jnp.stack(kr_l, axis=1), jnp.stack(st_l, axis=1),
            jnp.stack(nk_l, axis=1), jnp.stack(nv_l, axis=1), jnp.stack(sk_l, axis=1), jnp.stack(sv_l, axis=1))
```

```python
import functools

import jax
import jax.numpy as jnp
from jax import lax
from jax.experimental import pallas as pl
from jax.experimental.pallas import tpu as pltpu

F32 = jnp.float32
BF16 = jnp.bfloat16

GRID_W = 64
BLK = 128
ROPE_BASE = 10000.0
EPS = 1e-6
NEG_INF = -1e30
N_BRANCH = 4
BR_W = 512

MLA_H = 4
MLA_Q_LORA = 512
MLA_KV_LORA = 256
MLA_NOPE = 128
MLA_ROPE = 64
MLA_V = 128
MLA_DQ = 256
MLA_DV = 256
MLA_QSCALE = (MLA_NOPE + MLA_ROPE) ** -0.5 * 1.4426950408889634

RET_H = 4
RET_DK = 128
RET_DV = 128

NAT_H = 4
NAT_D = 128
NAT_ROWS = 8
NAT_COLS = 16
NAT_QR = 4
NAT_WR = NAT_ROWS + NAT_QR

SWA_H = 8
SWA_KVH = 2
SWA_D = 64
SWA_WINDOW = 128
SWA_QB = 2

OFF_QA = 0
OFF_RQ = 512
OFF_RK = 1024
OFF_RV = 1536
OFF_NQ = 2048
OFF_NK = 2560
OFF_NV = 3072
OFF_SQ = 3584
OFF_GP = 4096
OFF_KVA = 6144
OFF_SK = 6400
OFF_SV = 6528
OFF_KR = 6656
N_MIX_USED = 6784
N_MIX = 6912

V7X_VMEM_BYTES = 64 * 1024 * 1024


def _cp(sem, vmem_mb):
    assert vmem_mb * 1024 * 1024 < V7X_VMEM_BYTES
    return pltpu.CompilerParams(dimension_semantics=sem, vmem_limit_bytes=vmem_mb * 1024 * 1024)


def _sigmoid(x):
    return 1.0 / (1.0 + jnp.exp(-x))


def _silu(x):
    return x * _sigmoid(x)


def _swap_halves(x):
    n = x.shape[-1]
    ax = x.ndim - 1
    lane = lax.broadcasted_iota(jnp.int32, x.shape, ax)
    return jnp.where((lane & 32) == 0, pltpu.roll(x, n - 32, ax), pltpu.roll(x, 32, ax))


def _dot_nt(a, b):
    return lax.dot_general(a, b, (((1,), (1,)), ((), ())), preferred_element_type=F32)


def _dot(a, b):
    return jnp.dot(a, b, preferred_element_type=F32)


def _mod_body(c_ref, w_ref, b_ref, o_ref):
    a = _silu(c_ref[...]).astype(BF16)
    o_ref[0] = _dot(a, w_ref[0].astype(BF16)) + b_ref[0]


def _modulation(conds, w_mod, b_mod):
    depth, d, n = w_mod.shape
    rows = conds.shape[0]
    tn = 1024 if n % 1024 == 0 else n
    return pl.pallas_call(
        _mod_body,
        grid=(depth, n // tn),
        in_specs=[pl.BlockSpec((rows, d), lambda l, j: (0, 0)),
                  pl.BlockSpec((1, d, tn), lambda l, j: (l, 0, j)),
                  pl.BlockSpec((1, 1, tn), lambda l, j: (l, 0, j))],
        out_specs=pl.BlockSpec((1, rows, tn), lambda l, j: (l, 0, j)),
        out_shape=jax.ShapeDtypeStruct((depth, rows, n), F32),
        compiler_params=_cp(("parallel", "parallel"), 40),
        name="modulation",
    )(conds, w_mod, b_mod.reshape(depth, 1, n))


def _modnorm_body(x_ref, g_ref, sh_ref, sc_ref, h_ref):
    x = x_ref[0]
    y = x * lax.rsqrt(jnp.mean(x * x, axis=-1, keepdims=True) + EPS)
    y = y * g_ref[...]
    h_ref[0] = (y * (1.0 + sc_ref[0]) + sh_ref[0]).astype(BF16)


def _modnorm(x, g, shift, scale):
    b, s, d = x.shape
    ts = min(s, 512)
    per_b = shift.shape[0] > 1
    cidx = (lambda bi, i: (bi, 0, 0)) if per_b else (lambda bi, i: (0, 0, 0))
    return pl.pallas_call(
        _modnorm_body,
        grid=(b, s // ts),
        in_specs=[pl.BlockSpec((1, ts, d), lambda bi, i: (bi, i, 0)),
                  pl.BlockSpec((1, d), lambda bi, i: (0, 0)),
                  pl.BlockSpec((1, 1, d), cidx),
                  pl.BlockSpec((1, 1, d), cidx)],
        out_specs=pl.BlockSpec((1, ts, d), lambda bi, i: (bi, i, 0)),
        out_shape=jax.ShapeDtypeStruct((b, s, d), BF16),
        compiler_params=_cp(("parallel", "parallel"), 32),
        name="modnorm",
    )(x, g.reshape(1, d), shift, scale)


def _mm_body(a_ref, b_ref, o_ref):
    o_ref[...] = _dot(a_ref[...], b_ref[...])


def _in_proj(a, w_mix, layer, tm, tn):
    m, k = a.shape
    tm = min(tm, m)
    return pl.pallas_call(
        _mm_body,
        grid=(m // tm, N_MIX // tn),
        in_specs=[pl.BlockSpec((tm, k), lambda i, j: (i, 0)),
                  pl.BlockSpec((None, k, tn), lambda i, j: (layer, 0, j))],
        out_specs=pl.BlockSpec((tm, tn), lambda i, j: (i, j)),
        out_shape=jax.ShapeDtypeStruct((m, N_MIX), F32),
        compiler_params=_cp(("parallel", "arbitrary"), 48),
        name="in_proj",
    )(a, w_mix)


def _mla_prep_body(*refs, rope):
    if rope:
        (qa_ref, kva_ref, kr_ref, gq_ref, gkv_ref, wq_ref, wk_ref, wv_ref, p_ref, one_ref,
         cq_ref, sq_ref, ck_ref, sk_ref, q_out, k_out, v_out, ckv_out) = refs
    else:
        (qa_ref, kva_ref, kr_ref, gq_ref, gkv_ref, wq_ref, wk_ref, wv_ref, p_ref, one_ref,
         q_out, k_out, v_out, ckv_out) = refs
    qa = qa_ref[0]
    cq = qa * lax.rsqrt(jnp.mean(qa * qa, axis=-1, keepdims=True) + EPS) * gq_ref[...]
    q = _dot(cq.astype(BF16), wq_ref[...])
    kva = kva_ref[0]
    ckv = kva * lax.rsqrt(jnp.mean(kva * kva, axis=-1, keepdims=True) + EPS) * gkv_ref[...]
    ckv_out[0] = ckv
    kr = kr_ref[0]
    if rope:
        cq_t = jnp.concatenate([cq_ref[...]] * MLA_H, axis=-1)
        sq_t = jnp.concatenate([sq_ref[...]] * MLA_H, axis=-1)
        q = q * cq_t + _swap_halves(q) * sq_t
        kr = kr * ck_ref[...] + _swap_halves(kr) * sk_ref[...]
    q_out[0] = (q * MLA_QSCALE).astype(BF16)
    cb = ckv.astype(BF16)
    k_out[0] = (_dot(cb, wk_ref[...]) + _dot(kr.astype(BF16), p_ref[...])).astype(BF16)
    v_out[0] = (_dot(cb, wv_ref[...]) + one_ref[...]).astype(BF16)


def _mla_prep(z, gq, gkv, wq, wk, wv, p128, vones, tabs):
    b, s, _ = z.shape
    tm = min(s, 512)
    rope = tabs is not None
    full = lambda shape: pl.BlockSpec(shape, lambda bi, i: (0,) * len(shape))
    in_specs = [pl.BlockSpec((1, tm, MLA_Q_LORA), lambda bi, i: (bi, i, OFF_QA // MLA_Q_LORA)),
                pl.BlockSpec((1, tm, MLA_KV_LORA), lambda bi, i: (bi, i, OFF_KVA // MLA_KV_LORA)),
                pl.BlockSpec((1, tm, 128), lambda bi, i: (bi, i, OFF_KR // 128)),
                full((1, MLA_Q_LORA)), full((1, MLA_KV_LORA)),
                full(wq.shape), full(wk.shape), full(wv.shape), full(p128.shape), full(vones.shape)]
    args = [z, z, z, gq.reshape(1, -1), gkv.reshape(1, -1), wq, wk, wv, p128, vones]
    if rope:
        cq, sq, ck, sk = tabs
        in_specs += [pl.BlockSpec((tm, MLA_DQ), lambda bi, i: (i, 0))] * 2
        in_specs += [pl.BlockSpec((tm, 128), lambda bi, i: (i, 0))] * 2
        args += [cq, sq, ck, sk]
    hq = MLA_H * MLA_DQ
    hv = MLA_H * MLA_DV
    return pl.pallas_call(
        functools.partial(_mla_prep_body, rope=rope),
        grid=(b, s // tm),
        in_specs=in_specs,
        out_specs=[pl.BlockSpec((1, tm, hq), lambda bi, i: (bi, i, 0)),
                   pl.BlockSpec((1, tm, hq), lambda bi, i: (bi, i, 0)),
                   pl.BlockSpec((1, tm, hv), lambda bi, i: (bi, i, 0)),
                   pl.BlockSpec((1, tm, MLA_KV_LORA), lambda bi, i: (bi, i, 0))],
        out_shape=[jax.ShapeDtypeStruct((b, s, hq), BF16),
                   jax.ShapeDtypeStruct((b, s, hq), BF16),
                   jax.ShapeDtypeStruct((b, s, hv), BF16),
                   jax.ShapeDtypeStruct((b, s, MLA_KV_LORA), F32)],
        compiler_params=_cp(("parallel", "parallel"), 40),
        name="mla_prep",
    )(*args)


def _mla_expand_body(ckv_ref, kr_ref, wk_ref, wv_ref, p_ref, one_ref, k_out, v_out):
    cb = ckv_ref[0, 0].astype(BF16)
    k_out[0] = (_dot(cb, wk_ref[...]) + _dot(kr_ref[0, 0].astype(BF16), p_ref[...])).astype(BF16)
    v_out[0] = (_dot(cb, wv_ref[...]) + one_ref[...]).astype(BF16)


def _mla_expand(cache_ckv, cache_kr, layer, wk, wv, p64, vones):
    b, _, s, _ = cache_ckv.shape
    full = lambda shape: pl.BlockSpec(shape, lambda bi: (0,) * len(shape))
    hq = MLA_H * MLA_DQ
    hv = MLA_H * MLA_DV
    return pl.pallas_call(
        _mla_expand_body,
        grid=(b,),
        in_specs=[pl.BlockSpec((1, 1, s, MLA_KV_LORA), lambda bi: (bi, layer, 0, 0)),
                  pl.BlockSpec((1, 1, s, MLA_ROPE), lambda bi: (bi, layer, 0, 0)),
                  full(wk.shape), full(wv.shape), full(p64.shape), full(vones.shape)],
        out_specs=[pl.BlockSpec((1, s, hq), lambda bi: (bi, 0, 0)),
                   pl.BlockSpec((1, s, hv), lambda bi: (bi, 0, 0))],
        out_shape=[jax.ShapeDtypeStruct((b, s, hq), BF16),
                   jax.ShapeDtypeStruct((b, s, hv), BF16)],
        compiler_params=_cp(("parallel",), 32),
        name="mla_expand",
    )(cache_ckv, cache_kr, wk, wv, p64, vones)


def _mla_attn_body(*refs, has_ctx, nkv):
    it = iter(refs)
    q_ref, k_ref, v_ref = next(it), next(it), next(it)
    kc_ref = vc_ref = None
    if has_ctx:
        kc_ref, vc_ref = next(it), next(it)
    gp_ref, o_ref, m_scr, acc_scr = next(it), next(it), next(it), next(it)
    j = pl.program_id(2)

    def process(kr, vr):
        tk = kr.shape[1]
        for h in range(MLA_H):
            qk = slice(h * MLA_DQ, (h + 1) * MLA_DQ)
            s = _dot_nt(q_ref[0, :, qk], kr[0, :, qk])
            m_prev = m_scr[h]
            m_new = jnp.maximum(m_prev, jnp.max(s, axis=-1, keepdims=True))
            alpha = jnp.exp2(m_prev - m_new)
            p = jnp.exp2(s - jnp.concatenate([m_new] * (tk // 128), axis=-1))
            pv = _dot(p.astype(BF16), vr[0, :, h * MLA_DV:(h + 1) * MLA_DV])
            acc_scr[h] = jnp.concatenate([alpha] * (MLA_DV // 128), axis=-1) * acc_scr[h] + pv
            m_scr[h] = m_new

    @pl.when(j == 0)
    def _():
        m_scr[...] = jnp.full(m_scr.shape, NEG_INF, F32)
        acc_scr[...] = jnp.zeros(acc_scr.shape, F32)
        if has_ctx:
            process(kc_ref, vc_ref)

    process(k_ref, v_ref)

    @pl.when(j == nkv - 1)
    def _():
        outs = []
        for h in range(MLA_H):
            a = acc_scr[h]
            outs.append(a[:, :MLA_V] / a[:, MLA_V:MLA_V + 1])
        o_ref[0] = (jnp.concatenate(outs, axis=-1) * _silu(gp_ref[0])).astype(BF16)


def _mla_attn(q, k, v, gp, gpcol, ctx=None, tq=1024, tk=1024):
    b, sq, _ = q.shape
    sk = k.shape[1]
    tq = min(tq, sq)
    tk = min(tk, sk)
    nkv = sk // tk
    hq = MLA_H * MLA_DQ
    hv = MLA_H * MLA_DV
    in_specs = [pl.BlockSpec((1, tq, hq), lambda bi, i, j: (bi, i, 0)),
                pl.BlockSpec((1, tk, hq), lambda bi, i, j: (bi, j, 0)),
                pl.BlockSpec((1, tk, hv), lambda bi, i, j: (bi, j, 0))]
    args = [q, k, v]
    if ctx is not None:
        kc, vc = ctx
        in_specs += [pl.BlockSpec((1,) + kc.shape[1:], lambda bi, i, j: (bi, 0, 0)),
                     pl.BlockSpec((1,) + vc.shape[1:], lambda bi, i, j: (bi, 0, 0))]
        args += [kc, vc]
    in_specs += [pl.BlockSpec((1, tq, MLA_H * MLA_V), lambda bi, i, j: (bi, i, gpcol))]
    args += [gp]
    return pl.pallas_call(
        functools.partial(_mla_attn_body, has_ctx=ctx is not None, nkv=nkv),
        grid=(b, sq // tq, nkv),
        in_specs=in_specs,
        out_specs=pl.BlockSpec((1, tq, MLA_H * MLA_V), lambda bi, i, j: (bi, i, 0)),
        out_shape=jax.ShapeDtypeStruct((b, sq, MLA_H * MLA_V), BF16),
        scratch_shapes=[pltpu.VMEM((MLA_H, tq, 128), F32), pltpu.VMEM((MLA_H, tq, MLA_DV), F32)],
        compiler_params=_cp(("parallel", "parallel", "arbitrary"), 48),
        name="mla_attn",
    )(*args)


def _flash_body(*refs, n_h, n_kvh, dq, dv, scale, has_sink, has_ctx, nkv):
    it = iter(refs)
    q_ref, k_ref, v_ref = next(it), next(it), next(it)
    kc_ref = vc_ref = sink_ref = None
    if has_ctx:
        kc_ref, vc_ref = next(it), next(it)
    if has_sink:
        sink_ref = next(it)
    gp_ref, o_ref, m_scr, l_scr, acc_scr = next(it), next(it), next(it), next(it), next(it)
    j = pl.program_id(2)
    group = n_h // n_kvh
    tq = q_ref.shape[1]

    def process(kr, vr):
        for h in range(n_h):
            g = h // group
            q = q_ref[0, :, h * dq:(h + 1) * dq].astype(BF16)
            k = kr[0, :, g * dq:(g + 1) * dq].astype(BF16)
            v = vr[0, :, g * dv:(g + 1) * dv].astype(BF16)
            s = _dot_nt(q, k) * scale
            m_prev = m_scr[h]
            m_new = jnp.maximum(m_prev, jnp.max(s, axis=-1, keepdims=True))
            alpha = jnp.exp(m_prev - m_new)
            p = jnp.exp(s - m_new)
            l_scr[h] = alpha * l_scr[h] + jnp.sum(p, axis=-1, keepdims=True)
            acc_scr[h] = alpha * acc_scr[h] + _dot(p.astype(BF16), v)
            m_scr[h] = m_new

    @pl.when(j == 0)
    def _():
        if has_sink:
            for h in range(n_h):
                m_scr[h] = jnp.broadcast_to(sink_ref[0:1, h:h + 1], (tq, 1))
            l_scr[...] = jnp.ones(l_scr.shape, F32)
        else:
            m_scr[...] = jnp.full(m_scr.shape, NEG_INF, F32)
            l_scr[...] = jnp.zeros(l_scr.shape, F32)
        acc_scr[...] = jnp.zeros(acc_scr.shape, F32)
        if has_ctx:
            process(kc_ref, vc_ref)

    process(k_ref, v_ref)

    @pl.when(j == nkv - 1)
    def _():
        o = jnp.concatenate([acc_scr[h] / l_scr[h] for h in range(n_h)], axis=-1)
        o_ref[0] = (o * _silu(gp_ref[0])).astype(BF16)


def _flash(q, qcol, k, kcol, v, vcol, gp, gpcol, *, n_h, n_kvh, dq, dv, scale,
           sink=None, ctx=None, tq=512, tk=512):
    b, sq, _ = q.shape
    sk = k.shape[1]
    tq = min(tq, sq)
    tk = min(tk, sk)
    nkv = sk // tk
    in_specs = [pl.BlockSpec((1, tq, n_h * dq), lambda bi, i, j: (bi, i, qcol)),
                pl.BlockSpec((1, tk, n_kvh * dq), lambda bi, i, j: (bi, j, kcol)),
                pl.BlockSpec((1, tk, n_kvh * dv), lambda bi, i, j: (bi, j, vcol))]
    args = [q, k, v]
    if ctx is not None:
        kc, vc = ctx
        in_specs += [pl.BlockSpec((1,) + kc.shape[1:], lambda bi, i, j: (bi, 0, 0)),
                     pl.BlockSpec((1,) + vc.shape[1:], lambda bi, i, j: (bi, 0, 0))]
        args += [kc, vc]
    if sink is not None:
        in_specs += [pl.BlockSpec((1, n_h), lambda bi, i, j: (0, 0))]
        args += [sink.reshape(1, n_h)]
    in_specs += [pl.BlockSpec((1, tq, n_h * dv), lambda bi, i, j: (bi, i, gpcol))]
    args += [gp]
    body = functools.partial(_flash_body, n_h=n_h, n_kvh=n_kvh, dq=dq, dv=dv, scale=scale,
                             has_sink=sink is not None, has_ctx=ctx is not None, nkv=nkv)
    return pl.pallas_call(
        body,
        grid=(b, sq // tq, nkv),
        in_specs=in_specs,
        out_specs=pl.BlockSpec((1, tq, n_h * dv), lambda bi, i, j: (bi, i, 0)),
        out_shape=jax.ShapeDtypeStruct((b, sq, n_h * dv), BF16),
        scratch_shapes=[pltpu.VMEM((n_h, tq, 1), F32), pltpu.VMEM((n_h, tq, 1), F32),
                        pltpu.VMEM((n_h, tq, dv), F32)],
        compiler_params=_cp(("parallel", "parallel", "arbitrary"), 48),
        name="flash_attn",
    )(*args)


def _ret_body(*refs, nc, has_s0, has_st):
    it = iter(refs)
    dec_ref, norm_ref, q_ref, k_ref, v_ref = next(it), next(it), next(it), next(it), next(it)
    s0_ref = next(it) if has_s0 else None
    gp_ref, o_ref = next(it), next(it)
    st_ref = next(it) if has_st else None
    of_scr, ob_scr = next(it), next(it)
    head = pl.program_id(1)

    d = dec_ref[0]
    lg_all = jnp.minimum(d, 0.0) - jnp.log1p(jnp.exp(-jnp.abs(d)))
    hsel = lax.broadcasted_iota(jnp.int32, d.shape, 1) == head
    lg = jnp.sum(jnp.where(hsel, lg_all, 0.0), axis=1, keepdims=True)
    lg_f, lg_b = lg[0:1, :], lg[1:2, :]

    row = lax.broadcasted_iota(jnp.int32, (BLK, BLK), 0).astype(F32)
    col = lax.broadcasted_iota(jnp.int32, (BLK, BLK), 1).astype(F32)
    a = lax.broadcasted_iota(jnp.int32, (BLK, 1), 0).astype(F32)
    dm_f = jnp.where(row - col >= 0, jnp.exp(lg_f * jnp.maximum(row - col, 0.0)), 0.0)
    dm_b = jnp.where(col - row > 0, jnp.exp(lg_b * jnp.maximum(col - row, 0.0)), 0.0)
    qd_f, kd_f, cd_f = jnp.exp(lg_f * (a + 1.0)), jnp.exp(lg_f * (BLK - 1.0 - a)), jnp.exp(lg_f * BLK)
    qd_b, kd_b, cd_b = jnp.exp(lg_b * (BLK - a)), jnp.exp(lg_b * a), jnp.exp(lg_b * BLK)
    kscale = RET_DK ** -0.5

    def chunk(off, st, dm, qd, kd, cd):
        q = q_ref[0, pl.ds(off, BLK), :]
        k = k_ref[0, pl.ds(off, BLK), :] * kscale
        v = v_ref[0, pl.ds(off, BLK), :].astype(BF16)
        inner = _dot_nt(q.astype(BF16), k.astype(BF16)) * dm
        o = _dot(inner.astype(BF16), v) + _dot((q * qd).astype(BF16), st.astype(BF16))
        st = st * cd + _dot(jnp.transpose(k * kd).astype(BF16), v)
        return o, st

    def step(j, carry):
        sf, sb = carry
        off_f = pl.multiple_of(j * BLK, BLK)
        o, sf = chunk(off_f, sf, dm_f, qd_f, kd_f, cd_f)
        of_scr[pl.ds(off_f, BLK), :] = o
        off_b = pl.multiple_of((nc - 1 - j) * BLK, BLK)
        o, sb = chunk(off_b, sb, dm_b, qd_b, kd_b, cd_b)
        ob_scr[pl.ds(off_b, BLK), :] = o
        return sf, sb

    if has_s0:
        init = (s0_ref[0, 0, 0, 0], s0_ref[0, 0, 1, 0])
    else:
        init = (jnp.zeros((RET_DK, RET_DV), F32), jnp.zeros((RET_DK, RET_DV), F32))
    sf, sb = lax.fori_loop(0, nc, step, init, unroll=2)
    if has_st:
        st_ref[0, 0, 0] = sf
        st_ref[0, 1, 0] = sb
    o = of_scr[...] + ob_scr[...]
    o = o * lax.rsqrt(jnp.mean(o * o, axis=-1, keepdims=True) + EPS) * norm_ref[...]
    o_ref[0] = (o * _silu(gp_ref[0])).astype(BF16)


def _retention(z, decay, norm, s0, layer, want_state):
    b, s, _ = z.shape
    nc = s // BLK
    colspec = lambda blk: pl.BlockSpec((1, s, 128), lambda bi, h: (bi, 0, blk + h))
    in_specs = [pl.BlockSpec((1, 2, RET_H), lambda bi, h: (0, 0, 0)),
                pl.BlockSpec((1, RET_DV), lambda bi, h: (0, h)),
                colspec(OFF_RQ // 128), colspec(OFF_RK // 128), colspec(OFF_RV // 128)]
    args = [decay.reshape(1, 2, RET_H), norm.reshape(1, -1), z, z, z]
    if s0 is not None:
        in_specs.append(pl.BlockSpec((1, 1, 2, 1, RET_DK, RET_DV), lambda bi, h: (bi, layer, 0, h, 0, 0)))
        args.append(s0)
    in_specs.append(colspec((OFF_GP + BR_W) // 128))
    args.append(z)
    out_specs = [pl.BlockSpec((1, s, RET_DV), lambda bi, h: (bi, 0, h))]
    out_shape = [jax.ShapeDtypeStruct((b, s, RET_H * RET_DV), BF16)]
    if want_state:
        out_specs.append(pl.BlockSpec((1, 2, 1, RET_DK, RET_DV), lambda bi, h: (bi, 0, h, 0, 0)))
        out_shape.append(jax.ShapeDtypeStruct((b, 2, RET_H, RET_DK, RET_DV), F32))
    res = pl.pallas_call(
        functools.partial(_ret_body, nc=nc, has_s0=s0 is not None, has_st=want_state),
        grid=(b, RET_H),
        in_specs=in_specs,
        out_specs=out_specs,
        out_shape=out_shape,
        scratch_shapes=[pltpu.VMEM((s, RET_DV), F32), pltpu.VMEM((s, RET_DV), F32)],
        compiler_params=_cp(("parallel", "parallel"), 48),
        name="retention",
    )(*args)
    return (res[0], res[1]) if want_state else (res[0], None)


def _nat_body(q_ref, k_ref, v_ref, kc_ref, vc_ref, bias_ref, gp_ref, o_ref, kb_scr, vb_scr, o_scr, *, rows):
    scale = NAT_D ** -0.5
    kb_scr[...] = k_ref[0].astype(BF16)
    vb_scr[...] = v_ref[0].astype(BF16)
    kc = kc_ref[0, 0].astype(BF16)
    vc = vc_ref[0, 0].astype(BF16)
    nq = NAT_QR * GRID_W
    win = NAT_WR * GRID_W
    nblk = rows // NAT_QR

    def step(t, _):
        r0 = t * NAT_QR
        ws = jnp.clip(r0 - NAT_ROWS // 2, 0, rows - NAT_WR)
        pat = jnp.where(t == 0, 0, jnp.where(t == nblk - 1, 2, 1))
        qoff = pl.multiple_of(r0 * GRID_W, nq)
        koff = pl.multiple_of(ws * GRID_W, GRID_W)
        q = q_ref[0, pl.ds(qoff, nq), :].astype(BF16)
        kw = kb_scr[pl.ds(koff, win), :]
        vw = vb_scr[pl.ds(koff, win), :]
        s_loc = _dot_nt(q, kw) * scale + bias_ref[0, 0, pat]
        s_ctx = _dot_nt(q, kc) * scale
        m = jnp.maximum(jnp.max(s_loc, axis=-1, keepdims=True), jnp.max(s_ctx, axis=-1, keepdims=True))
        p_loc = jnp.exp(s_loc - m)
        p_ctx = jnp.exp(s_ctx - m)
        den = jnp.sum(p_loc, axis=-1, keepdims=True) + jnp.sum(p_ctx, axis=-1, keepdims=True)
        o = _dot(p_loc.astype(BF16), vw) + _dot(p_ctx.astype(BF16), vc)
        o_scr[pl.ds(qoff, nq), :] = o / den
        return 0

    lax.fori_loop(0, nblk, step, 0, unroll=2)
    o_ref[0] = (o_scr[...] * _silu(gp_ref[0])).astype(BF16)


def _nat_bias_tables(rpb, rows):
    depth, n_h = rpb.shape[:2]
    c = jnp.arange(GRID_W)[:, None]
    kc = jnp.arange(GRID_W)[None, :]
    cs = jnp.clip(c - NAT_COLS // 2, 0, GRID_W - NAT_COLS)
    valid = (kc >= cs) & (kc < cs + NAT_COLS)
    onehot = ((kc - c + (NAT_COLS - 1))[:, :, None] == jnp.arange(2 * NAT_COLS - 1)[None, None, :]).astype(F32)
    toep = jnp.einsum("dhrj,ckj->dhrck", rpb.astype(F32), onehot, precision=lax.Precision.HIGHEST)
    toep = jnp.where(valid, toep, NEG_INF)
    neg = jnp.full((depth, n_h, GRID_W, GRID_W), NEG_INF, F32)
    pats = []
    for r0, ws in ((0, 0), (NAT_QR, 0), (rows - NAT_QR, rows - NAT_WR)):
        qrows = []
        for qi in range(NAT_QR):
            r = r0 + qi
            rs = min(max(r - NAT_ROWS // 2, 0), rows - NAT_ROWS)
            blocks = []
            for wm in range(NAT_WR):
                kr = ws + wm
                blocks.append(toep[:, :, kr - r + NAT_ROWS - 1] if rs <= kr < rs + NAT_ROWS else neg)
            qrows.append(jnp.concatenate(blocks, axis=-1))
        pats.append(jnp.concatenate(qrows, axis=-2))
    return jnp.stack(pats, axis=2)


def _nat(z, cache_k, cache_v, layer, bias):
    b, s, _ = z.shape
    rows = s // GRID_W
    assert rows % NAT_QR == 0 and rows >= 2 * NAT_WR - NAT_ROWS
    lc = cache_k.shape[2]
    colspec = lambda blk: pl.BlockSpec((1, s, NAT_D), lambda bi, h: (bi, 0, blk + h))
    cspec = pl.BlockSpec((1, 1, lc, NAT_D), lambda bi, h: (bi, layer, 0, h))
    return pl.pallas_call(
        functools.partial(_nat_body, rows=rows),
        grid=(b, NAT_H),
        in_specs=[colspec(OFF_NQ // 128), colspec(OFF_NK // 128), colspec(OFF_NV // 128), cspec, cspec,
                  pl.BlockSpec((1, 1) + bias.shape[2:], lambda bi, h: (layer, h, 0, 0, 0)),
                  colspec((OFF_GP + 2 * BR_W) // 128)],
        out_specs=pl.BlockSpec((1, s, NAT_D), lambda bi, h: (bi, 0, h)),
        out_shape=jax.ShapeDtypeStruct((b, s, NAT_H * NAT_D), BF16),
        scratch_shapes=[pltpu.VMEM((s, NAT_D), BF16), pltpu.VMEM((s, NAT_D), BF16),
                        pltpu.VMEM((s, NAT_D), F32)],
        compiler_params=_cp(("parallel", "parallel"), 48),
        name="nat_attn",
    )(z, z, z, cache_k, cache_v, bias, z)


def _swa_body(q_ref, k_ref, v_ref, kc_ref, vc_ref, sink_ref, cos_ref, sin_ref, gp_ref, o_ref, *, seq):
    scale = SWA_D ** -0.5
    group = SWA_H // SWA_KVH
    win = 3 * BLK
    kc = kc_ref[0, 0].astype(BF16)
    vc = vc_ref[0, 0].astype(BF16)

    def block(sub):
        i = pl.program_id(1) * SWA_QB + sub
        start = pl.multiple_of(jnp.clip((i - 1) * BLK, 0, seq - win), BLK)
        qoff = pl.multiple_of(i * BLK, BLK)
        rows = slice(sub * BLK, (sub + 1) * BLK)
        cq = cos_ref[pl.ds(qoff, BLK), :]
        sq = sin_ref[pl.ds(qoff, BLK), :]
        q = q_ref[0, rows, :]
        q = (q * jnp.concatenate([cq] * (SWA_H // 2), axis=-1)
             + _swap_halves(q) * jnp.concatenate([sq] * (SWA_H // 2), axis=-1))
        q = q * scale
        kw = k_ref[0, pl.ds(start, win), :]
        kw = (kw * cos_ref[pl.ds(start, win), :] + _swap_halves(kw) * sin_ref[pl.ds(start, win), :]).astype(BF16)
        vw = v_ref[0, pl.ds(start, win), :].astype(BF16)

        qpos = qoff + lax.broadcasted_iota(jnp.int32, (BLK, win), 0)
        kpos = start + lax.broadcasted_iota(jnp.int32, (BLK, win), 1)
        valid = jnp.abs(qpos - kpos) <= SWA_WINDOW
        valid = jnp.concatenate([valid] * group, axis=0)

        outs = []
        for g in range(SWA_KVH):
            heads = range(g * group, (g + 1) * group)
            qg = jnp.concatenate([q[:, h * SWA_D:(h + 1) * SWA_D] for h in heads], axis=0).astype(BF16)
            sink = jnp.concatenate([jnp.broadcast_to(sink_ref[0:1, h:h + 1], (BLK, 1)) for h in heads], axis=0)
            ksl = slice(g * SWA_D, (g + 1) * SWA_D)
            s_loc = jnp.where(valid, _dot_nt(qg, kw[:, ksl]), NEG_INF)
            s_ctx = _dot_nt(qg, kc[:, ksl])
            m = jnp.maximum(jnp.max(s_loc, axis=-1, keepdims=True), jnp.max(s_ctx, axis=-1, keepdims=True))
            m = jnp.maximum(m, sink)
            p_loc = jnp.exp(s_loc - m)
            p_ctx = jnp.exp(s_ctx - m)
            den = (jnp.sum(p_loc, axis=-1, keepdims=True) + jnp.sum(p_ctx, axis=-1, keepdims=True)
                   + jnp.exp(sink - m))
            o = (_dot(p_loc.astype(BF16), vw[:, ksl]) + _dot(p_ctx.astype(BF16), vc[:, ksl])) / den
            outs += [o[t * BLK:(t + 1) * BLK] for t in range(group)]
        o = jnp.concatenate(outs, axis=-1)
        o_ref[0, rows, :] = (o * _silu(gp_ref[0, rows, :])).astype(BF16)

    for sub in range(SWA_QB):
        block(sub)


def _swa(z, cache_k, cache_v, layer, sink, cos2, sin2):
    b, s, _ = z.shape
    lc = cache_k.shape[2]
    hd = SWA_H * SWA_D
    kd = SWA_KVH * SWA_D
    cspec = pl.BlockSpec((1, 1, lc, kd), lambda bi, i: (bi, layer, 0, 0))
    tq = SWA_QB * BLK
    return pl.pallas_call(
        functools.partial(_swa_body, seq=s),
        grid=(b, s // tq),
        in_specs=[pl.BlockSpec((1, tq, hd), lambda bi, i: (bi, i, OFF_SQ // hd)),
                  pl.BlockSpec((1, s, kd), lambda bi, i: (bi, 0, OFF_SK // kd)),
                  pl.BlockSpec((1, s, kd), lambda bi, i: (bi, 0, OFF_SV // kd)),
                  cspec, cspec,
                  pl.BlockSpec((1, SWA_H), lambda bi, i: (0, 0)),
                  pl.BlockSpec((s, 128), lambda bi, i: (0, 0)),
                  pl.BlockSpec((s, 128), lambda bi, i: (0, 0)),
                  pl.BlockSpec((1, tq, hd), lambda bi, i: (bi, i, (OFF_GP + 3 * BR_W) // hd))],
        out_specs=pl.BlockSpec((1, tq, hd), lambda bi, i: (bi, i, 0)),
        out_shape=jax.ShapeDtypeStruct((b, s, hd), BF16),
        compiler_params=_cp(("parallel", "arbitrary"), 48),
        name="swa_attn",
    )(z, z, z, cache_k, cache_v, sink.reshape(1, SWA_H), cos2, sin2, z)


def _merge_body(h_ref, o0_ref, o1_ref, o2_ref, o3_ref, g0_ref, g1_ref, g2_ref, g3_ref, wb_ref, y_ref):
    h = h_ref[...]
    o_refs = (o0_ref, o1_ref, o2_ref, o3_ref)
    g_refs = (g0_ref, g1_ref, g2_ref, g3_ref)
    y = None
    for n in range(N_BRANCH):
        t = _sigmoid(_dot(h, g_refs[n][...])) * _dot(o_refs[n][...], wb_ref[n])
        y = t if y is None else y + t
    y_ref[...] = y.astype(BF16)


def _merge(h, ogs, w_gate, w_branch, layer, tm, tn):
    m, d = h.shape
    tm = min(tm, m)
    tn = min(tn, d)
    nj = d // tn
    gspec = lambda n: pl.BlockSpec((None, d, tn), lambda i, j: (layer, 0, n * nj + j))
    return pl.pallas_call(
        _merge_body,
        grid=(m // tm, nj),
        in_specs=[pl.BlockSpec((tm, d), lambda i, j: (i, 0))]
        + [pl.BlockSpec((tm, BR_W), lambda i, j: (i, 0))] * N_BRANCH
        + [gspec(n) for n in range(N_BRANCH)]
        + [pl.BlockSpec((None, N_BRANCH, BR_W, tn), lambda i, j: (layer, 0, 0, j))],
        out_specs=pl.BlockSpec((tm, tn), lambda i, j: (i, j)),
        out_shape=jax.ShapeDtypeStruct((m, d), BF16),
        compiler_params=_cp(("parallel", "arbitrary"), 56),
        name="merge",
    )(h, *ogs, w_gate, w_gate, w_gate, w_gate, w_branch)


def _out_body(*refs, has_next):
    if has_next:
        y_ref, w_ref, x_ref, gate_ref, g_ref, gn_ref, sh_ref, sc_ref, o_ref, h_ref = refs
    else:
        y_ref, w_ref, x_ref, gate_ref, g_ref, o_ref = refs
    t = _dot(y_ref[0], w_ref[...])
    t = t * lax.rsqrt(jnp.mean(t * t, axis=-1, keepdims=True) + EPS) * g_ref[...]
    x = x_ref[0] + gate_ref[0] * t
    o_ref[0] = x
    if has_next:
        y = x * lax.rsqrt(jnp.mean(x * x, axis=-1, keepdims=True) + EPS)
        y = y * gn_ref[...]
        h_ref[0] = (y * (1.0 + sc_ref[0]) + sh_ref[0]).astype(BF16)


def _out_proj(y, w_out, layer, x, gate, g, nxt):
    b, s, d = x.shape
    ts = min(s, 512)
    per_b = gate.shape[0] > 1
    cidx = (lambda bi, i: (bi, 0, 0)) if per_b else (lambda bi, i: (0, 0, 0))
    tile = pl.BlockSpec((1, ts, d), lambda bi, i: (bi, i, 0))
    vec = pl.BlockSpec((1, d), lambda bi, i: (0, 0))
    in_specs = [tile, pl.BlockSpec((None, d, d), lambda bi, i: (layer, 0, 0), pipeline_mode=pl.Buffered(1)),
                tile, pl.BlockSpec((1, 1, d), cidx), vec]
    args = [y, w_out, x, gate, g.reshape(1, d)]
    out_specs = [tile]
    out_shape = [jax.ShapeDtypeStruct((b, s, d), F32)]
    if nxt is not None:
        gn, shift, scale = nxt
        in_specs += [vec, pl.BlockSpec((1, 1, d), cidx), pl.BlockSpec((1, 1, d), cidx)]
        args += [gn.reshape(1, d), shift, scale]
        out_specs.append(tile)
        out_shape.append(jax.ShapeDtypeStruct((b, s, d), BF16))
    res = pl.pallas_call(
        functools.partial(_out_body, has_next=nxt is not None),
        grid=(b, s // ts),
        in_specs=in_specs,
        out_specs=out_specs,
        out_shape=out_shape,
        compiler_params=_cp(("parallel", "parallel"), 56),
        name="out_proj",
    )(*args)
    return (res[0], res[1]) if nxt is not None else (res[0], None)


def _relayout_w_in(w_in):
    c = {}
    o = 0
    for name, w in (("qa", MLA_Q_LORA), ("kva", MLA_KV_LORA), ("kr", MLA_ROPE),
                    ("rq", 512), ("rk", 512), ("rv", 512), ("nq", 512), ("nk", 512), ("nv", 512),
                    ("sq", 512), ("sk", 128), ("sv", 128), ("gp", N_BRANCH * BR_W)):
        c[name] = w_in[:, :, o:o + w]
        o += w
    depth, d, _ = w_in.shape
    order = ["qa", "rq", "rk", "rv", "nq", "nk", "nv", "sq", "gp", "kva", "sk", "sv", "kr"]
    parts = [c[n].astype(BF16) for n in order] + [jnp.zeros((depth, d, N_MIX - N_MIX_USED + MLA_ROPE), BF16)]
    return jnp.concatenate(parts, axis=-1), w_in[:, :, o:].astype(BF16)


def _mla_weights(w_q_up, w_kv_up):
    depth = w_q_up.shape[0]
    wq = w_q_up.reshape(depth, MLA_Q_LORA, MLA_H, MLA_NOPE + MLA_ROPE)
    wq = jnp.pad(wq, ((0, 0), (0, 0), (0, 0), (0, MLA_DQ - MLA_NOPE - MLA_ROPE)))
    wq = wq.reshape(depth, MLA_Q_LORA, MLA_H * MLA_DQ).astype(BF16)
    wkv = w_kv_up.reshape(depth, MLA_KV_LORA, MLA_H, MLA_NOPE + MLA_V)
    wk = jnp.pad(wkv[..., :MLA_NOPE], ((0, 0), (0, 0), (0, 0), (0, MLA_DQ - MLA_NOPE)))
    wk = wk.reshape(depth, MLA_KV_LORA, MLA_H * MLA_DQ).astype(BF16)
    wv = jnp.pad(wkv[..., MLA_NOPE:], ((0, 0), (0, 0), (0, 0), (0, MLA_DV - MLA_V)))
    wv = wv.reshape(depth, MLA_KV_LORA, MLA_H * MLA_DV).astype(BF16)
    j = jnp.arange(128)[:, None]
    cidx = jnp.arange(MLA_H * MLA_DQ)[None, :]
    p128 = ((cidx % MLA_DQ == MLA_NOPE + j) & (j < MLA_ROPE)).astype(BF16)
    vones = (jnp.arange(MLA_H * MLA_DV)[None, :] % MLA_DV == MLA_V).astype(F32)
    return wq, wk, wv, p128, vones


def _rope_tables(n_tok):
    pos = jnp.arange(n_tok)
    row = (pos // GRID_W).astype(F32)
    col = (pos % GRID_W).astype(F32)
    n_freq = MLA_ROPE // 4
    inv = ROPE_BASE ** (-jnp.arange(n_freq, dtype=F32) / n_freq)
    ang = jnp.concatenate([row[:, None] * inv[None], col[:, None] * inv[None]], axis=-1)
    cos, sin = jnp.cos(ang), jnp.sin(ang)
    c64 = jnp.concatenate([cos, cos], axis=-1)
    s64 = jnp.concatenate([-sin, sin], axis=-1)
    one, zero = jnp.ones_like(c64), jnp.zeros_like(c64)
    cq = jnp.concatenate([one, one, c64, one], axis=-1)
    sq = jnp.concatenate([zero, zero, s64, zero], axis=-1)
    ck = jnp.concatenate([c64, one], axis=-1)
    sk = jnp.concatenate([s64, zero], axis=-1)
    c2 = jnp.concatenate([c64, c64], axis=-1)
    s2 = jnp.concatenate([s64, s64], axis=-1)
    return (cq, sq, ck, sk), (c2, s2)


TM_IN = 2048
TN_IN = 768
TM_MERGE = 1024
TN_MERGE = 512


def kernel(x_prompt, x_sample, cache_mla_ckv, cache_mla_krope, state_ret, cache_nat_k, cache_nat_v,
           cache_swa_k, cache_swa_v, c, c_ctx, w_mod, b_mod, norm_pre, norm_post, w_in, mla_q_norm,
           mla_kv_norm, mla_w_q_up, mla_w_kv_up, ret_decay, ret_norm, nat_rpb, swa_sink, w_branch, w_out):
    depth = w_mod.shape[0]
    d = x_prompt.shape[-1]
    nb = x_sample.shape[0]
    s_lat = x_sample.shape[1]
    l_ctx = cache_mla_ckv.shape[2]

    w_mix, w_gate = _relayout_w_in(w_in)
    w_br = w_branch.astype(BF16)
    w_o = w_out.astype(BF16)
    wq, wk, wv, p128, vones = _mla_weights(mla_w_q_up, mla_w_kv_up)
    p64 = p128[:MLA_ROPE]
    mla_tabs, (cos2, sin2) = _rope_tables(s_lat)
    nat_bias = _nat_bias_tables(nat_rpb, s_lat // GRID_W)
    nat_k_c = cache_nat_k.reshape(nb, depth, l_ctx, NAT_H * NAT_D)
    nat_v_c = cache_nat_v.reshape(nb, depth, l_ctx, NAT_H * NAT_D)
    swa_k_c = cache_swa_k.reshape(nb, depth, l_ctx, SWA_KVH * SWA_D)
    swa_v_c = cache_swa_v.reshape(nb, depth, l_ctx, SWA_KVH * SWA_D)

    rows = 8
    conds = jnp.concatenate([c_ctx[None, :], c, jnp.zeros((rows - 1 - nb, d), F32)], axis=0)
    mod = _modulation(conds, w_mod, b_mod)

    def mod_parts(l, lo, hi):
        m = mod[l, lo:hi]
        return m[:, None, :d], m[:, None, d:2 * d], m[:, None, 2 * d:]

    def finish_layer(l, lo, hi, x, h, ogs):
        b, s, _ = x.shape
        gate = mod_parts(l, lo, hi)[2]
        y = _merge(h.reshape(b * s, d), [o.reshape(b * s, BR_W) for o in ogs], w_gate, w_br, l, TM_MERGE, TN_MERGE)
        nxt = None
        if l + 1 < depth:
            shift_n, scale_n, _ = mod_parts(l + 1, lo, hi)
            nxt = (norm_pre[l + 1], shift_n, scale_n)
        return _out_proj(y.reshape(b, s, d), w_o, l, x, gate, norm_post[l], nxt)

    def start_pass(x, lo, hi):
        shift, scale, _ = mod_parts(0, lo, hi)
        return _modnorm(x, norm_pre[0], shift, scale)

    def in_proj(x, h, l):
        b, s, _ = x.shape
        return _in_proj(h.reshape(b * s, d), w_mix, l, TM_IN, TN_IN).reshape(b, s, N_MIX)

    yp = x_prompt
    bc, sc, _ = yp.shape
    ckv_l, kr_l, st_l, nk_l, nv_l, sk_l, sv_l = [], [], [], [], [], [], []
    h = start_pass(yp, 0, 1)
    for l in range(depth):
        z = in_proj(yp, h, l)
        q, kf, vf, ckv = _mla_prep(z, mla_q_norm[l], mla_kv_norm[l], wq[l], wk[l], wv[l], p128, vones, None)
        og_mla = _mla_attn(q, kf, vf, z, OFF_GP // BR_W)
        og_ret, st = _retention(z, ret_decay[l], ret_norm[l], None, 0, True)
        og_nat = _flash(z, OFF_NQ // 512, z, OFF_NK // 512, z, OFF_NV // 512, z, (OFF_GP + 2 * BR_W) // BR_W,
                        n_h=NAT_H, n_kvh=NAT_H, dq=NAT_D, dv=NAT_D, scale=NAT_D ** -0.5)
        og_swa = _flash(z, OFF_SQ // 512, z, OFF_SK // 128, z, OFF_SV // 128, z, (OFF_GP + 3 * BR_W) // BR_W,
                        n_h=SWA_H, n_kvh=SWA_KVH, dq=SWA_D, dv=SWA_D, scale=SWA_D ** -0.5, sink=swa_sink[l])
        yp, h = finish_layer(l, 0, 1, yp, h, (og_mla, og_ret, og_nat, og_swa))
        ckv_l.append(ckv)
        kr_l.append(z[:, :, OFF_KR:OFF_KR + MLA_ROPE])
        st_l.append(st)
        nk_l.append(z[:, :, OFF_NK:OFF_NK + 512].reshape(bc, sc, NAT_H, NAT_D))
        nv_l.append(z[:, :, OFF_NV:OFF_NV + 512].reshape(bc, sc, NAT_H, NAT_D))
        sk_l.append(z[:, :, OFF_SK:OFF_SK + 128].reshape(bc, sc, SWA_KVH, SWA_D))
        sv_l.append(z[:, :, OFF_SV:OFF_SV + 128].reshape(bc, sc, SWA_KVH, SWA_D))

    ys = x_sample
    h = start_pass(ys, 1, 1 + nb)
    for l in range(depth):
        z = in_proj(ys, h, l)
        q, kf, vf, _ = _mla_prep(z, mla_q_norm[l], mla_kv_norm[l], wq[l], wk[l], wv[l], p128, vones, mla_tabs)
        kc, vc = _mla_expand(cache_mla_ckv, cache_mla_krope, l, wk[l], wv[l], p64, vones)
        og_mla = _mla_attn(q, kf, vf, z, OFF_GP // BR_W, ctx=(kc, vc))
        og_ret, _ = _retention(z, ret_decay[l], ret_norm[l], state_ret, l, False)
        og_nat = _nat(z, nat_k_c, nat_v_c, l, nat_bias)
        og_swa = _swa(z, swa_k_c, swa_v_c, l, swa_sink[l], cos2, sin2)
        ys, h = finish_layer(l, 1, 1 + nb, ys, h, (og_mla, og_ret, og_nat, og_swa))

    return (yp, ys, jnp.stack(ckv_l, axis=1), jnp.stack(kr_l, axis=1), jnp.stack(st_l, axis=1),
            jnp.stack(nk_l, axis=1), jnp.stack(nv_l, axis=1), jnp.stack(sk_l, axis=1), jnp.stack(sv_l, axis=1))
```

```python
import functools

import jax
import jax.numpy as jnp
from jax import lax
from jax.experimental import pallas as pl
from jax.experimental.pallas import tpu as pltpu

F32 = jnp.float32
BF16 = jnp.bfloat16

GRID_W = 64
BLK = 128
ROPE_BASE = 10000.0
EPS = 1e-6
NEG_INF = -1e30
N_BRANCH = 4
BR_W = 512

MLA_H = 4
MLA_Q_LORA = 512
MLA_KV_LORA = 256
MLA_NOPE = 128
MLA_ROPE = 64
MLA_V = 128
MLA_DQ = 256
MLA_DV = 256
MLA_QSCALE = (MLA_NOPE + MLA_ROPE) ** -0.5 * 1.4426950408889634

RET_H = 4
RET_DK = 128
RET_DV = 128
RET_HPS = 2

NAT_H = 4
NAT_D = 128
NAT_ROWS = 8
NAT_COLS = 16
NAT_QR = 4
NAT_WR = NAT_ROWS + NAT_QR

SWA_H = 8
SWA_KVH = 2
SWA_D = 64
SWA_WINDOW = 128
SWA_QB = 1

OFF_QA = 0
OFF_RQ = 512
OFF_RK = 1024
OFF_RV = 1536
OFF_NQ = 2048
OFF_NK = 2560
OFF_NV = 3072
OFF_SQ = 3584
OFF_GP = 4096
OFF_KVA = 6144
OFF_SK = 6400
OFF_SV = 6528
OFF_KR = 6656
N_MIX_USED = 6784
N_MIX = 6912

V7X_VMEM_BYTES = 64 * 1024 * 1024


def _cp(sem, vmem_mb):
    assert vmem_mb * 1024 * 1024 < V7X_VMEM_BYTES
    return pltpu.CompilerParams(dimension_semantics=sem, vmem_limit_bytes=vmem_mb * 1024 * 1024)


def _sigmoid(x):
    return 1.0 / (1.0 + jnp.exp(-x))


def _silu(x):
    return x * _sigmoid(x)


def _swap_halves(x):
    n = x.shape[-1]
    ax = x.ndim - 1
    lane = lax.broadcasted_iota(jnp.int32, x.shape, ax)
    return jnp.where((lane & 32) == 0, pltpu.roll(x, n - 32, ax), pltpu.roll(x, 32, ax))


def _dot_nt(a, b):
    return lax.dot_general(a, b, (((1,), (1,)), ((), ())), preferred_element_type=F32)


def _dot(a, b):
    return jnp.dot(a, b, preferred_element_type=F32)


def _mod_body(c_ref, w_ref, b_ref, o_ref):
    a = _silu(c_ref[...]).astype(BF16)
    o_ref[0] = _dot(a, w_ref[0].astype(BF16)) + b_ref[0]


def _modulation(conds, w_mod, b_mod):
    depth, d, n = w_mod.shape
    rows = conds.shape[0]
    tn = 1024 if n % 1024 == 0 else n
    return pl.pallas_call(
        _mod_body,
        grid=(depth, n // tn),
        in_specs=[pl.BlockSpec((rows, d), lambda l, j: (0, 0)),
                  pl.BlockSpec((1, d, tn), lambda l, j: (l, 0, j)),
                  pl.BlockSpec((1, 1, tn), lambda l, j: (l, 0, j))],
        out_specs=pl.BlockSpec((1, rows, tn), lambda l, j: (l, 0, j)),
        out_shape=jax.ShapeDtypeStruct((depth, rows, n), F32),
        compiler_params=_cp(("parallel", "parallel"), 40),
        name="modulation",
    )(conds, w_mod, b_mod.reshape(depth, 1, n))


def _modnorm_body(x_ref, g_ref, sh_ref, sc_ref, h_ref):
    x = x_ref[0]
    y = x * lax.rsqrt(jnp.mean(x * x, axis=-1, keepdims=True) + EPS)
    y = y * g_ref[...]
    h_ref[0] = (y * (1.0 + sc_ref[0]) + sh_ref[0]).astype(BF16)


def _modnorm(x, g, shift, scale):
    b, s, d = x.shape
    ts = min(s, 512)
    per_b = shift.shape[0] > 1
    cidx = (lambda bi, i: (bi, 0, 0)) if per_b else (lambda bi, i: (0, 0, 0))
    return pl.pallas_call(
        _modnorm_body,
        grid=(b, s // ts),
        in_specs=[pl.BlockSpec((1, ts, d), lambda bi, i: (bi, i, 0)),
                  pl.BlockSpec((1, d), lambda bi, i: (0, 0)),
                  pl.BlockSpec((1, 1, d), cidx),
                  pl.BlockSpec((1, 1, d), cidx)],
        out_specs=pl.BlockSpec((1, ts, d), lambda bi, i: (bi, i, 0)),
        out_shape=jax.ShapeDtypeStruct((b, s, d), BF16),
        compiler_params=_cp(("parallel", "parallel"), 32),
        name="modnorm",
    )(x, g.reshape(1, d), shift, scale)


def _mm_body(a_ref, b_ref, o_ref):
    o_ref[...] = _dot(a_ref[...], b_ref[...])


def _in_proj(a, w_mix, layer, tm, tn):
    m, k = a.shape
    tm = min(tm, m)
    return pl.pallas_call(
        _mm_body,
        grid=(m // tm, N_MIX // tn),
        in_specs=[pl.BlockSpec((tm, k), lambda i, j: (i, 0)),
                  pl.BlockSpec((None, k, tn), lambda i, j: (layer, 0, j))],
        out_specs=pl.BlockSpec((tm, tn), lambda i, j: (i, j)),
        out_shape=jax.ShapeDtypeStruct((m, N_MIX), F32),
        compiler_params=_cp(("parallel", "arbitrary"), 48),
        name="in_proj",
    )(a, w_mix)


def _mla_prep_body(*refs, rope):
    if rope:
        (qa_ref, kva_ref, kr_ref, gq_ref, gkv_ref, wq_ref, wk_ref, wv_ref, p_ref, one_ref,
         cq_ref, sq_ref, ck_ref, sk_ref, q_out, k_out, v_out, ckv_out) = refs
    else:
        (qa_ref, kva_ref, kr_ref, gq_ref, gkv_ref, wq_ref, wk_ref, wv_ref, p_ref, one_ref,
         q_out, k_out, v_out, ckv_out) = refs
    qa = qa_ref[0]
    cq = qa * lax.rsqrt(jnp.mean(qa * qa, axis=-1, keepdims=True) + EPS) * gq_ref[...]
    q = _dot(cq.astype(BF16), wq_ref[...])
    kva = kva_ref[0]
    ckv = kva * lax.rsqrt(jnp.mean(kva * kva, axis=-1, keepdims=True) + EPS) * gkv_ref[...]
    ckv_out[0] = ckv
    kr = kr_ref[0]
    if rope:
        cq_t = jnp.concatenate([cq_ref[...]] * MLA_H, axis=-1)
        sq_t = jnp.concatenate([sq_ref[...]] * MLA_H, axis=-1)
        q = q * cq_t + _swap_halves(q) * sq_t
        kr = kr * ck_ref[...] + _swap_halves(kr) * sk_ref[...]
    q_out[0] = (q * MLA_QSCALE).astype(BF16)
    cb = ckv.astype(BF16)
    k_out[0] = (_dot(cb, wk_ref[...]) + _dot(kr.astype(BF16), p_ref[...])).astype(BF16)
    v_out[0] = (_dot(cb, wv_ref[...]) + one_ref[...]).astype(BF16)


def _mla_prep(z, gq, gkv, wq, wk, wv, p128, vones, tabs):
    b, s, _ = z.shape
    tm = min(s, 512)
    rope = tabs is not None
    full = lambda shape: pl.BlockSpec(shape, lambda bi, i: (0,) * len(shape))
    in_specs = [pl.BlockSpec((1, tm, MLA_Q_LORA), lambda bi, i: (bi, i, OFF_QA // MLA_Q_LORA)),
                pl.BlockSpec((1, tm, MLA_KV_LORA), lambda bi, i: (bi, i, OFF_KVA // MLA_KV_LORA)),
                pl.BlockSpec((1, tm, 128), lambda bi, i: (bi, i, OFF_KR // 128)),
                full((1, MLA_Q_LORA)), full((1, MLA_KV_LORA)),
                full(wq.shape), full(wk.shape), full(wv.shape), full(p128.shape), full(vones.shape)]
    args = [z, z, z, gq.reshape(1, -1), gkv.reshape(1, -1), wq, wk, wv, p128, vones]
    if rope:
        cq, sq, ck, sk = tabs
        in_specs += [pl.BlockSpec((tm, MLA_DQ), lambda bi, i: (i, 0))] * 2
        in_specs += [pl.BlockSpec((tm, 128), lambda bi, i: (i, 0))] * 2
        args += [cq, sq, ck, sk]
    hq = MLA_H * MLA_DQ
    hv = MLA_H * MLA_DV
    return pl.pallas_call(
        functools.partial(_mla_prep_body, rope=rope),
        grid=(b, s // tm),
        in_specs=in_specs,
        out_specs=[pl.BlockSpec((1, tm, hq), lambda bi, i: (bi, i, 0)),
                   pl.BlockSpec((1, tm, hq), lambda bi, i: (bi, i, 0)),
                   pl.BlockSpec((1, tm, hv), lambda bi, i: (bi, i, 0)),
                   pl.BlockSpec((1, tm, MLA_KV_LORA), lambda bi, i: (bi, i, 0))],
        out_shape=[jax.ShapeDtypeStruct((b, s, hq), BF16),
                   jax.ShapeDtypeStruct((b, s, hq), BF16),
                   jax.ShapeDtypeStruct((b, s, hv), BF16),
                   jax.ShapeDtypeStruct((b, s, MLA_KV_LORA), F32)],
        compiler_params=_cp(("parallel", "parallel"), 40),
        name="mla_prep",
    )(*args)


def _mla_expand_body(ckv_ref, kr_ref, wk_ref, wv_ref, p_ref, one_ref, k_out, v_out):
    cb = ckv_ref[0, 0].astype(BF16)
    k_out[0] = (_dot(cb, wk_ref[...]) + _dot(kr_ref[0, 0].astype(BF16), p_ref[...])).astype(BF16)
    v_out[0] = (_dot(cb, wv_ref[...]) + one_ref[...]).astype(BF16)


def _mla_expand(cache_ckv, cache_kr, layer, wk, wv, p64, vones):
    b, _, s, _ = cache_ckv.shape
    full = lambda shape: pl.BlockSpec(shape, lambda bi: (0,) * len(shape))
    hq = MLA_H * MLA_DQ
    hv = MLA_H * MLA_DV
    return pl.pallas_call(
        _mla_expand_body,
        grid=(b,),
        in_specs=[pl.BlockSpec((1, 1, s, MLA_KV_LORA), lambda bi: (bi, layer, 0, 0)),
                  pl.BlockSpec((1, 1, s, MLA_ROPE), lambda bi: (bi, layer, 0, 0)),
                  full(wk.shape), full(wv.shape), full(p64.shape), full(vones.shape)],
        out_specs=[pl.BlockSpec((1, s, hq), lambda bi: (bi, 0, 0)),
                   pl.BlockSpec((1, s, hv), lambda bi: (bi, 0, 0))],
        out_shape=[jax.ShapeDtypeStruct((b, s, hq), BF16),
                   jax.ShapeDtypeStruct((b, s, hv), BF16)],
        compiler_params=_cp(("parallel",), 32),
        name="mla_expand",
    )(cache_ckv, cache_kr, wk, wv, p64, vones)


def _mla_attn_body(*refs, has_ctx, nkv):
    it = iter(refs)
    q_ref, k_ref, v_ref = next(it), next(it), next(it)
    kc_ref = vc_ref = None
    if has_ctx:
        kc_ref, vc_ref = next(it), next(it)
    gp_ref, o_ref, m_scr, acc_scr = next(it), next(it), next(it), next(it)
    j = pl.program_id(2)

    def process(kr, vr):
        tk = kr.shape[1]
        for h in range(MLA_H):
            qk = slice(h * MLA_DQ, (h + 1) * MLA_DQ)
            s = _dot_nt(q_ref[0, :, qk], kr[0, :, qk])
            m_prev = m_scr[h]
            m_new = jnp.maximum(m_prev, jnp.max(s, axis=-1, keepdims=True))
            alpha = jnp.exp2(m_prev - m_new)
            p = jnp.exp2(s - jnp.concatenate([m_new] * (tk // 128), axis=-1))
            pv = _dot(p.astype(BF16), vr[0, :, h * MLA_DV:(h + 1) * MLA_DV])
            acc_scr[h] = jnp.concatenate([alpha] * (MLA_DV // 128), axis=-1) * acc_scr[h] + pv
            m_scr[h] = m_new

    @pl.when(j == 0)
    def _():
        m_scr[...] = jnp.full(m_scr.shape, NEG_INF, F32)
        acc_scr[...] = jnp.zeros(acc_scr.shape, F32)
        if has_ctx:
            process(kc_ref, vc_ref)

    process(k_ref, v_ref)

    @pl.when(j == nkv - 1)
    def _():
        outs = []
        for h in range(MLA_H):
            a = acc_scr[h]
            outs.append(a[:, :MLA_V] / a[:, MLA_V:MLA_V + 1])
        o_ref[0] = (jnp.concatenate(outs, axis=-1) * _silu(gp_ref[0])).astype(BF16)


def _mla_attn(q, k, v, gp, gpcol, ctx=None, tq=1024, tk=1024):
    b, sq, _ = q.shape
    sk = k.shape[1]
    tq = min(tq, sq)
    tk = min(tk, sk)
    nkv = sk // tk
    hq = MLA_H * MLA_DQ
    hv = MLA_H * MLA_DV
    in_specs = [pl.BlockSpec((1, tq, hq), lambda bi, i, j: (bi, i, 0)),
                pl.BlockSpec((1, tk, hq), lambda bi, i, j: (bi, j, 0)),
                pl.BlockSpec((1, tk, hv), lambda bi, i, j: (bi, j, 0))]
    args = [q, k, v]
    if ctx is not None:
        kc, vc = ctx
        in_specs += [pl.BlockSpec((1,) + kc.shape[1:], lambda bi, i, j: (bi, 0, 0)),
                     pl.BlockSpec((1,) + vc.shape[1:], lambda bi, i, j: (bi, 0, 0))]
        args += [kc, vc]
    in_specs += [pl.BlockSpec((1, tq, MLA_H * MLA_V), lambda bi, i, j: (bi, i, gpcol))]
    args += [gp]
    return pl.pallas_call(
        functools.partial(_mla_attn_body, has_ctx=ctx is not None, nkv=nkv),
        grid=(b, sq // tq, nkv),
        in_specs=in_specs,
        out_specs=pl.BlockSpec((1, tq, MLA_H * MLA_V), lambda bi, i, j: (bi, i, 0)),
        out_shape=jax.ShapeDtypeStruct((b, sq, MLA_H * MLA_V), BF16),
        scratch_shapes=[pltpu.VMEM((MLA_H, tq, 128), F32), pltpu.VMEM((MLA_H, tq, MLA_DV), F32)],
        compiler_params=_cp(("parallel", "parallel", "arbitrary"), 48),
        name="mla_attn",
    )(*args)


def _ctx_attn_body(qm_ref, km_ref, vm_ref, nq_ref, nk_ref, nv_ref, sq_ref, sk_ref, sv_ref, sink_ref,
                   gpm_ref, gpn_ref, gps_ref, om_ref, on_ref, os_ref):
    outs = []
    for h in range(MLA_H):
        qk = slice(h * MLA_DQ, (h + 1) * MLA_DQ)
        s = _dot_nt(qm_ref[0, :, qk], km_ref[0, :, qk])
        p = jnp.exp2(s - jnp.max(s, axis=-1, keepdims=True))
        pv = _dot(p.astype(BF16), vm_ref[0, :, h * MLA_DV:(h + 1) * MLA_DV])
        outs.append(pv[:, :MLA_V] / pv[:, MLA_V:MLA_V + 1])
    om_ref[0] = (jnp.concatenate(outs, axis=-1) * _silu(gpm_ref[0])).astype(BF16)

    outs = []
    for h in range(NAT_H):
        sl = slice(h * NAT_D, (h + 1) * NAT_D)
        s = _dot_nt(nq_ref[0, :, sl].astype(BF16), nk_ref[0, :, sl].astype(BF16)) * NAT_D ** -0.5
        p = jnp.exp(s - jnp.max(s, axis=-1, keepdims=True))
        den = jnp.sum(p, axis=-1, keepdims=True)
        outs.append(_dot(p.astype(BF16), nv_ref[0, :, sl].astype(BF16)) / den)
    on_ref[0] = (jnp.concatenate(outs, axis=-1) * _silu(gpn_ref[0])).astype(BF16)

    n_tok = sq_ref.shape[1]
    group = SWA_H // SWA_KVH
    q = sq_ref[0] * SWA_D ** -0.5
    outs = []
    for g in range(SWA_KVH):
        heads = range(g * group, (g + 1) * group)
        qg = jnp.concatenate([q[:, h * SWA_D:(h + 1) * SWA_D] for h in heads], axis=0).astype(BF16)
        sink = jnp.concatenate([jnp.broadcast_to(sink_ref[0:1, h:h + 1], (n_tok, 1)) for h in heads], axis=0)
        ksl = slice(g * SWA_D, (g + 1) * SWA_D)
        s = _dot_nt(qg, sk_ref[0, :, ksl].astype(BF16))
        m = jnp.maximum(jnp.max(s, axis=-1, keepdims=True), sink)
        p = jnp.exp(s - m)
        den = jnp.sum(p, axis=-1, keepdims=True) + jnp.exp(sink - m)
        o = _dot(p.astype(BF16), sv_ref[0, :, ksl].astype(BF16)) / den
        outs += [o[t * n_tok:(t + 1) * n_tok] for t in range(group)]
    os_ref[0] = (jnp.concatenate(outs, axis=-1) * _silu(gps_ref[0])).astype(BF16)


def _ctx_attn(q, k, v, z, sink):
    b, n_tok, _ = z.shape
    blk = lambda w, off: pl.BlockSpec((1, n_tok, w), lambda bi: (bi, 0, off // w))
    kd = SWA_KVH * SWA_D
    out = jax.ShapeDtypeStruct((b, n_tok, BR_W), BF16)
    return pl.pallas_call(
        _ctx_attn_body,
        grid=(b,),
        in_specs=[blk(MLA_H * MLA_DQ, 0), blk(MLA_H * MLA_DQ, 0), blk(MLA_H * MLA_DV, 0),
                  blk(BR_W, OFF_NQ), blk(BR_W, OFF_NK), blk(BR_W, OFF_NV),
                  blk(BR_W, OFF_SQ), blk(kd, OFF_SK), blk(kd, OFF_SV),
                  pl.BlockSpec((1, SWA_H), lambda bi: (0, 0)),
                  blk(BR_W, OFF_GP), blk(BR_W, OFF_GP + 2 * BR_W), blk(BR_W, OFF_GP + 3 * BR_W)],
        out_specs=[blk(BR_W, 0)] * 3,
        out_shape=[out, out, out],
        compiler_params=_cp(("parallel",), 48),
        name="ctx_attn",
    )(q, k, v, z, z, z, z, z, z, sink.reshape(1, SWA_H), z, z, z)


def _ret_body(*refs, nc, has_s0, has_st):
    it = iter(refs)
    dec_ref, norm_ref, q_ref, k_ref, v_ref = next(it), next(it), next(it), next(it), next(it)
    s0_ref = next(it) if has_s0 else None
    gp_ref, o_ref = next(it), next(it)
    st_ref = next(it) if has_st else None
    of_scr, ob_scr = next(it), next(it)
    hgrp = pl.program_id(1)

    d = dec_ref[0]
    lg_all = jnp.minimum(d, 0.0) - jnp.log1p(jnp.exp(-jnp.abs(d)))
    hcol = lax.broadcasted_iota(jnp.int32, d.shape, 1)
    row = lax.broadcasted_iota(jnp.int32, (BLK, BLK), 0).astype(F32)
    col = lax.broadcasted_iota(jnp.int32, (BLK, BLK), 1).astype(F32)
    a = lax.broadcasted_iota(jnp.int32, (BLK, 1), 0).astype(F32)
    kscale = RET_DK ** -0.5

    heads = []
    for hh in range(RET_HPS):
        lg = jnp.sum(jnp.where(hcol == hgrp * RET_HPS + hh, lg_all, 0.0), axis=1, keepdims=True)
        lg_f, lg_b = lg[0:1, :], lg[1:2, :]
        fwd = (jnp.where(row - col >= 0, jnp.exp(lg_f * jnp.maximum(row - col, 0.0)), 0.0),
               jnp.exp(lg_f * (a + 1.0)), jnp.exp(lg_f * (BLK - 1.0 - a)), jnp.exp(lg_f * BLK))
        bwd = (jnp.where(col - row > 0, jnp.exp(lg_b * jnp.maximum(col - row, 0.0)), 0.0),
               jnp.exp(lg_b * (BLK - a)), jnp.exp(lg_b * a), jnp.exp(lg_b * BLK))
        heads.append((slice(hh * RET_DK, (hh + 1) * RET_DK), fwd, bwd))

    def chunk(off, sl, st, tabs):
        dm, qd, kd, cd = tabs
        q = q_ref[0, pl.ds(off, BLK), sl]
        k = k_ref[0, pl.ds(off, BLK), sl] * kscale
        v = v_ref[0, pl.ds(off, BLK), sl].astype(BF16)
        inner = _dot_nt(q.astype(BF16), k.astype(BF16)) * dm
        o = _dot(inner.astype(BF16), v) + _dot((q * qd).astype(BF16), st.astype(BF16))
        st = st * cd + _dot(jnp.transpose(k * kd).astype(BF16), v)
        return o, st

    def step(j, carry):
        off_f = pl.multiple_of(j * BLK, BLK)
        off_b = pl.multiple_of((nc - 1 - j) * BLK, BLK)
        new = []
        for (sl, fwd, bwd), (sf, sb) in zip(heads, carry):
            o, sf = chunk(off_f, sl, sf, fwd)
            of_scr[pl.ds(off_f, BLK), sl] = o
            o, sb = chunk(off_b, sl, sb, bwd)
            ob_scr[pl.ds(off_b, BLK), sl] = o
            new.append((sf, sb))
        return tuple(new)

    if has_s0:
        init = tuple((s0_ref[0, 0, 0, hh], s0_ref[0, 0, 1, hh]) for hh in range(RET_HPS))
    else:
        zero = jnp.zeros((RET_DK, RET_DV), F32)
        init = tuple((zero, zero) for _ in range(RET_HPS))
    fin = lax.fori_loop(0, nc, step, init, unroll=2)
    if has_st:
        for hh in range(RET_HPS):
            st_ref[0, 0, hh] = fin[hh][0]
            st_ref[0, 1, hh] = fin[hh][1]
    for sl, _, _ in heads:
        o = of_scr[:, sl] + ob_scr[:, sl]
        o = o * lax.rsqrt(jnp.mean(o * o, axis=-1, keepdims=True) + EPS) * norm_ref[:, sl]
        o_ref[0, :, sl] = (o * _silu(gp_ref[0, :, sl])).astype(BF16)


def _retention(z, decay, norm, s0, layer, want_state):
    b, s, _ = z.shape
    nc = s // BLK
    w = RET_HPS * RET_DK
    colspec = lambda off: pl.BlockSpec((1, s, w), lambda bi, g: (bi, 0, off // w + g))
    in_specs = [pl.BlockSpec((1, 2, RET_H), lambda bi, g: (0, 0, 0)),
                pl.BlockSpec((1, w), lambda bi, g: (0, g)),
                colspec(OFF_RQ), colspec(OFF_RK), colspec(OFF_RV)]
    args = [decay.reshape(1, 2, RET_H), norm.reshape(1, -1), z, z, z]
    if s0 is not None:
        in_specs.append(pl.BlockSpec((1, 1, 2, RET_HPS, RET_DK, RET_DV), lambda bi, g: (bi, layer, 0, g, 0, 0)))
        args.append(s0)
    in_specs.append(colspec(OFF_GP + BR_W))
    args.append(z)
    out_specs = [pl.BlockSpec((1, s, w), lambda bi, g: (bi, 0, g))]
    out_shape = [jax.ShapeDtypeStruct((b, s, RET_H * RET_DV), BF16)]
    if want_state:
        out_specs.append(pl.BlockSpec((1, 2, RET_HPS, RET_DK, RET_DV), lambda bi, g: (bi, 0, g, 0, 0)))
        out_shape.append(jax.ShapeDtypeStruct((b, 2, RET_H, RET_DK, RET_DV), F32))
    res = pl.pallas_call(
        functools.partial(_ret_body, nc=nc, has_s0=s0 is not None, has_st=want_state),
        grid=(b, RET_H // RET_HPS),
        in_specs=in_specs,
        out_specs=out_specs,
        out_shape=out_shape,
        scratch_shapes=[pltpu.VMEM((s, w), F32), pltpu.VMEM((s, w), F32)],
        compiler_params=_cp(("parallel", "parallel"), 56),
        name="retention",
    )(*args)
    return (res[0], res[1]) if want_state else (res[0], None)


def _nat_body(q_ref, k_ref, v_ref, kc_ref, vc_ref, bias_ref, gp_ref, o_ref, kb_scr, vb_scr, o_scr, *, rows):
    scale = NAT_D ** -0.5
    kb_scr[...] = k_ref[0].astype(BF16)
    vb_scr[...] = v_ref[0].astype(BF16)
    kc = kc_ref[0, 0].astype(BF16)
    vc = vc_ref[0, 0].astype(BF16)
    nq = NAT_QR * GRID_W
    win = NAT_WR * GRID_W
    nblk = rows // NAT_QR

    def step(t, _):
        r0 = t * NAT_QR
        ws = jnp.clip(r0 - NAT_ROWS // 2, 0, rows - NAT_WR)
        pat = jnp.where(t == 0, 0, jnp.where(t == nblk - 1, 2, 1))
        qoff = pl.multiple_of(r0 * GRID_W, nq)
        koff = pl.multiple_of(ws * GRID_W, GRID_W)
        q = q_ref[0, pl.ds(qoff, nq), :].astype(BF16)
        kw = kb_scr[pl.ds(koff, win), :]
        vw = vb_scr[pl.ds(koff, win), :]
        s_loc = _dot_nt(q, kw) * scale + bias_ref[0, 0, pat]
        s_ctx = _dot_nt(q, kc) * scale
        m = jnp.maximum(jnp.max(s_loc, axis=-1, keepdims=True), jnp.max(s_ctx, axis=-1, keepdims=True))
        p_loc = jnp.exp(s_loc - m)
        p_ctx = jnp.exp(s_ctx - m)
        den = jnp.sum(p_loc, axis=-1, keepdims=True) + jnp.sum(p_ctx, axis=-1, keepdims=True)
        o = _dot(p_loc.astype(BF16), vw) + _dot(p_ctx.astype(BF16), vc)
        o_scr[pl.ds(qoff, nq), :] = o / den
        return 0

    lax.fori_loop(0, nblk, step, 0, unroll=2)
    o_ref[0] = (o_scr[...] * _silu(gp_ref[0])).astype(BF16)


def _nat_bias_tables(rpb, rows):
    depth, n_h = rpb.shape[:2]
    c = jnp.arange(GRID_W)[:, None]
    kc = jnp.arange(GRID_W)[None, :]
    cs = jnp.clip(c - NAT_COLS // 2, 0, GRID_W - NAT_COLS)
    valid = (kc >= cs) & (kc < cs + NAT_COLS)
    onehot = ((kc - c + (NAT_COLS - 1))[:, :, None] == jnp.arange(2 * NAT_COLS - 1)[None, None, :]).astype(F32)
    toep = jnp.einsum("dhrj,ckj->dhrck", rpb.astype(F32), onehot, precision=lax.Precision.HIGHEST)
    toep = jnp.where(valid, toep, NEG_INF)
    neg = jnp.full((depth, n_h, GRID_W, GRID_W), NEG_INF, F32)
    pats = []
    for r0, ws in ((0, 0), (NAT_QR, 0), (rows - NAT_QR, rows - NAT_WR)):
        qrows = []
        for qi in range(NAT_QR):
            r = r0 + qi
            rs = min(max(r - NAT_ROWS // 2, 0), rows - NAT_ROWS)
            blocks = []
            for wm in range(NAT_WR):
                kr = ws + wm
                blocks.append(toep[:, :, kr - r + NAT_ROWS - 1] if rs <= kr < rs + NAT_ROWS else neg)
            qrows.append(jnp.concatenate(blocks, axis=-1))
        pats.append(jnp.concatenate(qrows, axis=-2))
    return jnp.stack(pats, axis=2)


def _nat(z, cache_k, cache_v, layer, bias):
    b, s, _ = z.shape
    rows = s // GRID_W
    assert rows % NAT_QR == 0 and rows >= 2 * NAT_WR - NAT_ROWS
    lc = cache_k.shape[2]
    colspec = lambda blk: pl.BlockSpec((1, s, NAT_D), lambda bi, h: (bi, 0, blk + h))
    cspec = pl.BlockSpec((1, 1, lc, NAT_D), lambda bi, h: (bi, layer, 0, h))
    return pl.pallas_call(
        functools.partial(_nat_body, rows=rows),
        grid=(b, NAT_H),
        in_specs=[colspec(OFF_NQ // 128), colspec(OFF_NK // 128), colspec(OFF_NV // 128), cspec, cspec,
                  pl.BlockSpec((1, 1) + bias.shape[2:], lambda bi, h: (layer, h, 0, 0, 0)),
                  colspec((OFF_GP + 2 * BR_W) // 128)],
        out_specs=pl.BlockSpec((1, s, NAT_D), lambda bi, h: (bi, 0, h)),
        out_shape=jax.ShapeDtypeStruct((b, s, NAT_H * NAT_D), BF16),
        scratch_shapes=[pltpu.VMEM((s, NAT_D), BF16), pltpu.VMEM((s, NAT_D), BF16),
                        pltpu.VMEM((s, NAT_D), F32)],
        compiler_params=_cp(("parallel", "parallel"), 48),
        name="nat_attn",
    )(z, z, z, cache_k, cache_v, bias, z)


def _swa_body(q_ref, k_ref, v_ref, kc_ref, vc_ref, sink_ref, cos_ref, sin_ref, gp_ref, o_ref, *, seq):
    scale = SWA_D ** -0.5
    group = SWA_H // SWA_KVH
    win = 3 * BLK
    kc = kc_ref[0, 0].astype(BF16)
    vc = vc_ref[0, 0].astype(BF16)

    def block(sub):
        i = pl.program_id(1) * SWA_QB + sub
        start = pl.multiple_of(jnp.clip((i - 1) * BLK, 0, seq - win), BLK)
        qoff = pl.multiple_of(i * BLK, BLK)
        rows = slice(sub * BLK, (sub + 1) * BLK)
        cq = cos_ref[pl.ds(qoff, BLK), :]
        sq = sin_ref[pl.ds(qoff, BLK), :]
        q = q_ref[0, rows, :]
        q = (q * jnp.concatenate([cq] * (SWA_H // 2), axis=-1)
             + _swap_halves(q) * jnp.concatenate([sq] * (SWA_H // 2), axis=-1))
        q = q * scale
        kw = k_ref[0, pl.ds(start, win), :]
        kw = (kw * cos_ref[pl.ds(start, win), :] + _swap_halves(kw) * sin_ref[pl.ds(start, win), :]).astype(BF16)
        vw = v_ref[0, pl.ds(start, win), :].astype(BF16)

        qpos = qoff + lax.broadcasted_iota(jnp.int32, (BLK, win), 0)
        kpos = start + lax.broadcasted_iota(jnp.int32, (BLK, win), 1)
        valid = jnp.abs(qpos - kpos) <= SWA_WINDOW
        valid = jnp.concatenate([valid] * group, axis=0)

        outs = []
        for g in range(SWA_KVH):
            heads = range(g * group, (g + 1) * group)
            qg = jnp.concatenate([q[:, h * SWA_D:(h + 1) * SWA_D] for h in heads], axis=0).astype(BF16)
            sink = jnp.concatenate([jnp.broadcast_to(sink_ref[0:1, h:h + 1], (BLK, 1)) for h in heads], axis=0)
            ksl = slice(g * SWA_D, (g + 1) * SWA_D)
            s_loc = jnp.where(valid, _dot_nt(qg, kw[:, ksl]), NEG_INF)
            s_ctx = _dot_nt(qg, kc[:, ksl])
            m = jnp.maximum(jnp.max(s_loc, axis=-1, keepdims=True), jnp.max(s_ctx, axis=-1, keepdims=True))
            m = jnp.maximum(m, sink)
            p_loc = jnp.exp(s_loc - m)
            p_ctx = jnp.exp(s_ctx - m)
            den = (jnp.sum(p_loc, axis=-1, keepdims=True) + jnp.sum(p_ctx, axis=-1, keepdims=True)
                   + jnp.exp(sink - m))
            o = (_dot(p_loc.astype(BF16), vw[:, ksl]) + _dot(p_ctx.astype(BF16), vc[:, ksl])) / den
            outs += [o[t * BLK:(t + 1) * BLK] for t in range(group)]
        o = jnp.concatenate(outs, axis=-1)
        o_ref[0, rows, :] = (o * _silu(gp_ref[0, rows, :])).astype(BF16)

    for sub in range(SWA_QB):
        block(sub)


def _swa(z, cache_k, cache_v, layer, sink, cos2, sin2):
    b, s, _ = z.shape
    lc = cache_k.shape[2]
    hd = SWA_H * SWA_D
    kd = SWA_KVH * SWA_D
    cspec = pl.BlockSpec((1, 1, lc, kd), lambda bi, i: (bi, layer, 0, 0))
    tq = SWA_QB * BLK
    return pl.pallas_call(
        functools.partial(_swa_body, seq=s),
        grid=(b, s // tq),
        in_specs=[pl.BlockSpec((1, tq, hd), lambda bi, i: (bi, i, OFF_SQ // hd)),
                  pl.BlockSpec((1, s, kd), lambda bi, i: (bi, 0, OFF_SK // kd)),
                  pl.BlockSpec((1, s, kd), lambda bi, i: (bi, 0, OFF_SV // kd)),
                  cspec, cspec,
                  pl.BlockSpec((1, SWA_H), lambda bi, i: (0, 0)),
                  pl.BlockSpec((s, 128), lambda bi, i: (0, 0)),
                  pl.BlockSpec((s, 128), lambda bi, i: (0, 0)),
                  pl.BlockSpec((1, tq, hd), lambda bi, i: (bi, i, (OFF_GP + 3 * BR_W) // hd))],
        out_specs=pl.BlockSpec((1, tq, hd), lambda bi, i: (bi, i, 0)),
        out_shape=jax.ShapeDtypeStruct((b, s, hd), BF16),
        compiler_params=_cp(("parallel", "arbitrary"), 48),
        name="swa_attn",
    )(z, z, z, cache_k, cache_v, sink.reshape(1, SWA_H), cos2, sin2, z)


def _merge_body(h_ref, o0_ref, o1_ref, o2_ref, o3_ref, g0_ref, g1_ref, g2_ref, g3_ref, wb_ref, y_ref):
    h = h_ref[...]
    o_refs = (o0_ref, o1_ref, o2_ref, o3_ref)
    g_refs = (g0_ref, g1_ref, g2_ref, g3_ref)
    y = None
    for n in range(N_BRANCH):
        t = _sigmoid(_dot(h, g_refs[n][...])) * _dot(o_refs[n][...], wb_ref[n])
        y = t if y is None else y + t
    y_ref[...] = y.astype(BF16)


def _merge(h, ogs, w_gate, w_branch, layer, tm, tn):
    m, d = h.shape
    tm = min(tm, m)
    tn = min(tn, d)
    nj = d // tn
    gspec = lambda n: pl.BlockSpec((None, d, tn), lambda i, j: (layer, 0, n * nj + j))
    return pl.pallas_call(
        _merge_body,
        grid=(m // tm, nj),
        in_specs=[pl.BlockSpec((tm, d), lambda i, j: (i, 0))]
        + [pl.BlockSpec((tm, BR_W), lambda i, j: (i, 0))] * N_BRANCH
        + [gspec(n) for n in range(N_BRANCH)]
        + [pl.BlockSpec((None, N_BRANCH, BR_W, tn), lambda i, j: (layer, 0, 0, j))],
        out_specs=pl.BlockSpec((tm, tn), lambda i, j: (i, j)),
        out_shape=jax.ShapeDtypeStruct((m, d), BF16),
        compiler_params=_cp(("parallel", "arbitrary"), 56),
        name="merge",
    )(h, *ogs, w_gate, w_gate, w_gate, w_gate, w_branch)


def _out_body(*refs, has_next):
    if has_next:
        y_ref, w_ref, x_ref, gate_ref, g_ref, gn_ref, sh_ref, sc_ref, o_ref, h_ref = refs
    else:
        y_ref, w_ref, x_ref, gate_ref, g_ref, o_ref = refs
    t = _dot(y_ref[0], w_ref[...])
    t = t * lax.rsqrt(jnp.mean(t * t, axis=-1, keepdims=True) + EPS) * g_ref[...]
    x = x_ref[0] + gate_ref[0] * t
    o_ref[0] = x
    if has_next:
        y = x * lax.rsqrt(jnp.mean(x * x, axis=-1, keepdims=True) + EPS)
        y = y * gn_ref[...]
        h_ref[0] = (y * (1.0 + sc_ref[0]) + sh_ref[0]).astype(BF16)


def _out_proj(y, w_out, layer, x, gate, g, nxt):
    b, s, d = x.shape
    ts = min(s, 512)
    per_b = gate.shape[0] > 1
    cidx = (lambda bi, i: (bi, 0, 0)) if per_b else (lambda bi, i: (0, 0, 0))
    tile = pl.BlockSpec((1, ts, d), lambda bi, i: (bi, i, 0))
    vec = pl.BlockSpec((1, d), lambda bi, i: (0, 0))
    in_specs = [tile, pl.BlockSpec((None, d, d), lambda bi, i: (layer, 0, 0), pipeline_mode=pl.Buffered(1)),
                tile, pl.BlockSpec((1, 1, d), cidx), vec]
    args = [y, w_out, x, gate, g.reshape(1, d)]
    out_specs = [tile]
    out_shape = [jax.ShapeDtypeStruct((b, s, d), F32)]
    if nxt is not None:
        gn, shift, scale = nxt
        in_specs += [vec, pl.BlockSpec((1, 1, d), cidx), pl.BlockSpec((1, 1, d), cidx)]
        args += [gn.reshape(1, d), shift, scale]
        out_specs.append(tile)
        out_shape.append(jax.ShapeDtypeStruct((b, s, d), BF16))
    res = pl.pallas_call(
        functools.partial(_out_body, has_next=nxt is not None),
        grid=(b, s // ts),
        in_specs=in_specs,
        out_specs=out_specs,
        out_shape=out_shape,
        compiler_params=_cp(("parallel", "parallel"), 56),
        name="out_proj",
    )(*args)
    return (res[0], res[1]) if nxt is not None else (res[0], None)


def _relayout_w_in(w_in):
    c = {}
    o = 0
    for name, w in (("qa", MLA_Q_LORA), ("kva", MLA_KV_LORA), ("kr", MLA_ROPE),
                    ("rq", 512), ("rk", 512), ("rv", 512), ("nq", 512), ("nk", 512), ("nv", 512),
                    ("sq", 512), ("sk", 128), ("sv", 128), ("gp", N_BRANCH * BR_W)):
        c[name] = w_in[:, :, o:o + w]
        o += w
    depth, d, _ = w_in.shape
    order = ["qa", "rq", "rk", "rv", "nq", "nk", "nv", "sq", "gp", "kva", "sk", "sv", "kr"]
    parts = [c[n].astype(BF16) for n in order] + [jnp.zeros((depth, d, N_MIX - N_MIX_USED + MLA_ROPE), BF16)]
    return jnp.concatenate(parts, axis=-1), w_in[:, :, o:].astype(BF16)


def _mla_weights(w_q_up, w_kv_up):
    depth = w_q_up.shape[0]
    wq = w_q_up.reshape(depth, MLA_Q_LORA, MLA_H, MLA_NOPE + MLA_ROPE)
    wq = jnp.pad(wq, ((0, 0), (0, 0), (0, 0), (0, MLA_DQ - MLA_NOPE - MLA_ROPE)))
    wq = wq.reshape(depth, MLA_Q_LORA, MLA_H * MLA_DQ).astype(BF16)
    wkv = w_kv_up.reshape(depth, MLA_KV_LORA, MLA_H, MLA_NOPE + MLA_V)
    wk = jnp.pad(wkv[..., :MLA_NOPE], ((0, 0), (0, 0), (0, 0), (0, MLA_DQ - MLA_NOPE)))
    wk = wk.reshape(depth, MLA_KV_LORA, MLA_H * MLA_DQ).astype(BF16)
    wv = jnp.pad(wkv[..., MLA_NOPE:], ((0, 0), (0, 0), (0, 0), (0, MLA_DV - MLA_V)))
    wv = wv.reshape(depth, MLA_KV_LORA, MLA_H * MLA_DV).astype(BF16)
    j = jnp.arange(128)[:, None]
    cidx = jnp.arange(MLA_H * MLA_DQ)[None, :]
    p128 = ((cidx % MLA_DQ == MLA_NOPE + j) & (j < MLA_ROPE)).astype(BF16)
    vones = (jnp.arange(MLA_H * MLA_DV)[None, :] % MLA_DV == MLA_V).astype(F32)
    return wq, wk, wv, p128, vones


def _rope_tables(n_tok):
    pos = jnp.arange(n_tok)
    row = (pos // GRID_W).astype(F32)
    col = (pos % GRID_W).astype(F32)
    n_freq = MLA_ROPE // 4
    inv = ROPE_BASE ** (-jnp.arange(n_freq, dtype=F32) / n_freq)
    ang = jnp.concatenate([row[:, None] * inv[None], col[:, None] * inv[None]], axis=-1)
    cos, sin = jnp.cos(ang), jnp.sin(ang)
    c64 = jnp.concatenate([cos, cos], axis=-1)
    s64 = jnp.concatenate([-sin, sin], axis=-1)
    one, zero = jnp.ones_like(c64), jnp.zeros_like(c64)
    cq = jnp.concatenate([one, one, c64, one], axis=-1)
    sq = jnp.concatenate([zero, zero, s64, zero], axis=-1)
    ck = jnp.concatenate([c64, one], axis=-1)
    sk = jnp.concatenate([s64, zero], axis=-1)
    c2 = jnp.concatenate([c64, c64], axis=-1)
    s2 = jnp.concatenate([s64, s64], axis=-1)
    return (cq, sq, ck, sk), (c2, s2)


TM_IN = 2048
TN_IN = 768
TM_MERGE = 1024
TN_MERGE = 512


def kernel(x_prompt, x_sample, cache_mla_ckv, cache_mla_krope, state_ret, cache_nat_k, cache_nat_v,
           cache_swa_k, cache_swa_v, c, c_ctx, w_mod, b_mod, norm_pre, norm_post, w_in, mla_q_norm,
           mla_kv_norm, mla_w_q_up, mla_w_kv_up, ret_decay, ret_norm, nat_rpb, swa_sink, w_branch, w_out):
    depth = w_mod.shape[0]
    d = x_prompt.shape[-1]
    nb = x_sample.shape[0]
    s_lat = x_sample.shape[1]
    l_ctx = cache_mla_ckv.shape[2]

    w_mix, w_gate = _relayout_w_in(w_in)
    w_br = w_branch.astype(BF16)
    w_o = w_out.astype(BF16)
    wq, wk, wv, p128, vones = _mla_weights(mla_w_q_up, mla_w_kv_up)
    p64 = p128[:MLA_ROPE]
    mla_tabs, (cos2, sin2) = _rope_tables(s_lat)
    nat_bias = _nat_bias_tables(nat_rpb, s_lat // GRID_W)
    nat_k_c = cache_nat_k.reshape(nb, depth, l_ctx, NAT_H * NAT_D)
    nat_v_c = cache_nat_v.reshape(nb, depth, l_ctx, NAT_H * NAT_D)
    swa_k_c = cache_swa_k.reshape(nb, depth, l_ctx, SWA_KVH * SWA_D)
    swa_v_c = cache_swa_v.reshape(nb, depth, l_ctx, SWA_KVH * SWA_D)

    rows = 8
    conds = jnp.concatenate([c_ctx[None, :], c, jnp.zeros((rows - 1 - nb, d), F32)], axis=0)
    mod = _modulation(conds, w_mod, b_mod)

    def mod_parts(l, lo, hi):
        m = mod[l, lo:hi]
        return m[:, None, :d], m[:, None, d:2 * d], m[:, None, 2 * d:]

    def finish_layer(l, lo, hi, x, h, ogs):
        b, s, _ = x.shape
        gate = mod_parts(l, lo, hi)[2]
        y = _merge(h.reshape(b * s, d), [o.reshape(b * s, BR_W) for o in ogs], w_gate, w_br, l, TM_MERGE, TN_MERGE)
        nxt = None
        if l + 1 < depth:
            shift_n, scale_n, _ = mod_parts(l + 1, lo, hi)
            nxt = (norm_pre[l + 1], shift_n, scale_n)
        return _out_proj(y.reshape(b, s, d), w_o, l, x, gate, norm_post[l], nxt)

    def start_pass(x, lo, hi):
        shift, scale, _ = mod_parts(0, lo, hi)
        return _modnorm(x, norm_pre[0], shift, scale)

    def in_proj(x, h, l):
        b, s, _ = x.shape
        return _in_proj(h.reshape(b * s, d), w_mix, l, TM_IN, TN_IN).reshape(b, s, N_MIX)

    yp = x_prompt
    bc, sc, _ = yp.shape
    ckv_l, kr_l, st_l, nk_l, nv_l, sk_l, sv_l = [], [], [], [], [], [], []
    h = start_pass(yp, 0, 1)
    for l in range(depth):
        z = in_proj(yp, h, l)
        q, kf, vf, ckv = _mla_prep(z, mla_q_norm[l], mla_kv_norm[l], wq[l], wk[l], wv[l], p128, vones, None)
        og_mla, og_nat, og_swa = _ctx_attn(q, kf, vf, z, swa_sink[l])
        og_ret, st = _retention(z, ret_decay[l], ret_norm[l], None, 0, True)
        yp, h = finish_layer(l, 0, 1, yp, h, (og_mla, og_ret, og_nat, og_swa))
        ckv_l.append(ckv)
        kr_l.append(z[:, :, OFF_KR:OFF_KR + MLA_ROPE])
        st_l.append(st)
        nk_l.append(z[:, :, OFF_NK:OFF_NK + 512].reshape(bc, sc, NAT_H, NAT_D))
        nv_l.append(z[:, :, OFF_NV:OFF_NV + 512].reshape(bc, sc, NAT_H, NAT_D))
        sk_l.append(z[:, :, OFF_SK:OFF_SK + 128].reshape(bc, sc, SWA_KVH, SWA_D))
        sv_l.append(z[:, :, OFF_SV:OFF_SV + 128].reshape(bc, sc, SWA_KVH, SWA_D))

    ys = x_sample
    h = start_pass(ys, 1, 1 + nb)
    for l in range(depth):
        z = in_proj(ys, h, l)
        q, kf, vf, _ = _mla_prep(z, mla_q_norm[l], mla_kv_norm[l], wq[l], wk[l], wv[l], p128, vones, mla_tabs)
        kc, vc = _mla_expand(cache_mla_ckv, cache_mla_krope, l, wk[l], wv[l], p64, vones)
        og_mla = _mla_attn(q, kf, vf, z, OFF_GP // BR_W, ctx=(kc, vc))
        og_ret, _ = _retention(z, ret_decay[l], ret_norm[l], state_ret, l, False)
        og_nat = _nat(z, nat_k_c, nat_v_c, l, nat_bias)
        og_swa = _swa(z, swa_k_c, swa_v_c, l, swa_sink[l], cos2, sin2)
        ys, h = finish_layer(l, 1, 1 + nb, ys, h, (og_mla, og_ret, og_nat, og_swa))

    return (yp, ys, jnp.stack(ckv_l, axis=1), jnp.stack(kr_l, axis=1), jnp.stack(st_l, axis=1),
            jnp.stack(nk_l, axis=1), jnp.stack(nv_l, axis=1), jnp.stack(sk_l, axis=1), jnp.stack(sv_l, axis=1))
```

```python
import functools

import jax
import jax.numpy as jnp
from jax import lax
from jax.experimental import pallas as pl
from jax.experimental.pallas import tpu as pltpu

F32 = jnp.float32
BF16 = jnp.bfloat16

GRID_W = 64
BLK = 128
ROPE_BASE = 10000.0
EPS = 1e-6
NEG_INF = -1e30
N_BRANCH = 4
BR_W = 512

MLA_H = 4
MLA_Q_LORA = 512
MLA_KV_LORA = 256
MLA_NOPE = 128
MLA_ROPE = 64
MLA_V = 128
MLA_DQ = 256
MLA_DV = 256
MLA_QSCALE = (MLA_NOPE + MLA_ROPE) ** -0.5 * 1.4426950408889634

RET_H = 4
RET_DK = 128
RET_DV = 128
RET_HPS = 2

NAT_H = 4
NAT_D = 128
NAT_ROWS = 8
NAT_COLS = 16
NAT_QR = 4
NAT_WR = NAT_ROWS + NAT_QR

SWA_H = 8
SWA_KVH = 2
SWA_D = 64
SWA_WINDOW = 128

OFF_QA = 0
OFF_RQ = 512
OFF_RK = 1024
OFF_RV = 1536
OFF_NQ = 2048
OFF_NK = 2560
OFF_NV = 3072
OFF_SQ = 3584
OFF_GP = 4096
OFF_KVA = 6144
OFF_SK = 6400
OFF_SV = 6528
OFF_KR = 6656
N_MIX_USED = 6784
N_MIX = 6912

V7X_VMEM_BYTES = 64 * 1024 * 1024


def _cp(sem, vmem_mb):
    assert vmem_mb * 1024 * 1024 < V7X_VMEM_BYTES
    return pltpu.CompilerParams(dimension_semantics=sem, vmem_limit_bytes=vmem_mb * 1024 * 1024)


def _sigmoid(x):
    return 1.0 / (1.0 + jnp.exp(-x))


def _silu(x):
    return x * _sigmoid(x)


def _swap_halves(x):
    n = x.shape[-1]
    ax = x.ndim - 1
    lane = lax.broadcasted_iota(jnp.int32, x.shape, ax)
    return jnp.where((lane & 32) == 0, pltpu.roll(x, n - 32, ax), pltpu.roll(x, 32, ax))


def _dot_nt(a, b):
    return lax.dot_general(a, b, (((1,), (1,)), ((), ())), preferred_element_type=F32)


def _dot(a, b):
    return jnp.dot(a, b, preferred_element_type=F32)


def _mod_body(c_ref, w_ref, b_ref, o_ref):
    a = _silu(c_ref[...]).astype(BF16)
    o_ref[0] = _dot(a, w_ref[0].astype(BF16)) + b_ref[0]


def _modulation(conds, w_mod, b_mod):
    depth, d, n = w_mod.shape
    rows = conds.shape[0]
    tn = 1024 if n % 1024 == 0 else n
    return pl.pallas_call(
        _mod_body,
        grid=(depth, n // tn),
        in_specs=[pl.BlockSpec((rows, d), lambda l, j: (0, 0)),
                  pl.BlockSpec((1, d, tn), lambda l, j: (l, 0, j)),
                  pl.BlockSpec((1, 1, tn), lambda l, j: (l, 0, j))],
        out_specs=pl.BlockSpec((1, rows, tn), lambda l, j: (l, 0, j)),
        out_shape=jax.ShapeDtypeStruct((depth, rows, n), F32),
        compiler_params=_cp(("parallel", "parallel"), 40),
        name="modulation",
    )(conds, w_mod, b_mod.reshape(depth, 1, n))


def _modnorm_body(x_ref, g_ref, sh_ref, sc_ref, h_ref):
    x = x_ref[0]
    y = x * lax.rsqrt(jnp.mean(x * x, axis=-1, keepdims=True) + EPS)
    y = y * g_ref[...]
    h_ref[0] = (y * (1.0 + sc_ref[0]) + sh_ref[0]).astype(BF16)


def _modnorm(x, g, shift, scale):
    b, s, d = x.shape
    ts = min(s, 512)
    per_b = shift.shape[0] > 1
    cidx = (lambda bi, i: (bi, 0, 0)) if per_b else (lambda bi, i: (0, 0, 0))
    return pl.pallas_call(
        _modnorm_body,
        grid=(b, s // ts),
        in_specs=[pl.BlockSpec((1, ts, d), lambda bi, i: (bi, i, 0)),
                  pl.BlockSpec((1, d), lambda bi, i: (0, 0)),
                  pl.BlockSpec((1, 1, d), cidx),
                  pl.BlockSpec((1, 1, d), cidx)],
        out_specs=pl.BlockSpec((1, ts, d), lambda bi, i: (bi, i, 0)),
        out_shape=jax.ShapeDtypeStruct((b, s, d), BF16),
        compiler_params=_cp(("parallel", "parallel"), 32),
        name="modnorm",
    )(x, g.reshape(1, d), shift, scale)


def _mm_body(a_ref, b_ref, o_ref):
    o_ref[...] = _dot(a_ref[...], b_ref[...])


def _in_proj(a, w_mix, layer, tm, tn):
    m, k = a.shape
    tm = min(tm, m)
    return pl.pallas_call(
        _mm_body,
        grid=(m // tm, N_MIX // tn),
        in_specs=[pl.BlockSpec((tm, k), lambda i, j: (i, 0)),
                  pl.BlockSpec((None, k, tn), lambda i, j: (layer, 0, j))],
        out_specs=pl.BlockSpec((tm, tn), lambda i, j: (i, j)),
        out_shape=jax.ShapeDtypeStruct((m, N_MIX), F32),
        compiler_params=_cp(("parallel", "arbitrary"), 48),
        name="in_proj",
    )(a, w_mix)


def _mla_prep_body(*refs, rope):
    if rope:
        (qa_ref, kva_ref, kr_ref, gq_ref, gkv_ref, wq_ref, wk_ref, wv_ref, p_ref, one_ref,
         cq_ref, sq_ref, ck_ref, sk_ref, q_out, k_out, v_out, ckv_out) = refs
    else:
        (qa_ref, kva_ref, kr_ref, gq_ref, gkv_ref, wq_ref, wk_ref, wv_ref, p_ref, one_ref,
         q_out, k_out, v_out, ckv_out) = refs
    qa = qa_ref[0]
    cq = qa * lax.rsqrt(jnp.mean(qa * qa, axis=-1, keepdims=True) + EPS) * gq_ref[...]
    q = _dot(cq.astype(BF16), wq_ref[...])
    kva = kva_ref[0]
    ckv = kva * lax.rsqrt(jnp.mean(kva * kva, axis=-1, keepdims=True) + EPS) * gkv_ref[...]
    ckv_out[0] = ckv
    kr = kr_ref[0]
    if rope:
        cq_t = jnp.concatenate([cq_ref[...]] * MLA_H, axis=-1)
        sq_t = jnp.concatenate([sq_ref[...]] * MLA_H, axis=-1)
        q = q * cq_t + _swap_halves(q) * sq_t
        kr = kr * ck_ref[...] + _swap_halves(kr) * sk_ref[...]
    q_out[0] = (q * MLA_QSCALE).astype(BF16)
    cb = ckv.astype(BF16)
    k_out[0] = (_dot(cb, wk_ref[...]) + _dot(kr.astype(BF16), p_ref[...])).astype(BF16)
    v_out[0] = (_dot(cb, wv_ref[...]) + one_ref[...]).astype(BF16)


def _mla_prep(z, gq, gkv, wq, wk, wv, p128, vones, tabs):
    b, s, _ = z.shape
    tm = min(s, 512)
    rope = tabs is not None
    full = lambda shape: pl.BlockSpec(shape, lambda bi, i: (0,) * len(shape))
    in_specs = [pl.BlockSpec((1, tm, MLA_Q_LORA), lambda bi, i: (bi, i, OFF_QA // MLA_Q_LORA)),
                pl.BlockSpec((1, tm, MLA_KV_LORA), lambda bi, i: (bi, i, OFF_KVA // MLA_KV_LORA)),
                pl.BlockSpec((1, tm, 128), lambda bi, i: (bi, i, OFF_KR // 128)),
                full((1, MLA_Q_LORA)), full((1, MLA_KV_LORA)),
                full(wq.shape), full(wk.shape), full(wv.shape), full(p128.shape), full(vones.shape)]
    args = [z, z, z, gq.reshape(1, -1), gkv.reshape(1, -1), wq, wk, wv, p128, vones]
    if rope:
        cq, sq, ck, sk = tabs
        in_specs += [pl.BlockSpec((tm, MLA_DQ), lambda bi, i: (i, 0))] * 2
        in_specs += [pl.BlockSpec((tm, 128), lambda bi, i: (i, 0))] * 2
        args += [cq, sq, ck, sk]
    hq = MLA_H * MLA_DQ
    hv = MLA_H * MLA_DV
    return pl.pallas_call(
        functools.partial(_mla_prep_body, rope=rope),
        grid=(b, s // tm),
        in_specs=in_specs,
        out_specs=[pl.BlockSpec((1, tm, hq), lambda bi, i: (bi, i, 0)),
                   pl.BlockSpec((1, tm, hq), lambda bi, i: (bi, i, 0)),
                   pl.BlockSpec((1, tm, hv), lambda bi, i: (bi, i, 0)),
                   pl.BlockSpec((1, tm, MLA_KV_LORA), lambda bi, i: (bi, i, 0))],
        out_shape=[jax.ShapeDtypeStruct((b, s, hq), BF16),
                   jax.ShapeDtypeStruct((b, s, hq), BF16),
                   jax.ShapeDtypeStruct((b, s, hv), BF16),
                   jax.ShapeDtypeStruct((b, s, MLA_KV_LORA), F32)],
        compiler_params=_cp(("parallel", "parallel"), 40),
        name="mla_prep",
    )(*args)


def _mla_expand_body(ckv_ref, kr_ref, wk_ref, wv_ref, p_ref, one_ref, k_out, v_out):
    cb = ckv_ref[0, 0].astype(BF16)
    k_out[0] = (_dot(cb, wk_ref[...]) + _dot(kr_ref[0, 0].astype(BF16), p_ref[...])).astype(BF16)
    v_out[0] = (_dot(cb, wv_ref[...]) + one_ref[...]).astype(BF16)


def _mla_expand(cache_ckv, cache_kr, layer, wk, wv, p64, vones):
    b, _, s, _ = cache_ckv.shape
    full = lambda shape: pl.BlockSpec(shape, lambda bi: (0,) * len(shape))
    hq = MLA_H * MLA_DQ
    hv = MLA_H * MLA_DV
    return pl.pallas_call(
        _mla_expand_body,
        grid=(b,),
        in_specs=[pl.BlockSpec((1, 1, s, MLA_KV_LORA), lambda bi: (bi, layer, 0, 0)),
                  pl.BlockSpec((1, 1, s, MLA_ROPE), lambda bi: (bi, layer, 0, 0)),
                  full(wk.shape), full(wv.shape), full(p64.shape), full(vones.shape)],
        out_specs=[pl.BlockSpec((1, s, hq), lambda bi: (bi, 0, 0)),
                   pl.BlockSpec((1, s, hv), lambda bi: (bi, 0, 0))],
        out_shape=[jax.ShapeDtypeStruct((b, s, hq), BF16),
                   jax.ShapeDtypeStruct((b, s, hv), BF16)],
        compiler_params=_cp(("parallel",), 32),
        name="mla_expand",
    )(cache_ckv, cache_kr, wk, wv, p64, vones)


def _mla_attn_body(*refs, has_ctx, nkv):
    it = iter(refs)
    q_ref, k_ref, v_ref = next(it), next(it), next(it)
    kc_ref = vc_ref = None
    if has_ctx:
        kc_ref, vc_ref = next(it), next(it)
    gp_ref, o_ref, m_scr, acc_scr = next(it), next(it), next(it), next(it)
    j = pl.program_id(2)

    def process(kr, vr):
        tk = kr.shape[1]
        for h in range(MLA_H):
            qk = slice(h * MLA_DQ, (h + 1) * MLA_DQ)
            s = _dot_nt(q_ref[0, :, qk], kr[0, :, qk])
            m_prev = m_scr[h]
            m_new = jnp.maximum(m_prev, jnp.max(s, axis=-1, keepdims=True))
            alpha = jnp.exp2(m_prev - m_new)
            p = jnp.exp2(s - jnp.concatenate([m_new] * (tk // 128), axis=-1))
            pv = _dot(p.astype(BF16), vr[0, :, h * MLA_DV:(h + 1) * MLA_DV])
            acc_scr[h] = jnp.concatenate([alpha] * (MLA_DV // 128), axis=-1) * acc_scr[h] + pv
            m_scr[h] = m_new

    @pl.when(j == 0)
    def _():
        m_scr[...] = jnp.full(m_scr.shape, NEG_INF, F32)
        acc_scr[...] = jnp.zeros(acc_scr.shape, F32)
        if has_ctx:
            process(kc_ref, vc_ref)

    process(k_ref, v_ref)

    @pl.when(j == nkv - 1)
    def _():
        outs = []
        for h in range(MLA_H):
            a = acc_scr[h]
            outs.append(a[:, :MLA_V] / a[:, MLA_V:MLA_V + 1])
        o_ref[0] = (jnp.concatenate(outs, axis=-1) * _silu(gp_ref[0])).astype(BF16)


def _mla_attn(q, k, v, gp, gpcol, ctx=None, tq=1024, tk=1024):
    b, sq, _ = q.shape
    sk = k.shape[1]
    tq = min(tq, sq)
    tk = min(tk, sk)
    nkv = sk // tk
    hq = MLA_H * MLA_DQ
    hv = MLA_H * MLA_DV
    in_specs = [pl.BlockSpec((1, tq, hq), lambda bi, i, j: (bi, i, 0)),
                pl.BlockSpec((1, tk, hq), lambda bi, i, j: (bi, j, 0)),
                pl.BlockSpec((1, tk, hv), lambda bi, i, j: (bi, j, 0))]
    args = [q, k, v]
    if ctx is not None:
        kc, vc = ctx
        in_specs += [pl.BlockSpec((1,) + kc.shape[1:], lambda bi, i, j: (bi, 0, 0)),
                     pl.BlockSpec((1,) + vc.shape[1:], lambda bi, i, j: (bi, 0, 0))]
        args += [kc, vc]
    in_specs += [pl.BlockSpec((1, tq, MLA_H * MLA_V), lambda bi, i, j: (bi, i, gpcol))]
    args += [gp]
    return pl.pallas_call(
        functools.partial(_mla_attn_body, has_ctx=ctx is not None, nkv=nkv),
        grid=(b, sq // tq, nkv),
        in_specs=in_specs,
        out_specs=pl.BlockSpec((1, tq, MLA_H * MLA_V), lambda bi, i, j: (bi, i, 0)),
        out_shape=jax.ShapeDtypeStruct((b, sq, MLA_H * MLA_V), BF16),
        scratch_shapes=[pltpu.VMEM((MLA_H, tq, 128), F32), pltpu.VMEM((MLA_H, tq, MLA_DV), F32)],
        compiler_params=_cp(("parallel", "parallel", "arbitrary"), 48),
        name="mla_attn",
    )(*args)


def _ctx_attn_body(qm_ref, km_ref, vm_ref, nq_ref, nk_ref, nv_ref, sq_ref, sk_ref, sv_ref, sink_ref,
                   gpm_ref, gpn_ref, gps_ref, om_ref, on_ref, os_ref):
    outs = []
    for h in range(MLA_H):
        qk = slice(h * MLA_DQ, (h + 1) * MLA_DQ)
        s = _dot_nt(qm_ref[0, :, qk], km_ref[0, :, qk])
        p = jnp.exp2(s - jnp.max(s, axis=-1, keepdims=True))
        pv = _dot(p.astype(BF16), vm_ref[0, :, h * MLA_DV:(h + 1) * MLA_DV])
        outs.append(pv[:, :MLA_V] / pv[:, MLA_V:MLA_V + 1])
    om_ref[0] = (jnp.concatenate(outs, axis=-1) * _silu(gpm_ref[0])).astype(BF16)

    outs = []
    for h in range(NAT_H):
        sl = slice(h * NAT_D, (h + 1) * NAT_D)
        s = _dot_nt(nq_ref[0, :, sl].astype(BF16), nk_ref[0, :, sl].astype(BF16)) * NAT_D ** -0.5
        p = jnp.exp(s - jnp.max(s, axis=-1, keepdims=True))
        den = jnp.sum(p, axis=-1, keepdims=True)
        outs.append(_dot(p.astype(BF16), nv_ref[0, :, sl].astype(BF16)) / den)
    on_ref[0] = (jnp.concatenate(outs, axis=-1) * _silu(gpn_ref[0])).astype(BF16)

    n_tok = sq_ref.shape[1]
    group = SWA_H // SWA_KVH
    q = sq_ref[0] * SWA_D ** -0.5
    outs = []
    for g in range(SWA_KVH):
        heads = range(g * group, (g + 1) * group)
        qg = jnp.concatenate([q[:, h * SWA_D:(h + 1) * SWA_D] for h in heads], axis=0).astype(BF16)
        sink = jnp.concatenate([jnp.broadcast_to(sink_ref[0:1, h:h + 1], (n_tok, 1)) for h in heads], axis=0)
        ksl = slice(g * SWA_D, (g + 1) * SWA_D)
        s = _dot_nt(qg, sk_ref[0, :, ksl].astype(BF16))
        m = jnp.maximum(jnp.max(s, axis=-1, keepdims=True), sink)
        p = jnp.exp(s - m)
        den = jnp.sum(p, axis=-1, keepdims=True) + jnp.exp(sink - m)
        o = _dot(p.astype(BF16), sv_ref[0, :, ksl].astype(BF16)) / den
        outs += [o[t * n_tok:(t + 1) * n_tok] for t in range(group)]
    os_ref[0] = (jnp.concatenate(outs, axis=-1) * _silu(gps_ref[0])).astype(BF16)


def _ctx_attn(q, k, v, z, sink):
    b, n_tok, _ = z.shape
    blk = lambda w, off: pl.BlockSpec((1, n_tok, w), lambda bi: (bi, 0, off // w))
    kd = SWA_KVH * SWA_D
    out = jax.ShapeDtypeStruct((b, n_tok, BR_W), BF16)
    return pl.pallas_call(
        _ctx_attn_body,
        grid=(b,),
        in_specs=[blk(MLA_H * MLA_DQ, 0), blk(MLA_H * MLA_DQ, 0), blk(MLA_H * MLA_DV, 0),
                  blk(BR_W, OFF_NQ), blk(BR_W, OFF_NK), blk(BR_W, OFF_NV),
                  blk(BR_W, OFF_SQ), blk(kd, OFF_SK), blk(kd, OFF_SV),
                  pl.BlockSpec((1, SWA_H), lambda bi: (0, 0)),
                  blk(BR_W, OFF_GP), blk(BR_W, OFF_GP + 2 * BR_W), blk(BR_W, OFF_GP + 3 * BR_W)],
        out_specs=[blk(BR_W, 0)] * 3,
        out_shape=[out, out, out],
        compiler_params=_cp(("parallel",), 48),
        name="ctx_attn",
    )(q, k, v, z, z, z, z, z, z, sink.reshape(1, SWA_H), z, z, z)


def _ret_body(*refs, nc, has_s0, has_st):
    it = iter(refs)
    dec_ref, norm_ref, q_ref, k_ref, v_ref = next(it), next(it), next(it), next(it), next(it)
    s0_ref = next(it) if has_s0 else None
    gp_ref, o_ref = next(it), next(it)
    st_ref = next(it) if has_st else None
    of_scr, ob_scr = next(it), next(it)
    hgrp = pl.program_id(1)

    d = dec_ref[0]
    lg_all = jnp.minimum(d, 0.0) - jnp.log1p(jnp.exp(-jnp.abs(d)))
    hcol = lax.broadcasted_iota(jnp.int32, d.shape, 1)
    row = lax.broadcasted_iota(jnp.int32, (BLK, BLK), 0).astype(F32)
    col = lax.broadcasted_iota(jnp.int32, (BLK, BLK), 1).astype(F32)
    a = lax.broadcasted_iota(jnp.int32, (BLK, 1), 0).astype(F32)
    kscale = RET_DK ** -0.5

    heads = []
    for hh in range(RET_HPS):
        lg = jnp.sum(jnp.where(hcol == hgrp * RET_HPS + hh, lg_all, 0.0), axis=1, keepdims=True)
        lg_f, lg_b = lg[0:1, :], lg[1:2, :]
        fwd = (jnp.where(row - col >= 0, jnp.exp(lg_f * jnp.maximum(row - col, 0.0)), 0.0),
               jnp.exp(lg_f * (a + 1.0)), jnp.exp(lg_f * (BLK - 1.0 - a)), jnp.exp(lg_f * BLK))
        bwd = (jnp.where(col - row > 0, jnp.exp(lg_b * jnp.maximum(col - row, 0.0)), 0.0),
               jnp.exp(lg_b * (BLK - a)), jnp.exp(lg_b * a), jnp.exp(lg_b * BLK))
        heads.append((slice(hh * RET_DK, (hh + 1) * RET_DK), fwd, bwd))

    def chunk(off, sl, st, tabs):
        dm, qd, kd, cd = tabs
        q = q_ref[0, pl.ds(off, BLK), sl]
        k = k_ref[0, pl.ds(off, BLK), sl] * kscale
        v = v_ref[0, pl.ds(off, BLK), sl].astype(BF16)
        inner = _dot_nt(q.astype(BF16), k.astype(BF16)) * dm
        o = _dot(inner.astype(BF16), v) + _dot((q * qd).astype(BF16), st.astype(BF16))
        st = st * cd + _dot(jnp.transpose(k * kd).astype(BF16), v)
        return o, st

    def step(j, carry):
        off_f = pl.multiple_of(j * BLK, BLK)
        off_b = pl.multiple_of((nc - 1 - j) * BLK, BLK)
        new = []
        for (sl, fwd, bwd), (sf, sb) in zip(heads, carry):
            o, sf = chunk(off_f, sl, sf, fwd)
            of_scr[pl.ds(off_f, BLK), sl] = o
            o, sb = chunk(off_b, sl, sb, bwd)
            ob_scr[pl.ds(off_b, BLK), sl] = o
            new.append((sf, sb))
        return tuple(new)

    if has_s0:
        init = tuple((s0_ref[0, 0, 0, hh], s0_ref[0, 0, 1, hh]) for hh in range(RET_HPS))
    else:
        zero = jnp.zeros((RET_DK, RET_DV), F32)
        init = tuple((zero, zero) for _ in range(RET_HPS))
    fin = lax.fori_loop(0, nc, step, init, unroll=2)
    if has_st:
        for hh in range(RET_HPS):
            st_ref[0, 0, hh] = fin[hh][0]
            st_ref[0, 1, hh] = fin[hh][1]
    for sl, _, _ in heads:
        o = of_scr[:, sl] + ob_scr[:, sl]
        o = o * lax.rsqrt(jnp.mean(o * o, axis=-1, keepdims=True) + EPS) * norm_ref[:, sl]
        o_ref[0, :, sl] = (o * _silu(gp_ref[0, :, sl])).astype(BF16)


def _retention(z, decay, norm, s0, layer, want_state):
    b, s, _ = z.shape
    nc = s // BLK
    w = RET_HPS * RET_DK
    colspec = lambda off: pl.BlockSpec((1, s, w), lambda bi, g: (bi, 0, off // w + g))
    in_specs = [pl.BlockSpec((1, 2, RET_H), lambda bi, g: (0, 0, 0)),
                pl.BlockSpec((1, w), lambda bi, g: (0, g)),
                colspec(OFF_RQ), colspec(OFF_RK), colspec(OFF_RV)]
    args = [decay.reshape(1, 2, RET_H), norm.reshape(1, -1), z, z, z]
    if s0 is not None:
        in_specs.append(pl.BlockSpec((1, 1, 2, RET_HPS, RET_DK, RET_DV), lambda bi, g: (bi, layer, 0, g, 0, 0)))
        args.append(s0)
    in_specs.append(colspec(OFF_GP + BR_W))
    args.append(z)
    out_specs = [pl.BlockSpec((1, s, w), lambda bi, g: (bi, 0, g))]
    out_shape = [jax.ShapeDtypeStruct((b, s, RET_H * RET_DV), BF16)]
    if want_state:
        out_specs.append(pl.BlockSpec((1, 2, RET_HPS, RET_DK, RET_DV), lambda bi, g: (bi, 0, g, 0, 0)))
        out_shape.append(jax.ShapeDtypeStruct((b, 2, RET_H, RET_DK, RET_DV), F32))
    res = pl.pallas_call(
        functools.partial(_ret_body, nc=nc, has_s0=s0 is not None, has_st=want_state),
        grid=(b, RET_H // RET_HPS),
        in_specs=in_specs,
        out_specs=out_specs,
        out_shape=out_shape,
        scratch_shapes=[pltpu.VMEM((s, w), F32), pltpu.VMEM((s, w), F32)],
        compiler_params=_cp(("parallel", "parallel"), 56),
        name="retention",
    )(*args)
    return (res[0], res[1]) if want_state else (res[0], None)


def _nat_body(q_ref, k_ref, v_ref, kc_ref, vc_ref, bias_ref, gp_ref, o_ref, kb_scr, vb_scr, o_scr, *, rows):
    scale = NAT_D ** -0.5
    kb_scr[...] = k_ref[0].astype(BF16)
    vb_scr[...] = v_ref[0].astype(BF16)
    kc = kc_ref[0, 0].astype(BF16)
    vc = vc_ref[0, 0].astype(BF16)
    nq = NAT_QR * GRID_W
    win = NAT_WR * GRID_W
    nblk = rows // NAT_QR

    def step(t, _):
        r0 = t * NAT_QR
        ws = jnp.clip(r0 - NAT_ROWS // 2, 0, rows - NAT_WR)
        pat = jnp.where(t == 0, 0, jnp.where(t == nblk - 1, 2, 1))
        qoff = pl.multiple_of(r0 * GRID_W, nq)
        koff = pl.multiple_of(ws * GRID_W, GRID_W)
        q = q_ref[0, pl.ds(qoff, nq), :].astype(BF16)
        kw = kb_scr[pl.ds(koff, win), :]
        vw = vb_scr[pl.ds(koff, win), :]
        s_loc = _dot_nt(q, kw) * scale + bias_ref[0, 0, pat]
        s_ctx = _dot_nt(q, kc) * scale
        m = jnp.maximum(jnp.max(s_loc, axis=-1, keepdims=True), jnp.max(s_ctx, axis=-1, keepdims=True))
        p_loc = jnp.exp(s_loc - m)
        p_ctx = jnp.exp(s_ctx - m)
        den = jnp.sum(p_loc, axis=-1, keepdims=True) + jnp.sum(p_ctx, axis=-1, keepdims=True)
        o = _dot(p_loc.astype(BF16), vw) + _dot(p_ctx.astype(BF16), vc)
        o_scr[pl.ds(qoff, nq), :] = o / den
        return 0

    lax.fori_loop(0, nblk, step, 0, unroll=2)
    o_ref[0] = (o_scr[...] * _silu(gp_ref[0])).astype(BF16)


def _nat_bias_tables(rpb, rows):
    depth, n_h = rpb.shape[:2]
    c = jnp.arange(GRID_W)[:, None]
    kc = jnp.arange(GRID_W)[None, :]
    cs = jnp.clip(c - NAT_COLS // 2, 0, GRID_W - NAT_COLS)
    valid = (kc >= cs) & (kc < cs + NAT_COLS)
    onehot = ((kc - c + (NAT_COLS - 1))[:, :, None] == jnp.arange(2 * NAT_COLS - 1)[None, None, :]).astype(F32)
    toep = jnp.einsum("dhrj,ckj->dhrck", rpb.astype(F32), onehot, precision=lax.Precision.HIGHEST)
    toep = jnp.where(valid, toep, NEG_INF)
    neg = jnp.full((depth, n_h, GRID_W, GRID_W), NEG_INF, F32)
    pats = []
    for r0, ws in ((0, 0), (NAT_QR, 0), (rows - NAT_QR, rows - NAT_WR)):
        qrows = []
        for qi in range(NAT_QR):
            r = r0 + qi
            rs = min(max(r - NAT_ROWS // 2, 0), rows - NAT_ROWS)
            blocks = []
            for wm in range(NAT_WR):
                kr = ws + wm
                blocks.append(toep[:, :, kr - r + NAT_ROWS - 1] if rs <= kr < rs + NAT_ROWS else neg)
            qrows.append(jnp.concatenate(blocks, axis=-1))
        pats.append(jnp.concatenate(qrows, axis=-2))
    return jnp.stack(pats, axis=2)


def _nat(z, cache_k, cache_v, layer, bias):
    b, s, _ = z.shape
    rows = s // GRID_W
    assert rows % NAT_QR == 0 and rows >= 2 * NAT_WR - NAT_ROWS
    lc = cache_k.shape[2]
    colspec = lambda blk: pl.BlockSpec((1, s, NAT_D), lambda bi, h: (bi, 0, blk + h))
    cspec = pl.BlockSpec((1, 1, lc, NAT_D), lambda bi, h: (bi, layer, 0, h))
    return pl.pallas_call(
        functools.partial(_nat_body, rows=rows),
        grid=(b, NAT_H),
        in_specs=[colspec(OFF_NQ // 128), colspec(OFF_NK // 128), colspec(OFF_NV // 128), cspec, cspec,
                  pl.BlockSpec((1, 1) + bias.shape[2:], lambda bi, h: (layer, h, 0, 0, 0)),
                  colspec((OFF_GP + 2 * BR_W) // 128)],
        out_specs=pl.BlockSpec((1, s, NAT_D), lambda bi, h: (bi, 0, h)),
        out_shape=jax.ShapeDtypeStruct((b, s, NAT_H * NAT_D), BF16),
        scratch_shapes=[pltpu.VMEM((s, NAT_D), BF16), pltpu.VMEM((s, NAT_D), BF16),
                        pltpu.VMEM((s, NAT_D), F32)],
        compiler_params=_cp(("parallel", "parallel"), 48),
        name="nat_attn",
    )(z, z, z, cache_k, cache_v, bias, z)


def _swa_values_with_ones(v):
    lane = lax.broadcasted_iota(jnp.int32, v.shape, 1)
    halves = [jnp.where(lane < SWA_D, v if g == 0 else pltpu.roll(v, (SWA_KVH - g) * SWA_D, 1), 1.0)
              for g in range(SWA_KVH)]
    return jnp.concatenate(halves, axis=-1).astype(BF16)


def _swa_body(q_ref, k_ref, v_ref, kc_ref, vc_ref, sink_ref, cos_ref, sin_ref, gp_ref, o_ref,
              kr_scr, vl_scr, kc_scr, vc_scr, *, seq):
    scale = SWA_D ** -0.5
    group = SWA_H // SWA_KVH
    win = 3 * BLK
    i = pl.program_id(1)

    @pl.when(i == 0)
    def _():
        k = k_ref[0]
        kr_scr[...] = (k * cos_ref[...] + _swap_halves(k) * sin_ref[...]).astype(BF16)
        vl_scr[...] = _swa_values_with_ones(v_ref[0])
        kc_scr[...] = kc_ref[0, 0].astype(BF16)
        vc_scr[...] = _swa_values_with_ones(vc_ref[0, 0])

    start = pl.multiple_of(jnp.clip((i - 1) * BLK, 0, seq - win), BLK)
    qoff = pl.multiple_of(i * BLK, BLK)
    cq = cos_ref[pl.ds(qoff, BLK), :]
    sq = sin_ref[pl.ds(qoff, BLK), :]
    q = q_ref[0]
    q = (q * jnp.concatenate([cq] * (SWA_H // 2), axis=-1)
         + _swap_halves(q) * jnp.concatenate([sq] * (SWA_H // 2), axis=-1))
    q = q * scale
    kw = kr_scr[pl.ds(start, win), :]
    vw = vl_scr[pl.ds(start, win), :]

    qpos = qoff + lax.broadcasted_iota(jnp.int32, (BLK, win), 0)
    kpos = start + lax.broadcasted_iota(jnp.int32, (BLK, win), 1)
    valid = jnp.abs(qpos - kpos) <= SWA_WINDOW
    valid = jnp.concatenate([valid] * group, axis=0)

    outs = []
    for g in range(SWA_KVH):
        heads = range(g * group, (g + 1) * group)
        qg = jnp.concatenate([q[:, h * SWA_D:(h + 1) * SWA_D] for h in heads], axis=0).astype(BF16)
        sink = jnp.concatenate([jnp.broadcast_to(sink_ref[0:1, h:h + 1], (BLK, 1)) for h in heads], axis=0)
        ksl = slice(g * SWA_D, (g + 1) * SWA_D)
        vsl = slice(g * 2 * SWA_D, (g + 1) * 2 * SWA_D)
        s = jnp.concatenate([jnp.where(valid, _dot_nt(qg, kw[:, ksl]), NEG_INF),
                             _dot_nt(qg, kc_scr[:, ksl])], axis=-1)
        m = jnp.maximum(jnp.max(s, axis=-1, keepdims=True), sink)
        p = jnp.exp(s - m).astype(BF16)
        pv = _dot(p[:, :win], vw[:, vsl]) + _dot(p[:, win:], vc_scr[:, vsl])
        o = pv[:, :SWA_D] / (pv[:, SWA_D:SWA_D + 1] + jnp.exp(sink - m))
        outs += [o[t * BLK:(t + 1) * BLK] for t in range(group)]
    o = jnp.concatenate(outs, axis=-1)
    o_ref[0] = (o * _silu(gp_ref[0])).astype(BF16)


def _swa(z, cache_k, cache_v, layer, sink, cos2, sin2):
    b, s, _ = z.shape
    lc = cache_k.shape[2]
    hd = SWA_H * SWA_D
    kd = SWA_KVH * SWA_D
    cspec = pl.BlockSpec((1, 1, lc, kd), lambda bi, i: (bi, layer, 0, 0))
    tq = BLK
    return pl.pallas_call(
        functools.partial(_swa_body, seq=s),
        grid=(b, s // tq),
        in_specs=[pl.BlockSpec((1, tq, hd), lambda bi, i: (bi, i, OFF_SQ // hd)),
                  pl.BlockSpec((1, s, kd), lambda bi, i: (bi, 0, OFF_SK // kd)),
                  pl.BlockSpec((1, s, kd), lambda bi, i: (bi, 0, OFF_SV // kd)),
                  cspec, cspec,
                  pl.BlockSpec((1, SWA_H), lambda bi, i: (0, 0)),
                  pl.BlockSpec((s, 128), lambda bi, i: (0, 0)),
                  pl.BlockSpec((s, 128), lambda bi, i: (0, 0)),
                  pl.BlockSpec((1, tq, hd), lambda bi, i: (bi, i, (OFF_GP + 3 * BR_W) // hd))],
        out_specs=pl.BlockSpec((1, tq, hd), lambda bi, i: (bi, i, 0)),
        out_shape=jax.ShapeDtypeStruct((b, s, hd), BF16),
        scratch_shapes=[pltpu.VMEM((s, kd), BF16), pltpu.VMEM((s, 2 * kd), BF16),
                        pltpu.VMEM((lc, kd), BF16), pltpu.VMEM((lc, 2 * kd), BF16)],
        compiler_params=_cp(("parallel", "arbitrary"), 48),
        name="swa_attn",
    )(z, z, z, cache_k, cache_v, sink.reshape(1, SWA_H), cos2, sin2, z)


def _merge_body(h_ref, o0_ref, o1_ref, o2_ref, o3_ref, g0_ref, g1_ref, g2_ref, g3_ref, wb_ref, y_ref):
    h = h_ref[...]
    o_refs = (o0_ref, o1_ref, o2_ref, o3_ref)
    g_refs = (g0_ref, g1_ref, g2_ref, g3_ref)
    y = None
    for n in range(N_BRANCH):
        t = _sigmoid(_dot(h, g_refs[n][...])) * _dot(o_refs[n][...], wb_ref[n])
        y = t if y is None else y + t
    y_ref[...] = y.astype(BF16)


def _merge(h, ogs, w_gate, w_branch, layer, tm, tn):
    m, d = h.shape
    tm = min(tm, m)
    tn = min(tn, d)
    nj = d // tn
    gspec = lambda n: pl.BlockSpec((None, d, tn), lambda i, j: (layer, 0, n * nj + j))
    return pl.pallas_call(
        _merge_body,
        grid=(m // tm, nj),
        in_specs=[pl.BlockSpec((tm, d), lambda i, j: (i, 0))]
        + [pl.BlockSpec((tm, BR_W), lambda i, j: (i, 0))] * N_BRANCH
        + [gspec(n) for n in range(N_BRANCH)]
        + [pl.BlockSpec((None, N_BRANCH, BR_W, tn), lambda i, j: (layer, 0, 0, j))],
        out_specs=pl.BlockSpec((tm, tn), lambda i, j: (i, j)),
        out_shape=jax.ShapeDtypeStruct((m, d), BF16),
        compiler_params=_cp(("parallel", "arbitrary"), 56),
        name="merge",
    )(h, *ogs, w_gate, w_gate, w_gate, w_gate, w_branch)


OUT_SUB = 2


def _out_body(*refs, has_next):
    if has_next:
        y_ref, w_ref, x_ref, gate_ref, g_ref, gn_ref, sh_ref, sc_ref, o_ref, h_ref = refs
    else:
        y_ref, w_ref, x_ref, gate_ref, g_ref, o_ref = refs
    ts = y_ref.shape[1]
    tsub = ts // OUT_SUB
    for sub in range(OUT_SUB):
        rows = slice(sub * tsub, (sub + 1) * tsub)
        t = _dot(y_ref[0, rows, :], w_ref[...])
        t = t * lax.rsqrt(jnp.mean(t * t, axis=-1, keepdims=True) + EPS) * g_ref[...]
        x = x_ref[0, rows, :] + gate_ref[0] * t
        o_ref[0, rows, :] = x
        if has_next:
            y = x * lax.rsqrt(jnp.mean(x * x, axis=-1, keepdims=True) + EPS)
            y = y * gn_ref[...]
            h_ref[0, rows, :] = (y * (1.0 + sc_ref[0]) + sh_ref[0]).astype(BF16)


def _out_proj(y, w_out, layer, x, gate, g, nxt):
    b, s, d = x.shape
    ts = min(s, 512)
    per_b = gate.shape[0] > 1
    cidx = (lambda bi, i: (bi, 0, 0)) if per_b else (lambda bi, i: (0, 0, 0))
    tile = pl.BlockSpec((1, ts, d), lambda bi, i: (bi, i, 0))
    vec = pl.BlockSpec((1, d), lambda bi, i: (0, 0))
    in_specs = [tile, pl.BlockSpec((None, d, d), lambda bi, i: (layer, 0, 0), pipeline_mode=pl.Buffered(1)),
                tile, pl.BlockSpec((1, 1, d), cidx), vec]
    args = [y, w_out, x, gate, g.reshape(1, d)]
    out_specs = [tile]
    out_shape = [jax.ShapeDtypeStruct((b, s, d), F32)]
    if nxt is not None:
        gn, shift, scale = nxt
        in_specs += [vec, pl.BlockSpec((1, 1, d), cidx), pl.BlockSpec((1, 1, d), cidx)]
        args += [gn.reshape(1, d), shift, scale]
        out_specs.append(tile)
        out_shape.append(jax.ShapeDtypeStruct((b, s, d), BF16))
    res = pl.pallas_call(
        functools.partial(_out_body, has_next=nxt is not None),
        grid=(b, s // ts),
        in_specs=in_specs,
        out_specs=out_specs,
        out_shape=out_shape,
        compiler_params=_cp(("parallel", "parallel"), 56),
        name="out_proj",
    )(*args)
    return (res[0], res[1]) if nxt is not None else (res[0], None)


_MIX_SRC = ((0, 512), (832, 4416), (4672, 6720), (512, 768), (4416, 4672), (768, 832))
_MIX_COLS = 6720


def _relayout_body(w_ref, mix_ref, gate_ref):
    w = w_ref[0]
    pieces = [w[:, a:b] for a, b in _MIX_SRC]
    pieces.append(jnp.zeros((w.shape[0], N_MIX - _MIX_COLS), F32))
    mix_ref[0] = jnp.concatenate(pieces, axis=-1).astype(BF16)
    gate_ref[0] = w[:, _MIX_COLS:].astype(BF16)


def _relayout_w_in(w_in):
    depth, d, n = w_in.shape
    assert sum(b - a for a, b in _MIX_SRC) == _MIX_COLS
    tr = 128
    return pl.pallas_call(
        _relayout_body,
        grid=(depth, d // tr),
        in_specs=[pl.BlockSpec((1, tr, n), lambda l, i: (l, i, 0))],
        out_specs=[pl.BlockSpec((1, tr, N_MIX), lambda l, i: (l, i, 0)),
                   pl.BlockSpec((1, tr, n - _MIX_COLS), lambda l, i: (l, i, 0))],
        out_shape=[jax.ShapeDtypeStruct((depth, d, N_MIX), BF16),
                   jax.ShapeDtypeStruct((depth, d, n - _MIX_COLS), BF16)],
        compiler_params=_cp(("parallel", "parallel"), 56),
        name="relayout_w_in",
    )(w_in)


def _mla_weights(w_q_up, w_kv_up):
    depth = w_q_up.shape[0]
    wq = w_q_up.reshape(depth, MLA_Q_LORA, MLA_H, MLA_NOPE + MLA_ROPE)
    wq = jnp.pad(wq, ((0, 0), (0, 0), (0, 0), (0, MLA_DQ - MLA_NOPE - MLA_ROPE)))
    wq = wq.reshape(depth, MLA_Q_LORA, MLA_H * MLA_DQ).astype(BF16)
    wkv = w_kv_up.reshape(depth, MLA_KV_LORA, MLA_H, MLA_NOPE + MLA_V)
    wk = jnp.pad(wkv[..., :MLA_NOPE], ((0, 0), (0, 0), (0, 0), (0, MLA_DQ - MLA_NOPE)))
    wk = wk.reshape(depth, MLA_KV_LORA, MLA_H * MLA_DQ).astype(BF16)
    wv = jnp.pad(wkv[..., MLA_NOPE:], ((0, 0), (0, 0), (0, 0), (0, MLA_DV - MLA_V)))
    wv = wv.reshape(depth, MLA_KV_LORA, MLA_H * MLA_DV).astype(BF16)
    j = jnp.arange(128)[:, None]
    cidx = jnp.arange(MLA_H * MLA_DQ)[None, :]
    p128 = ((cidx % MLA_DQ == MLA_NOPE + j) & (j < MLA_ROPE)).astype(BF16)
    vones = (jnp.arange(MLA_H * MLA_DV)[None, :] % MLA_DV == MLA_V).astype(F32)
    return wq, wk, wv, p128, vones


def _rope_tables(n_tok):
    pos = jnp.arange(n_tok)
    row = (pos // GRID_W).astype(F32)
    col = (pos % GRID_W).astype(F32)
    n_freq = MLA_ROPE // 4
    inv = ROPE_BASE ** (-jnp.arange(n_freq, dtype=F32) / n_freq)
    ang = jnp.concatenate([row[:, None] * inv[None], col[:, None] * inv[None]], axis=-1)
    cos, sin = jnp.cos(ang), jnp.sin(ang)
    c64 = jnp.concatenate([cos, cos], axis=-1)
    s64 = jnp.concatenate([-sin, sin], axis=-1)
    one, zero = jnp.ones_like(c64), jnp.zeros_like(c64)
    cq = jnp.concatenate([one, one, c64, one], axis=-1)
    sq = jnp.concatenate([zero, zero, s64, zero], axis=-1)
    ck = jnp.concatenate([c64, one], axis=-1)
    sk = jnp.concatenate([s64, zero], axis=-1)
    c2 = jnp.concatenate([c64, c64], axis=-1)
    s2 = jnp.concatenate([s64, s64], axis=-1)
    return (cq, sq, ck, sk), (c2, s2)


TM_IN = 2048
TN_IN = 768
TM_MERGE = 1024
TN_MERGE = 512


def kernel(x_prompt, x_sample, cache_mla_ckv, cache_mla_krope, state_ret, cache_nat_k, cache_nat_v,
           cache_swa_k, cache_swa_v, c, c_ctx, w_mod, b_mod, norm_pre, norm_post, w_in, mla_q_norm,
           mla_kv_norm, mla_w_q_up, mla_w_kv_up, ret_decay, ret_norm, nat_rpb, swa_sink, w_branch, w_out):
    depth = w_mod.shape[0]
    d = x_prompt.shape[-1]
    nb = x_sample.shape[0]
    s_lat = x_sample.shape[1]
    l_ctx = cache_mla_ckv.shape[2]

    w_mix, w_gate = _relayout_w_in(w_in)
    w_br = w_branch.astype(BF16)
    w_o = w_out.astype(BF16)
    wq, wk, wv, p128, vones = _mla_weights(mla_w_q_up, mla_w_kv_up)
    p64 = p128[:MLA_ROPE]
    mla_tabs, (cos2, sin2) = _rope_tables(s_lat)
    nat_bias = _nat_bias_tables(nat_rpb, s_lat // GRID_W)
    nat_k_c = cache_nat_k.reshape(nb, depth, l_ctx, NAT_H * NAT_D)
    nat_v_c = cache_nat_v.reshape(nb, depth, l_ctx, NAT_H * NAT_D)
    swa_k_c = cache_swa_k.reshape(nb, depth, l_ctx, SWA_KVH * SWA_D)
    swa_v_c = cache_swa_v.reshape(nb, depth, l_ctx, SWA_KVH * SWA_D)

    rows = 8
    conds = jnp.concatenate([c_ctx[None, :], c, jnp.zeros((rows - 1 - nb, d), F32)], axis=0)
    mod = _modulation(conds, w_mod, b_mod)

    def mod_parts(l, lo, hi):
        m = mod[l, lo:hi]
        return m[:, None, :d], m[:, None, d:2 * d], m[:, None, 2 * d:]

    def finish_layer(l, lo, hi, x, h, ogs):
        b, s, _ = x.shape
        gate = mod_parts(l, lo, hi)[2]
        y = _merge(h.reshape(b * s, d), [o.reshape(b * s, BR_W) for o in ogs], w_gate, w_br, l, TM_MERGE, TN_MERGE)
        nxt = None
        if l + 1 < depth:
            shift_n, scale_n, _ = mod_parts(l + 1, lo, hi)
            nxt = (norm_pre[l + 1], shift_n, scale_n)
        return _out_proj(y.reshape(b, s, d), w_o, l, x, gate, norm_post[l], nxt)

    def start_pass(x, lo, hi):
        shift, scale, _ = mod_parts(0, lo, hi)
        return _modnorm(x, norm_pre[0], shift, scale)

    def in_proj(x, h, l):
        b, s, _ = x.shape
        return _in_proj(h.reshape(b * s, d), w_mix, l, TM_IN, TN_IN).reshape(b, s, N_MIX)

    yp = x_prompt
    bc, sc, _ = yp.shape
    ckv_l, kr_l, st_l, nk_l, nv_l, sk_l, sv_l = [], [], [], [], [], [], []
    h = start_pass(yp, 0, 1)
    for l in range(depth):
        z = in_proj(yp, h, l)
        q, kf, vf, ckv = _mla_prep(z, mla_q_norm[l], mla_kv_norm[l], wq[l], wk[l], wv[l], p128, vones, None)
        og_mla, og_nat, og_swa = _ctx_attn(q, kf, vf, z, swa_sink[l])
        og_ret, st = _retention(z, ret_decay[l], ret_norm[l], None, 0, True)
        yp, h = finish_layer(l, 0, 1, yp, h, (og_mla, og_ret, og_nat, og_swa))
        ckv_l.append(ckv)
        kr_l.append(z[:, :, OFF_KR:OFF_KR + MLA_ROPE])
        st_l.append(st)
        nk_l.append(z[:, :, OFF_NK:OFF_NK + 512].reshape(bc, sc, NAT_H, NAT_D))
        nv_l.append(z[:, :, OFF_NV:OFF_NV + 512].reshape(bc, sc, NAT_H, NAT_D))
        sk_l.append(z[:, :, OFF_SK:OFF_SK + 128].reshape(bc, sc, SWA_KVH, SWA_D))
        sv_l.append(z[:, :, OFF_SV:OFF_SV + 128].reshape(bc, sc, SWA_KVH, SWA_D))

    ys = x_sample
    h = start_pass(ys, 1, 1 + nb)
    for l in range(depth):
        z = in_proj(ys, h, l)
        q, kf, vf, _ = _mla_prep(z, mla_q_norm[l], mla_kv_norm[l], wq[l], wk[l], wv[l], p128, vones, mla_tabs)
        kc, vc = _mla_expand(cache_mla_ckv, cache_mla_krope, l, wk[l], wv[l], p64, vones)
        og_mla = _mla_attn(q, kf, vf, z, OFF_GP // BR_W, ctx=(kc, vc))
        og_ret, _ = _retention(z, ret_decay[l], ret_norm[l], state_ret, l, False)
        og_nat = _nat(z, nat_k_c, nat_v_c, l, nat_bias)
        og_swa = _swa(z, swa_k_c, swa_v_c, l, swa_sink[l], cos2, sin2)
        ys, h = finish_layer(l, 1, 1 + nb, ys, h, (og_mla, og_ret, og_nat, og_swa))

    return (yp, ys, jnp.stack(ckv_l, axis=1), jnp.stack(kr_l, axis=1), jnp.stack(st_l, axis=1),
            jnp.stack(nk_l, axis=1), jnp.stack(nv_l, axis=1), jnp.stack(sk_l, axis=1), jnp.stack(sv_l, axis=1))
```

```python
import functools

import jax
import jax.numpy as jnp
from jax import lax
from jax.experimental import pallas as pl
from jax.experimental.pallas import tpu as pltpu

F32 = jnp.float32
BF16 = jnp.bfloat16

GRID_W = 64
BLK = 128
ROPE_BASE = 10000.0
EPS = 1e-6
NEG_INF = -1e30
N_BRANCH = 4
BR_W = 512

MLA_H = 4
MLA_Q_LORA = 512
MLA_KV_LORA = 256
MLA_NOPE = 128
MLA_ROPE = 64
MLA_V = 128
MLA_DQ = 256
MLA_DV = 256
MLA_QSCALE = (MLA_NOPE + MLA_ROPE) ** -0.5 * 1.4426950408889634

RET_H = 4
RET_DK = 128
RET_DV = 128
RET_HPS = 2

NAT_H = 4
NAT_D = 128
NAT_ROWS = 8
NAT_COLS = 16
NAT_QR = 4
NAT_WR = NAT_ROWS + NAT_QR

SWA_H = 8
SWA_KVH = 2
SWA_D = 64
SWA_WINDOW = 128

OFF_QA = 0
OFF_RQ = 512
OFF_RK = 1024
OFF_RV = 1536
OFF_NQ = 2048
OFF_NK = 2560
OFF_NV = 3072
OFF_SQ = 3584
OFF_GP = 4096
OFF_KVA = 6144
OFF_SK = 6400
OFF_SV = 6528
OFF_KR = 6656
N_MIX_USED = 6784
N_MIX = 6912

V7X_VMEM_BYTES = 64 * 1024 * 1024


def _cp(sem, vmem_mb):
    assert vmem_mb * 1024 * 1024 < V7X_VMEM_BYTES
    return pltpu.CompilerParams(dimension_semantics=sem, vmem_limit_bytes=vmem_mb * 1024 * 1024)


def _sigmoid(x):
    return 1.0 / (1.0 + jnp.exp(-x))


def _silu(x):
    return x * _sigmoid(x)


def _swap_halves(x):
    n = x.shape[-1]
    ax = x.ndim - 1
    lane = lax.broadcasted_iota(jnp.int32, x.shape, ax)
    return jnp.where((lane & 32) == 0, pltpu.roll(x, n - 32, ax), pltpu.roll(x, 32, ax))


def _dot_nt(a, b):
    return lax.dot_general(a, b, (((1,), (1,)), ((), ())), preferred_element_type=F32)


def _dot(a, b):
    return jnp.dot(a, b, preferred_element_type=F32)


def _mod_body(c_ref, w_ref, b_ref, o_ref):
    a = _silu(c_ref[...]).astype(BF16)
    o_ref[0] = _dot(a, w_ref[0].astype(BF16)) + b_ref[0]


def _modulation(conds, w_mod, b_mod):
    depth, d, n = w_mod.shape
    rows = conds.shape[0]
    tn = 1024 if n % 1024 == 0 else n
    return pl.pallas_call(
        _mod_body,
        grid=(depth, n // tn),
        in_specs=[pl.BlockSpec((rows, d), lambda l, j: (0, 0)),
                  pl.BlockSpec((1, d, tn), lambda l, j: (l, 0, j)),
                  pl.BlockSpec((1, 1, tn), lambda l, j: (l, 0, j))],
        out_specs=pl.BlockSpec((1, rows, tn), lambda l, j: (l, 0, j)),
        out_shape=jax.ShapeDtypeStruct((depth, rows, n), F32),
        compiler_params=_cp(("parallel", "parallel"), 40),
        name="modulation",
    )(conds, w_mod, b_mod.reshape(depth, 1, n))


def _modnorm_body(x_ref, g_ref, sh_ref, sc_ref, h_ref):
    x = x_ref[0]
    y = x * lax.rsqrt(jnp.mean(x * x, axis=-1, keepdims=True) + EPS)
    y = y * g_ref[...]
    h_ref[0] = (y * (1.0 + sc_ref[0]) + sh_ref[0]).astype(BF16)


def _modnorm(x, g, shift, scale):
    b, s, d = x.shape
    ts = min(s, 512)
    per_b = shift.shape[0] > 1
    cidx = (lambda bi, i: (bi, 0, 0)) if per_b else (lambda bi, i: (0, 0, 0))
    return pl.pallas_call(
        _modnorm_body,
        grid=(b, s // ts),
        in_specs=[pl.BlockSpec((1, ts, d), lambda bi, i: (bi, i, 0)),
                  pl.BlockSpec((1, d), lambda bi, i: (0, 0)),
                  pl.BlockSpec((1, 1, d), cidx),
                  pl.BlockSpec((1, 1, d), cidx)],
        out_specs=pl.BlockSpec((1, ts, d), lambda bi, i: (bi, i, 0)),
        out_shape=jax.ShapeDtypeStruct((b, s, d), BF16),
        compiler_params=_cp(("parallel", "parallel"), 32),
        name="modnorm",
    )(x, g.reshape(1, d), shift, scale)


def _mm_body(a_ref, b_ref, o_ref):
    o_ref[...] = _dot_nt(a_ref[...], b_ref[...])


def _in_proj(a, w_mix_t, layer, tm, tn):
    m, k = a.shape
    tm = min(tm, m)
    return pl.pallas_call(
        _mm_body,
        grid=(m // tm, N_MIX // tn),
        in_specs=[pl.BlockSpec((tm, k), lambda i, j: (i, 0)),
                  pl.BlockSpec((None, tn, k), lambda i, j: (layer, j, 0))],
        out_specs=pl.BlockSpec((tm, tn), lambda i, j: (i, j)),
        out_shape=jax.ShapeDtypeStruct((m, N_MIX), F32),
        compiler_params=_cp(("parallel", "arbitrary"), 48),
        name="in_proj",
    )(a, w_mix_t)


def _mla_prep_body(*refs, rope):
    if rope:
        (qa_ref, kva_ref, kr_ref, gq_ref, gkv_ref, wq_ref, wk_ref, wv_ref, p_ref, one_ref,
         cq_ref, sq_ref, ck_ref, sk_ref, q_out, k_out, v_out, ckv_out) = refs
    else:
        (qa_ref, kva_ref, kr_ref, gq_ref, gkv_ref, wq_ref, wk_ref, wv_ref, p_ref, one_ref,
         q_out, k_out, v_out, ckv_out) = refs
    qa = qa_ref[0]
    cq = qa * lax.rsqrt(jnp.mean(qa * qa, axis=-1, keepdims=True) + EPS) * gq_ref[...]
    q = _dot(cq.astype(BF16), wq_ref[...])
    kva = kva_ref[0]
    ckv = kva * lax.rsqrt(jnp.mean(kva * kva, axis=-1, keepdims=True) + EPS) * gkv_ref[...]
    ckv_out[0] = ckv
    kr = kr_ref[0]
    if rope:
        cq_t = jnp.concatenate([cq_ref[...]] * MLA_H, axis=-1)
        sq_t = jnp.concatenate([sq_ref[...]] * MLA_H, axis=-1)
        q = q * cq_t + _swap_halves(q) * sq_t
        kr = kr * ck_ref[...] + _swap_halves(kr) * sk_ref[...]
    q_out[0] = (q * MLA_QSCALE).astype(BF16)
    cb = ckv.astype(BF16)
    k_out[0] = (_dot(cb, wk_ref[...]) + _dot(kr.astype(BF16), p_ref[...])).astype(BF16)
    v_out[0] = (_dot(cb, wv_ref[...]) + one_ref[...]).astype(BF16)


def _mla_prep(z, gq, gkv, wq, wk, wv, p128, vones, tabs):
    b, s, _ = z.shape
    tm = min(s, 512)
    rope = tabs is not None
    full = lambda shape: pl.BlockSpec(shape, lambda bi, i: (0,) * len(shape))
    in_specs = [pl.BlockSpec((1, tm, MLA_Q_LORA), lambda bi, i: (bi, i, OFF_QA // MLA_Q_LORA)),
                pl.BlockSpec((1, tm, MLA_KV_LORA), lambda bi, i: (bi, i, OFF_KVA // MLA_KV_LORA)),
                pl.BlockSpec((1, tm, 128), lambda bi, i: (bi, i, OFF_KR // 128)),
                full((1, MLA_Q_LORA)), full((1, MLA_KV_LORA)),
                full(wq.shape), full(wk.shape), full(wv.shape), full(p128.shape), full(vones.shape)]
    args = [z, z, z, gq.reshape(1, -1), gkv.reshape(1, -1), wq, wk, wv, p128, vones]
    if rope:
        cq, sq, ck, sk = tabs
        in_specs += [pl.BlockSpec((tm, MLA_DQ), lambda bi, i: (i, 0))] * 2
        in_specs += [pl.BlockSpec((tm, 128), lambda bi, i: (i, 0))] * 2
        args += [cq, sq, ck, sk]
    hq = MLA_H * MLA_DQ
    hv = MLA_H * MLA_DV
    return pl.pallas_call(
        functools.partial(_mla_prep_body, rope=rope),
        grid=(b, s // tm),
        in_specs=in_specs,
        out_specs=[pl.BlockSpec((1, tm, hq), lambda bi, i: (bi, i, 0)),
                   pl.BlockSpec((1, tm, hq), lambda bi, i: (bi, i, 0)),
                   pl.BlockSpec((1, tm, hv), lambda bi, i: (bi, i, 0)),
                   pl.BlockSpec((1, tm, MLA_KV_LORA), lambda bi, i: (bi, i, 0))],
        out_shape=[jax.ShapeDtypeStruct((b, s, hq), BF16),
                   jax.ShapeDtypeStruct((b, s, hq), BF16),
                   jax.ShapeDtypeStruct((b, s, hv), BF16),
                   jax.ShapeDtypeStruct((b, s, MLA_KV_LORA), F32)],
        compiler_params=_cp(("parallel", "parallel"), 40),
        name="mla_prep",
    )(*args)


def _mla_expand_body(ckv_ref, kr_ref, wk_ref, wv_ref, p_ref, one_ref, k_out, v_out):
    cb = ckv_ref[0, 0].astype(BF16)
    k_out[0] = (_dot(cb, wk_ref[...]) + _dot(kr_ref[0, 0].astype(BF16), p_ref[...])).astype(BF16)
    v_out[0] = (_dot(cb, wv_ref[...]) + one_ref[...]).astype(BF16)


def _mla_expand(cache_ckv, cache_kr, layer, wk, wv, p64, vones):
    b, _, s, _ = cache_ckv.shape
    full = lambda shape: pl.BlockSpec(shape, lambda bi: (0,) * len(shape))
    hq = MLA_H * MLA_DQ
    hv = MLA_H * MLA_DV
    return pl.pallas_call(
        _mla_expand_body,
        grid=(b,),
        in_specs=[pl.BlockSpec((1, 1, s, MLA_KV_LORA), lambda bi: (bi, layer, 0, 0)),
                  pl.BlockSpec((1, 1, s, MLA_ROPE), lambda bi: (bi, layer, 0, 0)),
                  full(wk.shape), full(wv.shape), full(p64.shape), full(vones.shape)],
        out_specs=[pl.BlockSpec((1, s, hq), lambda bi: (bi, 0, 0)),
                   pl.BlockSpec((1, s, hv), lambda bi: (bi, 0, 0))],
        out_shape=[jax.ShapeDtypeStruct((b, s, hq), BF16),
                   jax.ShapeDtypeStruct((b, s, hv), BF16)],
        compiler_params=_cp(("parallel",), 32),
        name="mla_expand",
    )(cache_ckv, cache_kr, wk, wv, p64, vones)


def _mla_attn_body(*refs, has_ctx, nkv):
    it = iter(refs)
    q_ref, k_ref, v_ref = next(it), next(it), next(it)
    kc_ref = vc_ref = None
    if has_ctx:
        kc_ref, vc_ref = next(it), next(it)
    gp_ref, o_ref, m_scr, acc_scr = next(it), next(it), next(it), next(it)
    j = pl.program_id(2)

    def process(kr, vr):
        tk = kr.shape[1]
        for h in range(MLA_H):
            qk = slice(h * MLA_DQ, (h + 1) * MLA_DQ)
            s = _dot_nt(q_ref[0, :, qk], kr[0, :, qk])
            m_prev = m_scr[h]
            m_new = jnp.maximum(m_prev, jnp.max(s, axis=-1, keepdims=True))
            alpha = jnp.exp2(m_prev - m_new)
            p = jnp.exp2(s - jnp.concatenate([m_new] * (tk // 128), axis=-1))
            pv = _dot(p.astype(BF16), vr[0, :, h * MLA_DV:(h + 1) * MLA_DV])
            acc_scr[h] = jnp.concatenate([alpha] * (MLA_DV // 128), axis=-1) * acc_scr[h] + pv
            m_scr[h] = m_new

    @pl.when(j == 0)
    def _():
        m_scr[...] = jnp.full(m_scr.shape, NEG_INF, F32)
        acc_scr[...] = jnp.zeros(acc_scr.shape, F32)
        if has_ctx:
            process(kc_ref, vc_ref)

    process(k_ref, v_ref)

    @pl.when(j == nkv - 1)
    def _():
        outs = []
        for h in range(MLA_H):
            a = acc_scr[h]
            outs.append(a[:, :MLA_V] / a[:, MLA_V:MLA_V + 1])
        o_ref[0] = (jnp.concatenate(outs, axis=-1) * _silu(gp_ref[0])).astype(BF16)


def _mla_attn(q, k, v, gp, gpcol, ctx=None, tq=1024, tk=1024):
    b, sq, _ = q.shape
    sk = k.shape[1]
    tq = min(tq, sq)
    tk = min(tk, sk)
    nkv = sk // tk
    hq = MLA_H * MLA_DQ
    hv = MLA_H * MLA_DV
    in_specs = [pl.BlockSpec((1, tq, hq), lambda bi, i, j: (bi, i, 0)),
                pl.BlockSpec((1, tk, hq), lambda bi, i, j: (bi, j, 0)),
                pl.BlockSpec((1, tk, hv), lambda bi, i, j: (bi, j, 0))]
    args = [q, k, v]
    if ctx is not None:
        kc, vc = ctx
        in_specs += [pl.BlockSpec((1,) + kc.shape[1:], lambda bi, i, j: (bi, 0, 0)),
                     pl.BlockSpec((1,) + vc.shape[1:], lambda bi, i, j: (bi, 0, 0))]
        args += [kc, vc]
    in_specs += [pl.BlockSpec((1, tq, MLA_H * MLA_V), lambda bi, i, j: (bi, i, gpcol))]
    args += [gp]
    return pl.pallas_call(
        functools.partial(_mla_attn_body, has_ctx=ctx is not None, nkv=nkv),
        grid=(b, sq // tq, nkv),
        in_specs=in_specs,
        out_specs=pl.BlockSpec((1, tq, MLA_H * MLA_V), lambda bi, i, j: (bi, i, 0)),
        out_shape=jax.ShapeDtypeStruct((b, sq, MLA_H * MLA_V), BF16),
        scratch_shapes=[pltpu.VMEM((MLA_H, tq, 128), F32), pltpu.VMEM((MLA_H, tq, MLA_DV), F32)],
        compiler_params=_cp(("parallel", "parallel", "arbitrary"), 48),
        name="mla_attn",
    )(*args)


def _ctx_attn_body(qm_ref, km_ref, vm_ref, nq_ref, nk_ref, nv_ref, sq_ref, sk_ref, sv_ref, sink_ref,
                   gpm_ref, gpn_ref, gps_ref, om_ref, on_ref, os_ref):
    outs = []
    for h in range(MLA_H):
        qk = slice(h * MLA_DQ, (h + 1) * MLA_DQ)
        s = _dot_nt(qm_ref[0, :, qk], km_ref[0, :, qk])
        p = jnp.exp2(s - jnp.max(s, axis=-1, keepdims=True))
        pv = _dot(p.astype(BF16), vm_ref[0, :, h * MLA_DV:(h + 1) * MLA_DV])
        outs.append(pv[:, :MLA_V] / pv[:, MLA_V:MLA_V + 1])
    om_ref[0] = (jnp.concatenate(outs, axis=-1) * _silu(gpm_ref[0])).astype(BF16)

    outs = []
    for h in range(NAT_H):
        sl = slice(h * NAT_D, (h + 1) * NAT_D)
        s = _dot_nt(nq_ref[0, :, sl].astype(BF16), nk_ref[0, :, sl].astype(BF16)) * NAT_D ** -0.5
        p = jnp.exp(s - jnp.max(s, axis=-1, keepdims=True))
        den = jnp.sum(p, axis=-1, keepdims=True)
        outs.append(_dot(p.astype(BF16), nv_ref[0, :, sl].astype(BF16)) / den)
    on_ref[0] = (jnp.concatenate(outs, axis=-1) * _silu(gpn_ref[0])).astype(BF16)

    n_tok = sq_ref.shape[1]
    group = SWA_H // SWA_KVH
    q = sq_ref[0] * SWA_D ** -0.5
    outs = []
    for g in range(SWA_KVH):
        heads = range(g * group, (g + 1) * group)
        qg = jnp.concatenate([q[:, h * SWA_D:(h + 1) * SWA_D] for h in heads], axis=0).astype(BF16)
        sink = jnp.concatenate([jnp.broadcast_to(sink_ref[0:1, h:h + 1], (n_tok, 1)) for h in heads], axis=0)
        ksl = slice(g * SWA_D, (g + 1) * SWA_D)
        s = _dot_nt(qg, sk_ref[0, :, ksl].astype(BF16))
        m = jnp.maximum(jnp.max(s, axis=-1, keepdims=True), sink)
        p = jnp.exp(s - m)
        den = jnp.sum(p, axis=-1, keepdims=True) + jnp.exp(sink - m)
        o = _dot(p.astype(BF16), sv_ref[0, :, ksl].astype(BF16)) / den
        outs += [o[t * n_tok:(t + 1) * n_tok] for t in range(group)]
    os_ref[0] = (jnp.concatenate(outs, axis=-1) * _silu(gps_ref[0])).astype(BF16)


def _ctx_attn(q, k, v, z, sink):
    b, n_tok, _ = z.shape
    blk = lambda w, off: pl.BlockSpec((1, n_tok, w), lambda bi: (bi, 0, off // w))
    kd = SWA_KVH * SWA_D
    out = jax.ShapeDtypeStruct((b, n_tok, BR_W), BF16)
    return pl.pallas_call(
        _ctx_attn_body,
        grid=(b,),
        in_specs=[blk(MLA_H * MLA_DQ, 0), blk(MLA_H * MLA_DQ, 0), blk(MLA_H * MLA_DV, 0),
                  blk(BR_W, OFF_NQ), blk(BR_W, OFF_NK), blk(BR_W, OFF_NV),
                  blk(BR_W, OFF_SQ), blk(kd, OFF_SK), blk(kd, OFF_SV),
                  pl.BlockSpec((1, SWA_H), lambda bi: (0, 0)),
                  blk(BR_W, OFF_GP), blk(BR_W, OFF_GP + 2 * BR_W), blk(BR_W, OFF_GP + 3 * BR_W)],
        out_specs=[blk(BR_W, 0)] * 3,
        out_shape=[out, out, out],
        compiler_params=_cp(("parallel",), 48),
        name="ctx_attn",
    )(q, k, v, z, z, z, z, z, z, sink.reshape(1, SWA_H), z, z, z)


def _ret_body(*refs, nc, has_s0, has_st):
    it = iter(refs)
    dec_ref, norm_ref, q_ref, k_ref, v_ref = next(it), next(it), next(it), next(it), next(it)
    s0_ref = next(it) if has_s0 else None
    gp_ref, o_ref = next(it), next(it)
    st_ref = next(it) if has_st else None
    of_scr, ob_scr = next(it), next(it)
    hgrp = pl.program_id(1)

    d = dec_ref[0]
    lg_all = jnp.minimum(d, 0.0) - jnp.log1p(jnp.exp(-jnp.abs(d)))
    hcol = lax.broadcasted_iota(jnp.int32, d.shape, 1)
    row = lax.broadcasted_iota(jnp.int32, (BLK, BLK), 0).astype(F32)
    col = lax.broadcasted_iota(jnp.int32, (BLK, BLK), 1).astype(F32)
    a = lax.broadcasted_iota(jnp.int32, (BLK, 1), 0).astype(F32)
    kscale = RET_DK ** -0.5

    heads = []
    for hh in range(RET_HPS):
        lg = jnp.sum(jnp.where(hcol == hgrp * RET_HPS + hh, lg_all, 0.0), axis=1, keepdims=True)
        lg_f, lg_b = lg[0:1, :], lg[1:2, :]
        fwd = (jnp.where(row - col >= 0, jnp.exp(lg_f * jnp.maximum(row - col, 0.0)), 0.0),
               jnp.exp(lg_f * (a + 1.0)), jnp.exp(lg_f * (BLK - 1.0 - a)), jnp.exp(lg_f * BLK))
        bwd = (jnp.where(col - row > 0, jnp.exp(lg_b * jnp.maximum(col - row, 0.0)), 0.0),
               jnp.exp(lg_b * (BLK - a)), jnp.exp(lg_b * a), jnp.exp(lg_b * BLK))
        heads.append((slice(hh * RET_DK, (hh + 1) * RET_DK), fwd, bwd))

    def chunk(off, sl, st, tabs):
        dm, qd, kd, cd = tabs
        q = q_ref[0, pl.ds(off, BLK), sl]
        k = k_ref[0, pl.ds(off, BLK), sl] * kscale
        v = v_ref[0, pl.ds(off, BLK), sl].astype(BF16)
        inner = _dot_nt(q.astype(BF16), k.astype(BF16)) * dm
        o = _dot(inner.astype(BF16), v) + _dot((q * qd).astype(BF16), st.astype(BF16))
        st = st * cd + _dot(jnp.transpose(k * kd).astype(BF16), v)
        return o, st

    def step(j, carry):
        off_f = pl.multiple_of(j * BLK, BLK)
        off_b = pl.multiple_of((nc - 1 - j) * BLK, BLK)
        new = []
        for (sl, fwd, bwd), (sf, sb) in zip(heads, carry):
            o, sf = chunk(off_f, sl, sf, fwd)
            of_scr[pl.ds(off_f, BLK), sl] = o
            o, sb = chunk(off_b, sl, sb, bwd)
            ob_scr[pl.ds(off_b, BLK), sl] = o
            new.append((sf, sb))
        return tuple(new)

    if has_s0:
        init = tuple((s0_ref[0, 0, 0, hh], s0_ref[0, 0, 1, hh]) for hh in range(RET_HPS))
    else:
        zero = jnp.zeros((RET_DK, RET_DV), F32)
        init = tuple((zero, zero) for _ in range(RET_HPS))
    fin = lax.fori_loop(0, nc, step, init, unroll=2)
    if has_st:
        for hh in range(RET_HPS):
            st_ref[0, 0, hh] = fin[hh][0]
            st_ref[0, 1, hh] = fin[hh][1]
    for sl, _, _ in heads:
        o = of_scr[:, sl] + ob_scr[:, sl]
        o = o * lax.rsqrt(jnp.mean(o * o, axis=-1, keepdims=True) + EPS) * norm_ref[:, sl]
        o_ref[0, :, sl] = (o * _silu(gp_ref[0, :, sl])).astype(BF16)


def _retention(z, decay, norm, s0, layer, want_state):
    b, s, _ = z.shape
    nc = s // BLK
    w = RET_HPS * RET_DK
    colspec = lambda off: pl.BlockSpec((1, s, w), lambda bi, g: (bi, 0, off // w + g))
    in_specs = [pl.BlockSpec((1, 2, RET_H), lambda bi, g: (0, 0, 0)),
                pl.BlockSpec((1, w), lambda bi, g: (0, g)),
                colspec(OFF_RQ), colspec(OFF_RK), colspec(OFF_RV)]
    args = [decay.reshape(1, 2, RET_H), norm.reshape(1, -1), z, z, z]
    if s0 is not None:
        in_specs.append(pl.BlockSpec((1, 1, 2, RET_HPS, RET_DK, RET_DV), lambda bi, g: (bi, layer, 0, g, 0, 0)))
        args.append(s0)
    in_specs.append(colspec(OFF_GP + BR_W))
    args.append(z)
    out_specs = [pl.BlockSpec((1, s, w), lambda bi, g: (bi, 0, g))]
    out_shape = [jax.ShapeDtypeStruct((b, s, RET_H * RET_DV), BF16)]
    if want_state:
        out_specs.append(pl.BlockSpec((1, 2, RET_HPS, RET_DK, RET_DV), lambda bi, g: (bi, 0, g, 0, 0)))
        out_shape.append(jax.ShapeDtypeStruct((b, 2, RET_H, RET_DK, RET_DV), F32))
    res = pl.pallas_call(
        functools.partial(_ret_body, nc=nc, has_s0=s0 is not None, has_st=want_state),
        grid=(b, RET_H // RET_HPS),
        in_specs=in_specs,
        out_specs=out_specs,
        out_shape=out_shape,
        scratch_shapes=[pltpu.VMEM((s, w), F32), pltpu.VMEM((s, w), F32)],
        compiler_params=_cp(("parallel", "parallel"), 56),
        name="retention",
    )(*args)
    return (res[0], res[1]) if want_state else (res[0], None)


def _nat_body(q_ref, k_ref, v_ref, kc_ref, vc_ref, bias_ref, gp_ref, o_ref, kb_scr, vb_scr, o_scr, *, rows):
    scale = NAT_D ** -0.5
    kb_scr[...] = k_ref[0].astype(BF16)
    vb_scr[...] = v_ref[0].astype(BF16)
    kc = kc_ref[0, 0].astype(BF16)
    vc = vc_ref[0, 0].astype(BF16)
    nq = NAT_QR * GRID_W
    win = NAT_WR * GRID_W
    nblk = rows // NAT_QR

    def step(t, _):
        r0 = t * NAT_QR
        ws = jnp.clip(r0 - NAT_ROWS // 2, 0, rows - NAT_WR)
        pat = jnp.where(t == 0, 0, jnp.where(t == nblk - 1, 2, 1))
        qoff = pl.multiple_of(r0 * GRID_W, nq)
        koff = pl.multiple_of(ws * GRID_W, GRID_W)
        q = q_ref[0, pl.ds(qoff, nq), :].astype(BF16)
        kw = kb_scr[pl.ds(koff, win), :]
        vw = vb_scr[pl.ds(koff, win), :]
        s_loc = _dot_nt(q, kw) * scale + bias_ref[0, 0, pat]
        s_ctx = _dot_nt(q, kc) * scale
        m = jnp.maximum(jnp.max(s_loc, axis=-1, keepdims=True), jnp.max(s_ctx, axis=-1, keepdims=True))
        p_loc = jnp.exp(s_loc - m)
        p_ctx = jnp.exp(s_ctx - m)
        den = jnp.sum(p_loc, axis=-1, keepdims=True) + jnp.sum(p_ctx, axis=-1, keepdims=True)
        o = _dot(p_loc.astype(BF16), vw) + _dot(p_ctx.astype(BF16), vc)
        o_scr[pl.ds(qoff, nq), :] = o / den
        return 0

    lax.fori_loop(0, nblk, step, 0, unroll=2)
    o_ref[0] = (o_scr[...] * _silu(gp_ref[0])).astype(BF16)


def _nat_bias_tables(rpb, rows):
    depth, n_h = rpb.shape[:2]
    c = jnp.arange(GRID_W)[:, None]
    kc = jnp.arange(GRID_W)[None, :]
    cs = jnp.clip(c - NAT_COLS // 2, 0, GRID_W - NAT_COLS)
    valid = (kc >= cs) & (kc < cs + NAT_COLS)
    onehot = ((kc - c + (NAT_COLS - 1))[:, :, None] == jnp.arange(2 * NAT_COLS - 1)[None, None, :]).astype(F32)
    toep = jnp.einsum("dhrj,ckj->dhrck", rpb.astype(F32), onehot, precision=lax.Precision.HIGHEST)
    toep = jnp.where(valid, toep, NEG_INF)
    neg = jnp.full((depth, n_h, GRID_W, GRID_W), NEG_INF, F32)
    pats = []
    for r0, ws in ((0, 0), (NAT_QR, 0), (rows - NAT_QR, rows - NAT_WR)):
        qrows = []
        for qi in range(NAT_QR):
            r = r0 + qi
            rs = min(max(r - NAT_ROWS // 2, 0), rows - NAT_ROWS)
            blocks = []
            for wm in range(NAT_WR):
                kr = ws + wm
                blocks.append(toep[:, :, kr - r + NAT_ROWS - 1] if rs <= kr < rs + NAT_ROWS else neg)
            qrows.append(jnp.concatenate(blocks, axis=-1))
        pats.append(jnp.concatenate(qrows, axis=-2))
    return jnp.stack(pats, axis=2)


def _nat(z, cache_k, cache_v, layer, bias):
    b, s, _ = z.shape
    rows = s // GRID_W
    assert rows % NAT_QR == 0 and rows >= 2 * NAT_WR - NAT_ROWS
    lc = cache_k.shape[2]
    colspec = lambda blk: pl.BlockSpec((1, s, NAT_D), lambda bi, h: (bi, 0, blk + h))
    cspec = pl.BlockSpec((1, 1, lc, NAT_D), lambda bi, h: (bi, layer, 0, h))
    return pl.pallas_call(
        functools.partial(_nat_body, rows=rows),
        grid=(b, NAT_H),
        in_specs=[colspec(OFF_NQ // 128), colspec(OFF_NK // 128), colspec(OFF_NV // 128), cspec, cspec,
                  pl.BlockSpec((1, 1) + bias.shape[2:], lambda bi, h: (layer, h, 0, 0, 0)),
                  colspec((OFF_GP + 2 * BR_W) // 128)],
        out_specs=pl.BlockSpec((1, s, NAT_D), lambda bi, h: (bi, 0, h)),
        out_shape=jax.ShapeDtypeStruct((b, s, NAT_H * NAT_D), BF16),
        scratch_shapes=[pltpu.VMEM((s, NAT_D), BF16), pltpu.VMEM((s, NAT_D), BF16),
                        pltpu.VMEM((s, NAT_D), F32)],
        compiler_params=_cp(("parallel", "parallel"), 48),
        name="nat_attn",
    )(z, z, z, cache_k, cache_v, bias, z)


def _swa_values_with_ones(v):
    lane = lax.broadcasted_iota(jnp.int32, v.shape, 1)
    halves = [jnp.where(lane < SWA_D, v if g == 0 else pltpu.roll(v, (SWA_KVH - g) * SWA_D, 1), 1.0)
              for g in range(SWA_KVH)]
    return jnp.concatenate(halves, axis=-1).astype(BF16)


def _swa_body(q_ref, k_ref, v_ref, kc_ref, vc_ref, sink_ref, cos_ref, sin_ref, gp_ref, o_ref,
              kr_scr, vl_scr, kc_scr, vc_scr, *, seq):
    scale = SWA_D ** -0.5
    group = SWA_H // SWA_KVH
    win = 3 * BLK
    i = pl.program_id(1)

    @pl.when(i == 0)
    def _():
        k = k_ref[0]
        kr_scr[...] = (k * cos_ref[...] + _swap_halves(k) * sin_ref[...]).astype(BF16)
        vl_scr[...] = _swa_values_with_ones(v_ref[0])
        kc_scr[...] = kc_ref[0, 0].astype(BF16)
        vc_scr[...] = _swa_values_with_ones(vc_ref[0, 0])

    start = pl.multiple_of(jnp.clip((i - 1) * BLK, 0, seq - win), BLK)
    qoff = pl.multiple_of(i * BLK, BLK)
    cq = cos_ref[pl.ds(qoff, BLK), :]
    sq = sin_ref[pl.ds(qoff, BLK), :]
    q = q_ref[0]
    q = (q * jnp.concatenate([cq] * (SWA_H // 2), axis=-1)
         + _swap_halves(q) * jnp.concatenate([sq] * (SWA_H // 2), axis=-1))
    q = q * scale
    kw = kr_scr[pl.ds(start, win), :]
    vw = vl_scr[pl.ds(start, win), :]

    qpos = qoff + lax.broadcasted_iota(jnp.int32, (BLK, win), 0)
    kpos = start + lax.broadcasted_iota(jnp.int32, (BLK, win), 1)
    valid = jnp.abs(qpos - kpos) <= SWA_WINDOW
    valid = jnp.concatenate([valid] * group, axis=0)

    outs = []
    for g in range(SWA_KVH):
        heads = range(g * group, (g + 1) * group)
        qg = jnp.concatenate([q[:, h * SWA_D:(h + 1) * SWA_D] for h in heads], axis=0).astype(BF16)
        sink = jnp.concatenate([jnp.broadcast_to(sink_ref[0:1, h:h + 1], (BLK, 1)) for h in heads], axis=0)
        ksl = slice(g * SWA_D, (g + 1) * SWA_D)
        vsl = slice(g * 2 * SWA_D, (g + 1) * 2 * SWA_D)
        s = jnp.concatenate([jnp.where(valid, _dot_nt(qg, kw[:, ksl]), NEG_INF),
                             _dot_nt(qg, kc_scr[:, ksl])], axis=-1)
        m = jnp.maximum(jnp.max(s, axis=-1, keepdims=True), sink)
        p = jnp.exp(s - m).astype(BF16)
        pv = _dot(p[:, :win], vw[:, vsl]) + _dot(p[:, win:], vc_scr[:, vsl])
        o = pv[:, :SWA_D] / (pv[:, SWA_D:SWA_D + 1] + jnp.exp(sink - m))
        outs += [o[t * BLK:(t + 1) * BLK] for t in range(group)]
    o = jnp.concatenate(outs, axis=-1)
    o_ref[0] = (o * _silu(gp_ref[0])).astype(BF16)


def _swa(z, cache_k, cache_v, layer, sink, cos2, sin2):
    b, s, _ = z.shape
    lc = cache_k.shape[2]
    hd = SWA_H * SWA_D
    kd = SWA_KVH * SWA_D
    cspec = pl.BlockSpec((1, 1, lc, kd), lambda bi, i: (bi, layer, 0, 0))
    tq = BLK
    return pl.pallas_call(
        functools.partial(_swa_body, seq=s),
        grid=(b, s // tq),
        in_specs=[pl.BlockSpec((1, tq, hd), lambda bi, i: (bi, i, OFF_SQ // hd)),
                  pl.BlockSpec((1, s, kd), lambda bi, i: (bi, 0, OFF_SK // kd)),
                  pl.BlockSpec((1, s, kd), lambda bi, i: (bi, 0, OFF_SV // kd)),
                  cspec, cspec,
                  pl.BlockSpec((1, SWA_H), lambda bi, i: (0, 0)),
                  pl.BlockSpec((s, 128), lambda bi, i: (0, 0)),
                  pl.BlockSpec((s, 128), lambda bi, i: (0, 0)),
                  pl.BlockSpec((1, tq, hd), lambda bi, i: (bi, i, (OFF_GP + 3 * BR_W) // hd))],
        out_specs=pl.BlockSpec((1, tq, hd), lambda bi, i: (bi, i, 0)),
        out_shape=jax.ShapeDtypeStruct((b, s, hd), BF16),
        scratch_shapes=[pltpu.VMEM((s, kd), BF16), pltpu.VMEM((s, 2 * kd), BF16),
                        pltpu.VMEM((lc, kd), BF16), pltpu.VMEM((lc, 2 * kd), BF16)],
        compiler_params=_cp(("parallel", "arbitrary"), 48),
        name="swa_attn",
    )(z, z, z, cache_k, cache_v, sink.reshape(1, SWA_H), cos2, sin2, z)


def _merge_body(h_ref, o0_ref, o1_ref, o2_ref, o3_ref, g0_ref, g1_ref, g2_ref, g3_ref, wb_ref, y_ref):
    h = h_ref[...]
    o_refs = (o0_ref, o1_ref, o2_ref, o3_ref)
    g_refs = (g0_ref, g1_ref, g2_ref, g3_ref)
    y = None
    for n in range(N_BRANCH):
        t = _sigmoid(_dot_nt(h, g_refs[n][...])) * _dot(o_refs[n][...], wb_ref[n])
        y = t if y is None else y + t
    y_ref[...] = y.astype(BF16)


def _merge(h, ogs, w_gate_t, w_branch, layer, tm, tn):
    m, d = h.shape
    tm = min(tm, m)
    tn = min(tn, d)
    nj = d // tn
    gspec = lambda n: pl.BlockSpec((None, tn, d), lambda i, j: (layer, n * nj + j, 0))
    return pl.pallas_call(
        _merge_body,
        grid=(m // tm, nj),
        in_specs=[pl.BlockSpec((tm, d), lambda i, j: (i, 0))]
        + [pl.BlockSpec((tm, BR_W), lambda i, j: (i, 0))] * N_BRANCH
        + [gspec(n) for n in range(N_BRANCH)]
        + [pl.BlockSpec((None, N_BRANCH, BR_W, tn), lambda i, j: (layer, 0, 0, j))],
        out_specs=pl.BlockSpec((tm, tn), lambda i, j: (i, j)),
        out_shape=jax.ShapeDtypeStruct((m, d), BF16),
        compiler_params=_cp(("parallel", "arbitrary"), 56),
        name="merge",
    )(h, *ogs, w_gate_t, w_gate_t, w_gate_t, w_gate_t, w_branch)


OUT_SUB = 2


def _out_body(*refs, has_next):
    if has_next:
        y_ref, w_ref, x_ref, gate_ref, g_ref, gn_ref, sh_ref, sc_ref, o_ref, h_ref = refs
    else:
        y_ref, w_ref, x_ref, gate_ref, g_ref, o_ref = refs
    ts = y_ref.shape[1]
    tsub = ts // OUT_SUB
    for sub in range(OUT_SUB):
        rows = slice(sub * tsub, (sub + 1) * tsub)
        t = _dot(y_ref[0, rows, :], w_ref[...])
        t = t * lax.rsqrt(jnp.mean(t * t, axis=-1, keepdims=True) + EPS) * g_ref[...]
        x = x_ref[0, rows, :] + gate_ref[0] * t
        o_ref[0, rows, :] = x
        if has_next:
            y = x * lax.rsqrt(jnp.mean(x * x, axis=-1, keepdims=True) + EPS)
            y = y * gn_ref[...]
            h_ref[0, rows, :] = (y * (1.0 + sc_ref[0]) + sh_ref[0]).astype(BF16)


def _out_proj(y, w_out, layer, x, gate, g, nxt):
    b, s, d = x.shape
    ts = min(s, 512)
    per_b = gate.shape[0] > 1
    cidx = (lambda bi, i: (bi, 0, 0)) if per_b else (lambda bi, i: (0, 0, 0))
    tile = pl.BlockSpec((1, ts, d), lambda bi, i: (bi, i, 0))
    vec = pl.BlockSpec((1, d), lambda bi, i: (0, 0))
    in_specs = [tile, pl.BlockSpec((None, d, d), lambda bi, i: (layer, 0, 0), pipeline_mode=pl.Buffered(1)),
                tile, pl.BlockSpec((1, 1, d), cidx), vec]
    args = [y, w_out, x, gate, g.reshape(1, d)]
    out_specs = [tile]
    out_shape = [jax.ShapeDtypeStruct((b, s, d), F32)]
    if nxt is not None:
        gn, shift, scale = nxt
        in_specs += [vec, pl.BlockSpec((1, 1, d), cidx), pl.BlockSpec((1, 1, d), cidx)]
        args += [gn.reshape(1, d), shift, scale]
        out_specs.append(tile)
        out_shape.append(jax.ShapeDtypeStruct((b, s, d), BF16))
    res = pl.pallas_call(
        functools.partial(_out_body, has_next=nxt is not None),
        grid=(b, s // ts),
        in_specs=in_specs,
        out_specs=out_specs,
        out_shape=out_shape,
        compiler_params=_cp(("parallel", "parallel"), 56),
        name="out_proj",
    )(*args)
    return (res[0], res[1]) if nxt is not None else (res[0], None)


_MIX_SRC = ((0, 512), (832, 4416), (4672, 6720), (512, 768), (4416, 4672), (768, 832))
_MIX_COLS = 6720


RELAYOUT_ROWS = 256
RELAYOUT_ALIGN = 64


def _relayout_body(off_ref, valid_ref, w_ref, o_ref):
    j = pl.program_id(1)
    row = lax.broadcasted_iota(jnp.int32, w_ref.shape, 0)
    o_ref[...] = jnp.where(row < valid_ref[j], w_ref[...], 0.0).astype(BF16)


def _relayout_rows(w_t, offs, valid):
    depth, _, d = w_t.shape
    nblk = len(offs)
    grid_spec = pltpu.PrefetchScalarGridSpec(
        num_scalar_prefetch=2,
        grid=(depth, nblk),
        in_specs=[pl.BlockSpec((pl.Element(1), pl.Element(RELAYOUT_ROWS), pl.Element(d)),
                               lambda l, j, off, val: (l, off[j] * RELAYOUT_ALIGN, 0))],
        out_specs=pl.BlockSpec((1, RELAYOUT_ROWS, d), lambda l, j, off, val: (l, j, 0)),
    )
    return pl.pallas_call(
        _relayout_body,
        grid_spec=grid_spec,
        out_shape=jax.ShapeDtypeStruct((depth, nblk * RELAYOUT_ROWS, d), BF16),
        compiler_params=_cp(("parallel", "arbitrary"), 32),
        name="relayout_w_in",
    )(jnp.asarray([o // RELAYOUT_ALIGN for o in offs], jnp.int32), jnp.asarray(valid, jnp.int32), w_t)


def _relayout_w_in(w_in):
    n = w_in.shape[2]
    w_t = jnp.swapaxes(w_in, 1, 2)
    r = RELAYOUT_ROWS
    offs, valid = [], []
    for a, b in _MIX_SRC:
        assert len(offs) * r == sum(valid)
        for o in range(a, b, r):
            offs.append(o)
            valid.append(min(r, b - o))
    while len(offs) * r < N_MIX:
        offs.append(0)
        valid.append(0)
    assert len(offs) * r == N_MIX and sum(valid) == _MIX_COLS
    w_mix_t = _relayout_rows(w_t, offs, valid)
    ng = (n - _MIX_COLS) // r
    w_gate_t = _relayout_rows(w_t, [_MIX_COLS + r * j for j in range(ng)], [r] * ng)
    return w_mix_t, w_gate_t


def _mla_weights(w_q_up, w_kv_up):
    depth = w_q_up.shape[0]
    wq = w_q_up.reshape(depth, MLA_Q_LORA, MLA_H, MLA_NOPE + MLA_ROPE)
    wq = jnp.pad(wq, ((0, 0), (0, 0), (0, 0), (0, MLA_DQ - MLA_NOPE - MLA_ROPE)))
    wq = wq.reshape(depth, MLA_Q_LORA, MLA_H * MLA_DQ).astype(BF16)
    wkv = w_kv_up.reshape(depth, MLA_KV_LORA, MLA_H, MLA_NOPE + MLA_V)
    wk = jnp.pad(wkv[..., :MLA_NOPE], ((0, 0), (0, 0), (0, 0), (0, MLA_DQ - MLA_NOPE)))
    wk = wk.reshape(depth, MLA_KV_LORA, MLA_H * MLA_DQ).astype(BF16)
    wv = jnp.pad(wkv[..., MLA_NOPE:], ((0, 0), (0, 0), (0, 0), (0, MLA_DV - MLA_V)))
    wv = wv.reshape(depth, MLA_KV_LORA, MLA_H * MLA_DV).astype(BF16)
    j = jnp.arange(128)[:, None]
    cidx = jnp.arange(MLA_H * MLA_DQ)[None, :]
    p128 = ((cidx % MLA_DQ == MLA_NOPE + j) & (j < MLA_ROPE)).astype(BF16)
    vones = (jnp.arange(MLA_H * MLA_DV)[None, :] % MLA_DV == MLA_V).astype(F32)
    return wq, wk, wv, p128, vones


def _rope_tables(n_tok):
    pos = jnp.arange(n_tok)
    row = (pos // GRID_W).astype(F32)
    col = (pos % GRID_W).astype(F32)
    n_freq = MLA_ROPE // 4
    inv = ROPE_BASE ** (-jnp.arange(n_freq, dtype=F32) / n_freq)
    ang = jnp.concatenate([row[:, None] * inv[None], col[:, None] * inv[None]], axis=-1)
    cos, sin = jnp.cos(ang), jnp.sin(ang)
    c64 = jnp.concatenate([cos, cos], axis=-1)
    s64 = jnp.concatenate([-sin, sin], axis=-1)
    one, zero = jnp.ones_like(c64), jnp.zeros_like(c64)
    cq = jnp.concatenate([one, one, c64, one], axis=-1)
    sq = jnp.concatenate([zero, zero, s64, zero], axis=-1)
    ck = jnp.concatenate([c64, one], axis=-1)
    sk = jnp.concatenate([s64, zero], axis=-1)
    c2 = jnp.concatenate([c64, c64], axis=-1)
    s2 = jnp.concatenate([s64, s64], axis=-1)
    return (cq, sq, ck, sk), (c2, s2)


TM_IN = 2048
TN_IN = 768
TM_MERGE = 1024
TN_MERGE = 512


def kernel(x_prompt, x_sample, cache_mla_ckv, cache_mla_krope, state_ret, cache_nat_k, cache_nat_v,
           cache_swa_k, cache_swa_v, c, c_ctx, w_mod, b_mod, norm_pre, norm_post, w_in, mla_q_norm,
           mla_kv_norm, mla_w_q_up, mla_w_kv_up, ret_decay, ret_norm, nat_rpb, swa_sink, w_branch, w_out):
    depth = w_mod.shape[0]
    d = x_prompt.shape[-1]
    nb = x_sample.shape[0]
    s_lat = x_sample.shape[1]
    l_ctx = cache_mla_ckv.shape[2]

    w_mix, w_gate = _relayout_w_in(w_in)
    w_br = w_branch.astype(BF16)
    w_o = w_out.astype(BF16)
    wq, wk, wv, p128, vones = _mla_weights(mla_w_q_up, mla_w_kv_up)
    p64 = p128[:MLA_ROPE]
    mla_tabs, (cos2, sin2) = _rope_tables(s_lat)
    nat_bias = _nat_bias_tables(nat_rpb, s_lat // GRID_W)
    nat_k_c = cache_nat_k.reshape(nb, depth, l_ctx, NAT_H * NAT_D)
    nat_v_c = cache_nat_v.reshape(nb, depth, l_ctx, NAT_H * NAT_D)
    swa_k_c = cache_swa_k.reshape(nb, depth, l_ctx, SWA_KVH * SWA_D)
    swa_v_c = cache_swa_v.reshape(nb, depth, l_ctx, SWA_KVH * SWA_D)

    rows = 8
    conds = jnp.concatenate([c_ctx[None, :], c, jnp.zeros((rows - 1 - nb, d), F32)], axis=0)
    mod = _modulation(conds, w_mod, b_mod)

    def mod_parts(l, lo, hi):
        m = mod[l, lo:hi]
        return m[:, None, :d], m[:, None, d:2 * d], m[:, None, 2 * d:]

    def finish_layer(l, lo, hi, x, h, ogs):
        b, s, _ = x.shape
        gate = mod_parts(l, lo, hi)[2]
        y = _merge(h.reshape(b * s, d), [o.reshape(b * s, BR_W) for o in ogs], w_gate, w_br, l, TM_MERGE, TN_MERGE)
        nxt = None
        if l + 1 < depth:
            shift_n, scale_n, _ = mod_parts(l + 1, lo, hi)
            nxt = (norm_pre[l + 1], shift_n, scale_n)
        return _out_proj(y.reshape(b, s, d), w_o, l, x, gate, norm_post[l], nxt)

    def start_pass(x, lo, hi):
        shift, scale, _ = mod_parts(0, lo, hi)
        return _modnorm(x, norm_pre[0], shift, scale)

    def in_proj(x, h, l):
        b, s, _ = x.shape
        return _in_proj(h.reshape(b * s, d), w_mix, l, TM_IN, TN_IN).reshape(b, s, N_MIX)

    yp = x_prompt
    bc, sc, _ = yp.shape
    ckv_l, kr_l, st_l, nk_l, nv_l, sk_l, sv_l = [], [], [], [], [], [], []
    h = start_pass(yp, 0, 1)
    for l in range(depth):
        z = in_proj(yp, h, l)
        q, kf, vf, ckv = _mla_prep(z, mla_q_norm[l], mla_kv_norm[l], wq[l], wk[l], wv[l], p128, vones, None)
        og_mla, og_nat, og_swa = _ctx_attn(q, kf, vf, z, swa_sink[l])
        og_ret, st = _retention(z, ret_decay[l], ret_norm[l], None, 0, True)
        yp, h = finish_layer(l, 0, 1, yp, h, (og_mla, og_ret, og_nat, og_swa))
        ckv_l.append(ckv)
        kr_l.append(z[:, :, OFF_KR:OFF_KR + MLA_ROPE])
        st_l.append(st)
        nk_l.append(z[:, :, OFF_NK:OFF_NK + 512].reshape(bc, sc, NAT_H, NAT_D))
        nv_l.append(z[:, :, OFF_NV:OFF_NV + 512].reshape(bc, sc, NAT_H, NAT_D))
        sk_l.append(z[:, :, OFF_SK:OFF_SK + 128].reshape(bc, sc, SWA_KVH, SWA_D))
        sv_l.append(z[:, :, OFF_SV:OFF_SV + 128].reshape(bc, sc, SWA_KVH, SWA_D))

    ys = x_sample
    h = start_pass(ys, 1, 1 + nb)
    for l in range(depth):
        z = in_proj(ys, h, l)
        q, kf, vf, _ = _mla_prep(z, mla_q_norm[l], mla_kv_norm[l], wq[l], wk[l], wv[l], p128, vones, mla_tabs)
        kc, vc = _mla_expand(cache_mla_ckv, cache_mla_krope, l, wk[l], wv[l], p64, vones)
        og_mla = _mla_attn(q, kf, vf, z, OFF_GP // BR_W, ctx=(kc, vc))
        og_ret, _ = _retention(z, ret_decay[l], ret_norm[l], state_ret, l, False)
        og_nat = _nat(z, nat_k_c, nat_v_c, l, nat_bias)
        og_swa = _swa(z, swa_k_c, swa_v_c, l, swa_sink[l], cos2, sin2)
        ys, h = finish_layer(l, 1, 1 + nb, ys, h, (og_mla, og_ret, og_nat, og_swa))

    return (yp, ys, jnp.stack(ckv_l, axis=1), jnp.stack(kr_l, axis=1), jnp.stack(st_l, axis=1),
            jnp.stack(nk_l, axis=1), jnp.stack(nv_l, axis=1), jnp.stack(sk_l, axis=1), jnp.stack(sv_l, axis=1))
```

```python
import functools

import jax
import jax.numpy as jnp
from jax import lax
from jax.experimental import pallas as pl
from jax.experimental.pallas import tpu as pltpu

F32 = jnp.float32
BF16 = jnp.bfloat16

GRID_W = 64
BLK = 128
ROPE_BASE = 10000.0
EPS = 1e-6
NEG_INF = -1e30
N_BRANCH = 4
BR_W = 512

MLA_H = 4
MLA_Q_LORA = 512
MLA_KV_LORA = 256
MLA_NOPE = 128
MLA_ROPE = 64
MLA_V = 128
MLA_DQ = 256
MLA_DV = 256
MLA_QSCALE = (MLA_NOPE + MLA_ROPE) ** -0.5 * 1.4426950408889634

RET_H = 4
RET_DK = 128
RET_DV = 128
RET_HPS = 2

NAT_H = 4
NAT_D = 128
NAT_ROWS = 8
NAT_COLS = 16
NAT_QR = 4
NAT_WR = NAT_ROWS + NAT_QR

SWA_H = 8
SWA_KVH = 2
SWA_D = 64
SWA_WINDOW = 128

OFF_QA = 0
OFF_RQ = 512
OFF_RK = 1024
OFF_RV = 1536
OFF_NQ = 2048
OFF_NK = 2560
OFF_NV = 3072
OFF_SQ = 3584
OFF_GP = 4096
OFF_KVA = 6144
OFF_SK = 6400
OFF_SV = 6528
OFF_KR = 6656
N_MIX_USED = 6784
N_MIX = 6912

V7X_VMEM_BYTES = 64 * 1024 * 1024


def _cp(sem, vmem_mb):
    assert vmem_mb * 1024 * 1024 < V7X_VMEM_BYTES
    return pltpu.CompilerParams(dimension_semantics=sem, vmem_limit_bytes=vmem_mb * 1024 * 1024)


def _sigmoid(x):
    return 1.0 / (1.0 + jnp.exp(-x))


def _silu(x):
    return x * _sigmoid(x)


def _swap_halves(x):
    n = x.shape[-1]
    ax = x.ndim - 1
    lane = lax.broadcasted_iota(jnp.int32, x.shape, ax)
    return jnp.where((lane & 32) == 0, pltpu.roll(x, n - 32, ax), pltpu.roll(x, 32, ax))


def _dot_nt(a, b):
    return lax.dot_general(a, b, (((1,), (1,)), ((), ())), preferred_element_type=F32)


def _dot(a, b):
    return jnp.dot(a, b, preferred_element_type=F32)


def _mod_body(c_ref, w_ref, b_ref, o_ref):
    a = _silu(c_ref[...]).astype(BF16)
    o_ref[0] = _dot(a, w_ref[0].astype(BF16)) + b_ref[0]


def _modulation(conds, w_mod, b_mod):
    depth, d, n = w_mod.shape
    rows = conds.shape[0]
    tn = 1024 if n % 1024 == 0 else n
    return pl.pallas_call(
        _mod_body,
        grid=(depth, n // tn),
        in_specs=[pl.BlockSpec((rows, d), lambda l, j: (0, 0)),
                  pl.BlockSpec((1, d, tn), lambda l, j: (l, 0, j)),
                  pl.BlockSpec((1, 1, tn), lambda l, j: (l, 0, j))],
        out_specs=pl.BlockSpec((1, rows, tn), lambda l, j: (l, 0, j)),
        out_shape=jax.ShapeDtypeStruct((depth, rows, n), F32),
        compiler_params=_cp(("parallel", "parallel"), 40),
        name="modulation",
    )(conds, w_mod, b_mod.reshape(depth, 1, n))


def _modnorm_body(x_ref, g_ref, sh_ref, sc_ref, h_ref):
    x = x_ref[0]
    y = x * lax.rsqrt(jnp.mean(x * x, axis=-1, keepdims=True) + EPS)
    y = y * g_ref[...]
    h_ref[0] = (y * (1.0 + sc_ref[0]) + sh_ref[0]).astype(BF16)


def _modnorm(x, g, shift, scale):
    b, s, d = x.shape
    ts = min(s, 512)
    per_b = shift.shape[0] > 1
    cidx = (lambda bi, i: (bi, 0, 0)) if per_b else (lambda bi, i: (0, 0, 0))
    return pl.pallas_call(
        _modnorm_body,
        grid=(b, s // ts),
        in_specs=[pl.BlockSpec((1, ts, d), lambda bi, i: (bi, i, 0)),
                  pl.BlockSpec((1, d), lambda bi, i: (0, 0)),
                  pl.BlockSpec((1, 1, d), cidx),
                  pl.BlockSpec((1, 1, d), cidx)],
        out_specs=pl.BlockSpec((1, ts, d), lambda bi, i: (bi, i, 0)),
        out_shape=jax.ShapeDtypeStruct((b, s, d), BF16),
        compiler_params=_cp(("parallel", "parallel"), 32),
        name="modnorm",
    )(x, g.reshape(1, d), shift, scale)


def _mm_body(a_ref, b_ref, o_ref):
    o_ref[...] = _dot_nt(a_ref[...], b_ref[...])


def _in_proj(a, w_mix_t, layer, tm, tn):
    m, k = a.shape
    tm = min(tm, m)
    return pl.pallas_call(
        _mm_body,
        grid=(m // tm, N_MIX // tn),
        in_specs=[pl.BlockSpec((tm, k), lambda i, j: (i, 0)),
                  pl.BlockSpec((None, tn, k), lambda i, j: (layer, j, 0))],
        out_specs=pl.BlockSpec((tm, tn), lambda i, j: (i, j)),
        out_shape=jax.ShapeDtypeStruct((m, N_MIX), F32),
        compiler_params=_cp(("parallel", "arbitrary"), 48),
        name="in_proj",
    )(a, w_mix_t)


def _mla_prep_body(*refs, rope):
    if rope:
        (qa_ref, kva_ref, kr_ref, gq_ref, gkv_ref, wq_ref, wk_ref, wv_ref, p_ref, one_ref,
         cq_ref, sq_ref, ck_ref, sk_ref, q_out, k_out, v_out, ckv_out) = refs
    else:
        (qa_ref, kva_ref, kr_ref, gq_ref, gkv_ref, wq_ref, wk_ref, wv_ref, p_ref, one_ref,
         q_out, k_out, v_out, ckv_out) = refs
    qa = qa_ref[0]
    cq = qa * lax.rsqrt(jnp.mean(qa * qa, axis=-1, keepdims=True) + EPS) * gq_ref[...]
    q = _dot(cq.astype(BF16), wq_ref[...])
    kva = kva_ref[0]
    ckv = kva * lax.rsqrt(jnp.mean(kva * kva, axis=-1, keepdims=True) + EPS) * gkv_ref[...]
    ckv_out[0] = ckv
    kr = kr_ref[0]
    if rope:
        cq_t = jnp.concatenate([cq_ref[...]] * MLA_H, axis=-1)
        sq_t = jnp.concatenate([sq_ref[...]] * MLA_H, axis=-1)
        q = q * cq_t + _swap_halves(q) * sq_t
        kr = kr * ck_ref[...] + _swap_halves(kr) * sk_ref[...]
    q_out[0] = (q * MLA_QSCALE).astype(BF16)
    cb = ckv.astype(BF16)
    k_out[0] = (_dot(cb, wk_ref[...]) + _dot(kr.astype(BF16), p_ref[...])).astype(BF16)
    v_out[0] = (_dot(cb, wv_ref[...]) + one_ref[...]).astype(BF16)


def _mla_prep(z, gq, gkv, wq, wk, wv, p128, vones, tabs):
    b, s, _ = z.shape
    tm = min(s, 512)
    rope = tabs is not None
    full = lambda shape: pl.BlockSpec(shape, lambda bi, i: (0,) * len(shape))
    in_specs = [pl.BlockSpec((1, tm, MLA_Q_LORA), lambda bi, i: (bi, i, OFF_QA // MLA_Q_LORA)),
                pl.BlockSpec((1, tm, MLA_KV_LORA), lambda bi, i: (bi, i, OFF_KVA // MLA_KV_LORA)),
                pl.BlockSpec((1, tm, 128), lambda bi, i: (bi, i, OFF_KR // 128)),
                full((1, MLA_Q_LORA)), full((1, MLA_KV_LORA)),
                full(wq.shape), full(wk.shape), full(wv.shape), full(p128.shape), full(vones.shape)]
    args = [z, z, z, gq.reshape(1, -1), gkv.reshape(1, -1), wq, wk, wv, p128, vones]
    if rope:
        cq, sq, ck, sk = tabs
        in_specs += [pl.BlockSpec((tm, MLA_DQ), lambda bi, i: (i, 0))] * 2
        in_specs += [pl.BlockSpec((tm, 128), lambda bi, i: (i, 0))] * 2
        args += [cq, sq, ck, sk]
    hq = MLA_H * MLA_DQ
    hv = MLA_H * MLA_DV
    return pl.pallas_call(
        functools.partial(_mla_prep_body, rope=rope),
        grid=(b, s // tm),
        in_specs=in_specs,
        out_specs=[pl.BlockSpec((1, tm, hq), lambda bi, i: (bi, i, 0)),
                   pl.BlockSpec((1, tm, hq), lambda bi, i: (bi, i, 0)),
                   pl.BlockSpec((1, tm, hv), lambda bi, i: (bi, i, 0)),
                   pl.BlockSpec((1, tm, MLA_KV_LORA), lambda bi, i: (bi, i, 0))],
        out_shape=[jax.ShapeDtypeStruct((b, s, hq), BF16),
                   jax.ShapeDtypeStruct((b, s, hq), BF16),
                   jax.ShapeDtypeStruct((b, s, hv), BF16),
                   jax.ShapeDtypeStruct((b, s, MLA_KV_LORA), F32)],
        compiler_params=_cp(("parallel", "parallel"), 40),
        name="mla_prep",
    )(*args)


def _mla_expand_body(ckv_ref, kr_ref, wk_ref, wv_ref, p_ref, one_ref, k_out, v_out):
    cb = ckv_ref[0, 0].astype(BF16)
    k_out[0] = (_dot(cb, wk_ref[...]) + _dot(kr_ref[0, 0].astype(BF16), p_ref[...])).astype(BF16)
    v_out[0] = (_dot(cb, wv_ref[...]) + one_ref[...]).astype(BF16)


def _mla_expand(cache_ckv, cache_kr, layer, wk, wv, p64, vones):
    b, _, s, _ = cache_ckv.shape
    full = lambda shape: pl.BlockSpec(shape, lambda bi: (0,) * len(shape))
    hq = MLA_H * MLA_DQ
    hv = MLA_H * MLA_DV
    return pl.pallas_call(
        _mla_expand_body,
        grid=(b,),
        in_specs=[pl.BlockSpec((1, 1, s, MLA_KV_LORA), lambda bi: (bi, layer, 0, 0)),
                  pl.BlockSpec((1, 1, s, MLA_ROPE), lambda bi: (bi, layer, 0, 0)),
                  full(wk.shape), full(wv.shape), full(p64.shape), full(vones.shape)],
        out_specs=[pl.BlockSpec((1, s, hq), lambda bi: (bi, 0, 0)),
                   pl.BlockSpec((1, s, hv), lambda bi: (bi, 0, 0))],
        out_shape=[jax.ShapeDtypeStruct((b, s, hq), BF16),
                   jax.ShapeDtypeStruct((b, s, hv), BF16)],
        compiler_params=_cp(("parallel",), 32),
        name="mla_expand",
    )(cache_ckv, cache_kr, wk, wv, p64, vones)


def _mla_attn_body(*refs, has_ctx, nkv):
    it = iter(refs)
    q_ref, k_ref, v_ref = next(it), next(it), next(it)
    kc_ref = vc_ref = None
    if has_ctx:
        kc_ref, vc_ref = next(it), next(it)
    gp_ref, o_ref, m_scr, acc_scr = next(it), next(it), next(it), next(it)
    j = pl.program_id(2)

    def process(kr, vr):
        tk = kr.shape[1]
        for h in range(MLA_H):
            qk = slice(h * MLA_DQ, (h + 1) * MLA_DQ)
            s = _dot_nt(q_ref[0, :, qk], kr[0, :, qk])
            m_prev = m_scr[h]
            m_new = jnp.maximum(m_prev, jnp.max(s, axis=-1, keepdims=True))
            alpha = jnp.exp2(m_prev - m_new)
            p = jnp.exp2(s - jnp.concatenate([m_new] * (tk // 128), axis=-1))
            pv = _dot(p.astype(BF16), vr[0, :, h * MLA_DV:(h + 1) * MLA_DV])
            acc_scr[h] = jnp.concatenate([alpha] * (MLA_DV // 128), axis=-1) * acc_scr[h] + pv
            m_scr[h] = m_new

    @pl.when(j == 0)
    def _():
        m_scr[...] = jnp.full(m_scr.shape, NEG_INF, F32)
        acc_scr[...] = jnp.zeros(acc_scr.shape, F32)
        if has_ctx:
            process(kc_ref, vc_ref)

    process(k_ref, v_ref)

    @pl.when(j == nkv - 1)
    def _():
        outs = []
        for h in range(MLA_H):
            a = acc_scr[h]
            outs.append(a[:, :MLA_V] / a[:, MLA_V:MLA_V + 1])
        o_ref[0] = (jnp.concatenate(outs, axis=-1) * _silu(gp_ref[0])).astype(BF16)


def _mla_attn(q, k, v, gp, gpcol, ctx=None, tq=1024, tk=1024):
    b, sq, _ = q.shape
    sk = k.shape[1]
    tq = min(tq, sq)
    tk = min(tk, sk)
    nkv = sk // tk
    hq = MLA_H * MLA_DQ
    hv = MLA_H * MLA_DV
    in_specs = [pl.BlockSpec((1, tq, hq), lambda bi, i, j: (bi, i, 0)),
                pl.BlockSpec((1, tk, hq), lambda bi, i, j: (bi, j, 0)),
                pl.BlockSpec((1, tk, hv), lambda bi, i, j: (bi, j, 0))]
    args = [q, k, v]
    if ctx is not None:
        kc, vc = ctx
        in_specs += [pl.BlockSpec((1,) + kc.shape[1:], lambda bi, i, j: (bi, 0, 0)),
                     pl.BlockSpec((1,) + vc.shape[1:], lambda bi, i, j: (bi, 0, 0))]
        args += [kc, vc]
    in_specs += [pl.BlockSpec((1, tq, MLA_H * MLA_V), lambda bi, i, j: (bi, i, gpcol))]
    args += [gp]
    return pl.pallas_call(
        functools.partial(_mla_attn_body, has_ctx=ctx is not None, nkv=nkv),
        grid=(b, sq // tq, nkv),
        in_specs=in_specs,
        out_specs=pl.BlockSpec((1, tq, MLA_H * MLA_V), lambda bi, i, j: (bi, i, 0)),
        out_shape=jax.ShapeDtypeStruct((b, sq, MLA_H * MLA_V), BF16),
        scratch_shapes=[pltpu.VMEM((MLA_H, tq, 128), F32), pltpu.VMEM((MLA_H, tq, MLA_DV), F32)],
        compiler_params=_cp(("parallel", "parallel", "arbitrary"), 48),
        name="mla_attn",
    )(*args)


def _ctx_attn_body(qm_ref, km_ref, vm_ref, nq_ref, nk_ref, nv_ref, sq_ref, sk_ref, sv_ref, sink_ref,
                   gpm_ref, gpn_ref, gps_ref, om_ref, on_ref, os_ref):
    outs = []
    for h in range(MLA_H):
        qk = slice(h * MLA_DQ, (h + 1) * MLA_DQ)
        s = _dot_nt(qm_ref[0, :, qk], km_ref[0, :, qk])
        p = jnp.exp2(s - jnp.max(s, axis=-1, keepdims=True))
        pv = _dot(p.astype(BF16), vm_ref[0, :, h * MLA_DV:(h + 1) * MLA_DV])
        outs.append(pv[:, :MLA_V] / pv[:, MLA_V:MLA_V + 1])
    om_ref[0] = (jnp.concatenate(outs, axis=-1) * _silu(gpm_ref[0])).astype(BF16)

    outs = []
    for h in range(NAT_H):
        sl = slice(h * NAT_D, (h + 1) * NAT_D)
        s = _dot_nt(nq_ref[0, :, sl].astype(BF16), nk_ref[0, :, sl].astype(BF16)) * NAT_D ** -0.5
        p = jnp.exp(s - jnp.max(s, axis=-1, keepdims=True))
        den = jnp.sum(p, axis=-1, keepdims=True)
        outs.append(_dot(p.astype(BF16), nv_ref[0, :, sl].astype(BF16)) / den)
    on_ref[0] = (jnp.concatenate(outs, axis=-1) * _silu(gpn_ref[0])).astype(BF16)

    n_tok = sq_ref.shape[1]
    group = SWA_H // SWA_KVH
    q = sq_ref[0] * SWA_D ** -0.5
    outs = []
    for g in range(SWA_KVH):
        heads = range(g * group, (g + 1) * group)
        qg = jnp.concatenate([q[:, h * SWA_D:(h + 1) * SWA_D] for h in heads], axis=0).astype(BF16)
        sink = jnp.concatenate([jnp.broadcast_to(sink_ref[0:1, h:h + 1], (n_tok, 1)) for h in heads], axis=0)
        ksl = slice(g * SWA_D, (g + 1) * SWA_D)
        s = _dot_nt(qg, sk_ref[0, :, ksl].astype(BF16))
        m = jnp.maximum(jnp.max(s, axis=-1, keepdims=True), sink)
        p = jnp.exp(s - m)
        den = jnp.sum(p, axis=-1, keepdims=True) + jnp.exp(sink - m)
        o = _dot(p.astype(BF16), sv_ref[0, :, ksl].astype(BF16)) / den
        outs += [o[t * n_tok:(t + 1) * n_tok] for t in range(group)]
    os_ref[0] = (jnp.concatenate(outs, axis=-1) * _silu(gps_ref[0])).astype(BF16)


def _ctx_attn(q, k, v, z, sink):
    b, n_tok, _ = z.shape
    blk = lambda w, off: pl.BlockSpec((1, n_tok, w), lambda bi: (bi, 0, off // w))
    kd = SWA_KVH * SWA_D
    out = jax.ShapeDtypeStruct((b, n_tok, BR_W), BF16)
    return pl.pallas_call(
        _ctx_attn_body,
        grid=(b,),
        in_specs=[blk(MLA_H * MLA_DQ, 0), blk(MLA_H * MLA_DQ, 0), blk(MLA_H * MLA_DV, 0),
                  blk(BR_W, OFF_NQ), blk(BR_W, OFF_NK), blk(BR_W, OFF_NV),
                  blk(BR_W, OFF_SQ), blk(kd, OFF_SK), blk(kd, OFF_SV),
                  pl.BlockSpec((1, SWA_H), lambda bi: (0, 0)),
                  blk(BR_W, OFF_GP), blk(BR_W, OFF_GP + 2 * BR_W), blk(BR_W, OFF_GP + 3 * BR_W)],
        out_specs=[blk(BR_W, 0)] * 3,
        out_shape=[out, out, out],
        compiler_params=_cp(("parallel",), 48),
        name="ctx_attn",
    )(q, k, v, z, z, z, z, z, z, sink.reshape(1, SWA_H), z, z, z)


def _ret_body(*refs, nc, has_s0, has_st):
    it = iter(refs)
    dec_ref, norm_ref, q_ref, k_ref, v_ref = next(it), next(it), next(it), next(it), next(it)
    s0_ref = next(it) if has_s0 else None
    gp_ref, o_ref = next(it), next(it)
    st_ref = next(it) if has_st else None
    o_scr, kv_scr = next(it), next(it)
    hgrp = pl.program_id(1)

    d = dec_ref[0]
    lg_all = jnp.minimum(d, 0.0) - jnp.log1p(jnp.exp(-jnp.abs(d)))
    hcol = lax.broadcasted_iota(jnp.int32, d.shape, 1)
    row = lax.broadcasted_iota(jnp.int32, (BLK, BLK), 0).astype(F32)
    col = lax.broadcasted_iota(jnp.int32, (BLK, BLK), 1).astype(F32)
    a = lax.broadcasted_iota(jnp.int32, (BLK, 1), 0).astype(F32)
    kscale = RET_DK ** -0.5

    heads = []
    for hh in range(RET_HPS):
        lg = jnp.sum(jnp.where(hcol == hgrp * RET_HPS + hh, lg_all, 0.0), axis=1, keepdims=True)
        lg_f, lg_b = lg[0:1, :], lg[1:2, :]
        dm = (jnp.where(row - col >= 0, jnp.exp(lg_f * jnp.maximum(row - col, 0.0)), 0.0)
              + jnp.where(col - row > 0, jnp.exp(lg_b * jnp.maximum(col - row, 0.0)), 0.0))
        fwd = (jnp.exp(lg_f * (a + 1.0)), jnp.exp(lg_f * (BLK - 1.0 - a)), jnp.exp(lg_f * BLK))
        bwd = (jnp.exp(lg_b * (BLK - a)), jnp.exp(lg_b * a), jnp.exp(lg_b * BLK))
        heads.append((hh, slice(hh * RET_DK, (hh + 1) * RET_DK), dm, fwd, bwd))

    def intra(c, _):
        off = pl.multiple_of(c * BLK, BLK)
        for hh, sl, dm, fwd, bwd in heads:
            q = q_ref[0, pl.ds(off, BLK), sl]
            k = k_ref[0, pl.ds(off, BLK), sl] * kscale
            v = v_ref[0, pl.ds(off, BLK), sl].astype(BF16)
            inner = _dot_nt(q.astype(BF16), k.astype(BF16)) * dm
            lhs = jnp.concatenate([inner.astype(BF16), jnp.transpose(k * fwd[1]).astype(BF16),
                                   jnp.transpose(k * bwd[1]).astype(BF16)], axis=0)
            r = _dot(lhs, v)
            o_scr[pl.ds(off, BLK), sl] = r[:BLK]
            kv_scr[hh, 0, c] = r[BLK:2 * BLK]
            kv_scr[hh, 1, c] = r[2 * BLK:]
        return 0

    lax.fori_loop(0, nc, intra, 0, unroll=2)

    def scan(j, carry):
        cf = j
        cb = nc - 1 - j
        off_f = pl.multiple_of(cf * BLK, BLK)
        off_b = pl.multiple_of(cb * BLK, BLK)
        new = []
        for (hh, sl, _, fwd, bwd), (sf, sb) in zip(heads, carry):
            qf = (q_ref[0, pl.ds(off_f, BLK), sl] * fwd[0]).astype(BF16)
            o_scr[pl.ds(off_f, BLK), sl] += _dot(qf, sf.astype(BF16))
            sf = sf * fwd[2] + kv_scr[hh, 0, cf]
            qb = (q_ref[0, pl.ds(off_b, BLK), sl] * bwd[0]).astype(BF16)
            o_scr[pl.ds(off_b, BLK), sl] += _dot(qb, sb.astype(BF16))
            sb = sb * bwd[2] + kv_scr[hh, 1, cb]
            new.append((sf, sb))
        return tuple(new)

    if has_s0:
        init = tuple((s0_ref[0, 0, 0, hh], s0_ref[0, 0, 1, hh]) for hh in range(RET_HPS))
    else:
        zero = jnp.zeros((RET_DK, RET_DV), F32)
        init = tuple((zero, zero) for _ in range(RET_HPS))
    fin = lax.fori_loop(0, nc, scan, init, unroll=2)
    if has_st:
        for hh in range(RET_HPS):
            st_ref[0, 0, hh] = fin[hh][0]
            st_ref[0, 1, hh] = fin[hh][1]
    for _, sl, _, _, _ in heads:
        o = o_scr[:, sl]
        o = o * lax.rsqrt(jnp.mean(o * o, axis=-1, keepdims=True) + EPS) * norm_ref[:, sl]
        o_ref[0, :, sl] = (o * _silu(gp_ref[0, :, sl])).astype(BF16)


def _retention(z, decay, norm, s0, layer, want_state):
    b, s, _ = z.shape
    nc = s // BLK
    w = RET_HPS * RET_DK
    colspec = lambda off: pl.BlockSpec((1, s, w), lambda bi, g: (bi, 0, off // w + g))
    in_specs = [pl.BlockSpec((1, 2, RET_H), lambda bi, g: (0, 0, 0)),
                pl.BlockSpec((1, w), lambda bi, g: (0, g)),
                colspec(OFF_RQ), colspec(OFF_RK), colspec(OFF_RV)]
    args = [decay.reshape(1, 2, RET_H), norm.reshape(1, -1), z, z, z]
    if s0 is not None:
        in_specs.append(pl.BlockSpec((1, 1, 2, RET_HPS, RET_DK, RET_DV), lambda bi, g: (bi, layer, 0, g, 0, 0)))
        args.append(s0)
    in_specs.append(colspec(OFF_GP + BR_W))
    args.append(z)
    out_specs = [pl.BlockSpec((1, s, w), lambda bi, g: (bi, 0, g))]
    out_shape = [jax.ShapeDtypeStruct((b, s, RET_H * RET_DV), BF16)]
    if want_state:
        out_specs.append(pl.BlockSpec((1, 2, RET_HPS, RET_DK, RET_DV), lambda bi, g: (bi, 0, g, 0, 0)))
        out_shape.append(jax.ShapeDtypeStruct((b, 2, RET_H, RET_DK, RET_DV), F32))
    res = pl.pallas_call(
        functools.partial(_ret_body, nc=nc, has_s0=s0 is not None, has_st=want_state),
        grid=(b, RET_H // RET_HPS),
        in_specs=in_specs,
        out_specs=out_specs,
        out_shape=out_shape,
        scratch_shapes=[pltpu.VMEM((s, w), F32), pltpu.VMEM((RET_HPS, 2, nc, RET_DK, RET_DV), F32)],
        compiler_params=_cp(("parallel", "parallel"), 56),
        name="retention",
    )(*args)
    return (res[0], res[1]) if want_state else (res[0], None)


def _nat_body(q_ref, k_ref, v_ref, kc_ref, vc_ref, bias_ref, gp_ref, o_ref, kb_scr, vb_scr, o_scr, *, rows):
    scale = NAT_D ** -0.5
    kb_scr[...] = k_ref[0].astype(BF16)
    vb_scr[...] = v_ref[0].astype(BF16)
    kc = kc_ref[0, 0].astype(BF16)
    vc = vc_ref[0, 0].astype(BF16)
    nq = NAT_QR * GRID_W
    win = NAT_WR * GRID_W
    nblk = rows // NAT_QR

    def step(t, _):
        r0 = t * NAT_QR
        ws = jnp.clip(r0 - NAT_ROWS // 2, 0, rows - NAT_WR)
        pat = jnp.where(t == 0, 0, jnp.where(t == nblk - 1, 2, 1))
        qoff = pl.multiple_of(r0 * GRID_W, nq)
        koff = pl.multiple_of(ws * GRID_W, GRID_W)
        q = q_ref[0, pl.ds(qoff, nq), :].astype(BF16)
        kw = kb_scr[pl.ds(koff, win), :]
        vw = vb_scr[pl.ds(koff, win), :]
        s_loc = _dot_nt(q, kw) * scale + bias_ref[0, 0, pat]
        s_ctx = _dot_nt(q, kc) * scale
        m = jnp.maximum(jnp.max(s_loc, axis=-1, keepdims=True), jnp.max(s_ctx, axis=-1, keepdims=True))
        p_loc = jnp.exp(s_loc - m)
        p_ctx = jnp.exp(s_ctx - m)
        den = jnp.sum(p_loc, axis=-1, keepdims=True) + jnp.sum(p_ctx, axis=-1, keepdims=True)
        o = _dot(p_loc.astype(BF16), vw) + _dot(p_ctx.astype(BF16), vc)
        o_scr[pl.ds(qoff, nq), :] = o / den
        return 0

    lax.fori_loop(0, nblk, step, 0, unroll=2)
    o_ref[0] = (o_scr[...] * _silu(gp_ref[0])).astype(BF16)


def _nat_bias_tables(rpb, rows):
    depth, n_h = rpb.shape[:2]
    c = jnp.arange(GRID_W)[:, None]
    kc = jnp.arange(GRID_W)[None, :]
    cs = jnp.clip(c - NAT_COLS // 2, 0, GRID_W - NAT_COLS)
    valid = (kc >= cs) & (kc < cs + NAT_COLS)
    onehot = ((kc - c + (NAT_COLS - 1))[:, :, None] == jnp.arange(2 * NAT_COLS - 1)[None, None, :]).astype(F32)
    toep = jnp.einsum("dhrj,ckj->dhrck", rpb.astype(F32), onehot, precision=lax.Precision.HIGHEST)
    toep = jnp.where(valid, toep, NEG_INF)
    neg = jnp.full((depth, n_h, GRID_W, GRID_W), NEG_INF, F32)
    pats = []
    for r0, ws in ((0, 0), (NAT_QR, 0), (rows - NAT_QR, rows - NAT_WR)):
        qrows = []
        for qi in range(NAT_QR):
            r = r0 + qi
            rs = min(max(r - NAT_ROWS // 2, 0), rows - NAT_ROWS)
            blocks = []
            for wm in range(NAT_WR):
                kr = ws + wm
                blocks.append(toep[:, :, kr - r + NAT_ROWS - 1] if rs <= kr < rs + NAT_ROWS else neg)
            qrows.append(jnp.concatenate(blocks, axis=-1))
        pats.append(jnp.concatenate(qrows, axis=-2))
    return jnp.stack(pats, axis=2)


def _nat(z, cache_k, cache_v, layer, bias):
    b, s, _ = z.shape
    rows = s // GRID_W
    assert rows % NAT_QR == 0 and rows >= 2 * NAT_WR - NAT_ROWS
    lc = cache_k.shape[2]
    colspec = lambda blk: pl.BlockSpec((1, s, NAT_D), lambda bi, h: (bi, 0, blk + h))
    cspec = pl.BlockSpec((1, 1, lc, NAT_D), lambda bi, h: (bi, layer, 0, h))
    return pl.pallas_call(
        functools.partial(_nat_body, rows=rows),
        grid=(b, NAT_H),
        in_specs=[colspec(OFF_NQ // 128), colspec(OFF_NK // 128), colspec(OFF_NV // 128), cspec, cspec,
                  pl.BlockSpec((1, 1) + bias.shape[2:], lambda bi, h: (layer, h, 0, 0, 0)),
                  colspec((OFF_GP + 2 * BR_W) // 128)],
        out_specs=pl.BlockSpec((1, s, NAT_D), lambda bi, h: (bi, 0, h)),
        out_shape=jax.ShapeDtypeStruct((b, s, NAT_H * NAT_D), BF16),
        scratch_shapes=[pltpu.VMEM((s, NAT_D), BF16), pltpu.VMEM((s, NAT_D), BF16),
                        pltpu.VMEM((s, NAT_D), F32)],
        compiler_params=_cp(("parallel", "parallel"), 48),
        name="nat_attn",
    )(z, z, z, cache_k, cache_v, bias, z)


def _swa_values_with_ones(v):
    lane = lax.broadcasted_iota(jnp.int32, v.shape, 1)
    halves = [jnp.where(lane < SWA_D, v if g == 0 else pltpu.roll(v, (SWA_KVH - g) * SWA_D, 1), 1.0)
              for g in range(SWA_KVH)]
    return jnp.concatenate(halves, axis=-1).astype(BF16)


def _swa_body(q_ref, k_ref, v_ref, kc_ref, vc_ref, sink_ref, cos_ref, sin_ref, gp_ref, o_ref,
              kr_scr, vl_scr, kc_scr, vc_scr, *, seq):
    scale = SWA_D ** -0.5
    group = SWA_H // SWA_KVH
    win = 3 * BLK
    i = pl.program_id(1)

    @pl.when(i == 0)
    def _():
        k = k_ref[0]
        kr_scr[...] = (k * cos_ref[...] + _swap_halves(k) * sin_ref[...]).astype(BF16)
        vl_scr[...] = _swa_values_with_ones(v_ref[0])
        kc_scr[...] = kc_ref[0, 0].astype(BF16)
        vc_scr[...] = _swa_values_with_ones(vc_ref[0, 0])

    start = pl.multiple_of(jnp.clip((i - 1) * BLK, 0, seq - win), BLK)
    qoff = pl.multiple_of(i * BLK, BLK)
    cq = cos_ref[pl.ds(qoff, BLK), :]
    sq = sin_ref[pl.ds(qoff, BLK), :]
    q = q_ref[0]
    q = (q * jnp.concatenate([cq] * (SWA_H // 2), axis=-1)
         + _swap_halves(q) * jnp.concatenate([sq] * (SWA_H // 2), axis=-1))
    q = q * scale
    kw = kr_scr[pl.ds(start, win), :]
    vw = vl_scr[pl.ds(start, win), :]

    qpos = qoff + lax.broadcasted_iota(jnp.int32, (BLK, win), 0)
    kpos = start + lax.broadcasted_iota(jnp.int32, (BLK, win), 1)
    valid = jnp.abs(qpos - kpos) <= SWA_WINDOW
    valid = jnp.concatenate([valid] * group, axis=0)

    outs = []
    for g in range(SWA_KVH):
        heads = range(g * group, (g + 1) * group)
        qg = jnp.concatenate([q[:, h * SWA_D:(h + 1) * SWA_D] for h in heads], axis=0).astype(BF16)
        sink = jnp.concatenate([jnp.broadcast_to(sink_ref[0:1, h:h + 1], (BLK, 1)) for h in heads], axis=0)
        ksl = slice(g * SWA_D, (g + 1) * SWA_D)
        vsl = slice(g * 2 * SWA_D, (g + 1) * 2 * SWA_D)
        s = jnp.concatenate([jnp.where(valid, _dot_nt(qg, kw[:, ksl]), NEG_INF),
                             _dot_nt(qg, kc_scr[:, ksl])], axis=-1)
        m = jnp.maximum(jnp.max(s, axis=-1, keepdims=True), sink)
        p = jnp.exp(s - m).astype(BF16)
        pv = _dot(p[:, :win], vw[:, vsl]) + _dot(p[:, win:], vc_scr[:, vsl])
        o = pv[:, :SWA_D] / (pv[:, SWA_D:SWA_D + 1] + jnp.exp(sink - m))
        outs += [o[t * BLK:(t + 1) * BLK] for t in range(group)]
    o = jnp.concatenate(outs, axis=-1)
    o_ref[0] = (o * _silu(gp_ref[0])).astype(BF16)


def _swa(z, cache_k, cache_v, layer, sink, cos2, sin2):
    b, s, _ = z.shape
    lc = cache_k.shape[2]
    hd = SWA_H * SWA_D
    kd = SWA_KVH * SWA_D
    cspec = pl.BlockSpec((1, 1, lc, kd), lambda bi, i: (bi, layer, 0, 0))
    tq = BLK
    return pl.pallas_call(
        functools.partial(_swa_body, seq=s),
        grid=(b, s // tq),
        in_specs=[pl.BlockSpec((1, tq, hd), lambda bi, i: (bi, i, OFF_SQ // hd)),
                  pl.BlockSpec((1, s, kd), lambda bi, i: (bi, 0, OFF_SK // kd)),
                  pl.BlockSpec((1, s, kd), lambda bi, i: (bi, 0, OFF_SV // kd)),
                  cspec, cspec,
                  pl.BlockSpec((1, SWA_H), lambda bi, i: (0, 0)),
                  pl.BlockSpec((s, 128), lambda bi, i: (0, 0)),
                  pl.BlockSpec((s, 128), lambda bi, i: (0, 0)),
                  pl.BlockSpec((1, tq, hd), lambda bi, i: (bi, i, (OFF_GP + 3 * BR_W) // hd))],
        out_specs=pl.BlockSpec((1, tq, hd), lambda bi, i: (bi, i, 0)),
        out_shape=jax.ShapeDtypeStruct((b, s, hd), BF16),
        scratch_shapes=[pltpu.VMEM((s, kd), BF16), pltpu.VMEM((s, 2 * kd), BF16),
                        pltpu.VMEM((lc, kd), BF16), pltpu.VMEM((lc, 2 * kd), BF16)],
        compiler_params=_cp(("parallel", "arbitrary"), 48),
        name="swa_attn",
    )(z, z, z, cache_k, cache_v, sink.reshape(1, SWA_H), cos2, sin2, z)


def _merge_body(h_ref, o0_ref, o1_ref, o2_ref, o3_ref, g0_ref, g1_ref, g2_ref, g3_ref, wb_ref, y_ref):
    h = h_ref[...]
    o_refs = (o0_ref, o1_ref, o2_ref, o3_ref)
    g_refs = (g0_ref, g1_ref, g2_ref, g3_ref)
    y = None
    for n in range(N_BRANCH):
        t = _sigmoid(_dot_nt(h, g_refs[n][...])) * _dot(o_refs[n][...], wb_ref[n])
        y = t if y is None else y + t
    y_ref[...] = y.astype(BF16)


def _merge(h, ogs, w_gate_t, w_branch, layer, tm, tn):
    m, d = h.shape
    tm = min(tm, m)
    tn = min(tn, d)
    nj = d // tn
    gspec = lambda n: pl.BlockSpec((None, tn, d), lambda i, j: (layer, n * nj + j, 0))
    return pl.pallas_call(
        _merge_body,
        grid=(m // tm, nj),
        in_specs=[pl.BlockSpec((tm, d), lambda i, j: (i, 0))]
        + [pl.BlockSpec((tm, BR_W), lambda i, j: (i, 0))] * N_BRANCH
        + [gspec(n) for n in range(N_BRANCH)]
        + [pl.BlockSpec((None, N_BRANCH, BR_W, tn), lambda i, j: (layer, 0, 0, j))],
        out_specs=pl.BlockSpec((tm, tn), lambda i, j: (i, j)),
        out_shape=jax.ShapeDtypeStruct((m, d), BF16),
        compiler_params=_cp(("parallel", "arbitrary"), 56),
        name="merge",
    )(h, *ogs, w_gate_t, w_gate_t, w_gate_t, w_gate_t, w_branch)


OUT_SUB = 2


def _out_body(*refs, has_next):
    if has_next:
        y_ref, w_ref, x_ref, gate_ref, g_ref, gn_ref, sh_ref, sc_ref, o_ref, h_ref = refs
    else:
        y_ref, w_ref, x_ref, gate_ref, g_ref, o_ref = refs
    ts = y_ref.shape[1]
    tsub = ts // OUT_SUB
    for sub in range(OUT_SUB):
        rows = slice(sub * tsub, (sub + 1) * tsub)
        t = _dot(y_ref[0, rows, :], w_ref[...])
        t = t * lax.rsqrt(jnp.mean(t * t, axis=-1, keepdims=True) + EPS) * g_ref[...]
        x = x_ref[0, rows, :] + gate_ref[0] * t
        o_ref[0, rows, :] = x
        if has_next:
            y = x * lax.rsqrt(jnp.mean(x * x, axis=-1, keepdims=True) + EPS)
            y = y * gn_ref[...]
            h_ref[0, rows, :] = (y * (1.0 + sc_ref[0]) + sh_ref[0]).astype(BF16)


def _out_proj(y, w_out, layer, x, gate, g, nxt):
    b, s, d = x.shape
    ts = min(s, 512)
    per_b = gate.shape[0] > 1
    cidx = (lambda bi, i: (bi, 0, 0)) if per_b else (lambda bi, i: (0, 0, 0))
    tile = pl.BlockSpec((1, ts, d), lambda bi, i: (bi, i, 0))
    vec = pl.BlockSpec((1, d), lambda bi, i: (0, 0))
    in_specs = [tile, pl.BlockSpec((None, d, d), lambda bi, i: (layer, 0, 0), pipeline_mode=pl.Buffered(1)),
                tile, pl.BlockSpec((1, 1, d), cidx), vec]
    args = [y, w_out, x, gate, g.reshape(1, d)]
    out_specs = [tile]
    out_shape = [jax.ShapeDtypeStruct((b, s, d), F32)]
    if nxt is not None:
        gn, shift, scale = nxt
        in_specs += [vec, pl.BlockSpec((1, 1, d), cidx), pl.BlockSpec((1, 1, d), cidx)]
        args += [gn.reshape(1, d), shift, scale]
        out_specs.append(tile)
        out_shape.append(jax.ShapeDtypeStruct((b, s, d), BF16))
    res = pl.pallas_call(
        functools.partial(_out_body, has_next=nxt is not None),
        grid=(b, s // ts),
        in_specs=in_specs,
        out_specs=out_specs,
        out_shape=out_shape,
        compiler_params=_cp(("parallel", "parallel"), 56),
        name="out_proj",
    )(*args)
    return (res[0], res[1]) if nxt is not None else (res[0], None)


_MIX_SRC = ((0, 512), (832, 4416), (4672, 6720), (512, 768), (4416, 4672), (768, 832))
_MIX_COLS = 6720


RELAYOUT_ROWS = 256
RELAYOUT_ALIGN = 64


def _relayout_body(off_ref, valid_ref, w_ref, o_ref):
    j = pl.program_id(1)
    row = lax.broadcasted_iota(jnp.int32, w_ref.shape, 0)
    o_ref[...] = jnp.where(row < valid_ref[j], w_ref[...], 0.0).astype(BF16)


def _relayout_rows(w_t, offs, valid):
    depth, _, d = w_t.shape
    nblk = len(offs)
    grid_spec = pltpu.PrefetchScalarGridSpec(
        num_scalar_prefetch=2,
        grid=(depth, nblk),
        in_specs=[pl.BlockSpec((pl.Element(1), pl.Element(RELAYOUT_ROWS), pl.Element(d)),
                               lambda l, j, off, val: (l, off[j] * RELAYOUT_ALIGN, 0))],
        out_specs=pl.BlockSpec((1, RELAYOUT_ROWS, d), lambda l, j, off, val: (l, j, 0)),
    )
    return pl.pallas_call(
        _relayout_body,
        grid_spec=grid_spec,
        out_shape=jax.ShapeDtypeStruct((depth, nblk * RELAYOUT_ROWS, d), BF16),
        compiler_params=_cp(("parallel", "arbitrary"), 32),
        name="relayout_w_in",
    )(jnp.asarray([o // RELAYOUT_ALIGN for o in offs], jnp.int32), jnp.asarray(valid, jnp.int32), w_t)


def _relayout_w_in(w_in):
    n = w_in.shape[2]
    w_t = jnp.swapaxes(w_in, 1, 2)
    r = RELAYOUT_ROWS
    offs, valid = [], []
    for a, b in _MIX_SRC:
        assert len(offs) * r == sum(valid)
        for o in range(a, b, r):
            offs.append(o)
            valid.append(min(r, b - o))
    while len(offs) * r < N_MIX:
        offs.append(0)
        valid.append(0)
    assert len(offs) * r == N_MIX and sum(valid) == _MIX_COLS
    w_mix_t = _relayout_rows(w_t, offs, valid)
    ng = (n - _MIX_COLS) // r
    w_gate_t = _relayout_rows(w_t, [_MIX_COLS + r * j for j in range(ng)], [r] * ng)
    return w_mix_t, w_gate_t


def _mla_weights(w_q_up, w_kv_up):
    depth = w_q_up.shape[0]
    wq = w_q_up.reshape(depth, MLA_Q_LORA, MLA_H, MLA_NOPE + MLA_ROPE)
    wq = jnp.pad(wq, ((0, 0), (0, 0), (0, 0), (0, MLA_DQ - MLA_NOPE - MLA_ROPE)))
    wq = wq.reshape(depth, MLA_Q_LORA, MLA_H * MLA_DQ).astype(BF16)
    wkv = w_kv_up.reshape(depth, MLA_KV_LORA, MLA_H, MLA_NOPE + MLA_V)
    wk = jnp.pad(wkv[..., :MLA_NOPE], ((0, 0), (0, 0), (0, 0), (0, MLA_DQ - MLA_NOPE)))
    wk = wk.reshape(depth, MLA_KV_LORA, MLA_H * MLA_DQ).astype(BF16)
    wv = jnp.pad(wkv[..., MLA_NOPE:], ((0, 0), (0, 0), (0, 0), (0, MLA_DV - MLA_V)))
    wv = wv.reshape(depth, MLA_KV_LORA, MLA_H * MLA_DV).astype(BF16)
    j = jnp.arange(128)[:, None]
    cidx = jnp.arange(MLA_H * MLA_DQ)[None, :]
    p128 = ((cidx % MLA_DQ == MLA_NOPE + j) & (j < MLA_ROPE)).astype(BF16)
    vones = (jnp.arange(MLA_H * MLA_DV)[None, :] % MLA_DV == MLA_V).astype(F32)
    return wq, wk, wv, p128, vones


def _rope_tables(n_tok):
    pos = jnp.arange(n_tok)
    row = (pos // GRID_W).astype(F32)
    col = (pos % GRID_W).astype(F32)
    n_freq = MLA_ROPE // 4
    inv = ROPE_BASE ** (-jnp.arange(n_freq, dtype=F32) / n_freq)
    ang = jnp.concatenate([row[:, None] * inv[None], col[:, None] * inv[None]], axis=-1)
    cos, sin = jnp.cos(ang), jnp.sin(ang)
    c64 = jnp.concatenate([cos, cos], axis=-1)
    s64 = jnp.concatenate([-sin, sin], axis=-1)
    one, zero = jnp.ones_like(c64), jnp.zeros_like(c64)
    cq = jnp.concatenate([one, one, c64, one], axis=-1)
    sq = jnp.concatenate([zero, zero, s64, zero], axis=-1)
    ck = jnp.concatenate([c64, one], axis=-1)
    sk = jnp.concatenate([s64, zero], axis=-1)
    c2 = jnp.concatenate([c64, c64], axis=-1)
    s2 = jnp.concatenate([s64, s64], axis=-1)
    return (cq, sq, ck, sk), (c2, s2)


TM_IN = 2048
TN_IN = 768
TM_MERGE = 1024
TN_MERGE = 512


def kernel(x_prompt, x_sample, cache_mla_ckv, cache_mla_krope, state_ret, cache_nat_k, cache_nat_v,
           cache_swa_k, cache_swa_v, c, c_ctx, w_mod, b_mod, norm_pre, norm_post, w_in, mla_q_norm,
           mla_kv_norm, mla_w_q_up, mla_w_kv_up, ret_decay, ret_norm, nat_rpb, swa_sink, w_branch, w_out):
    depth = w_mod.shape[0]
    d = x_prompt.shape[-1]
    nb = x_sample.shape[0]
    s_lat = x_sample.shape[1]
    l_ctx = cache_mla_ckv.shape[2]

    w_mix, w_gate = _relayout_w_in(w_in)
    w_br = w_branch.astype(BF16)
    w_o = w_out.astype(BF16)
    wq, wk, wv, p128, vones = _mla_weights(mla_w_q_up, mla_w_kv_up)
    p64 = p128[:MLA_ROPE]
    mla_tabs, (cos2, sin2) = _rope_tables(s_lat)
    nat_bias = _nat_bias_tables(nat_rpb, s_lat // GRID_W)
    nat_k_c = cache_nat_k.reshape(nb, depth, l_ctx, NAT_H * NAT_D)
    nat_v_c = cache_nat_v.reshape(nb, depth, l_ctx, NAT_H * NAT_D)
    swa_k_c = cache_swa_k.reshape(nb, depth, l_ctx, SWA_KVH * SWA_D)
    swa_v_c = cache_swa_v.reshape(nb, depth, l_ctx, SWA_KVH * SWA_D)

    rows = 8
    conds = jnp.concatenate([c_ctx[None, :], c, jnp.zeros((rows - 1 - nb, d), F32)], axis=0)
    mod = _modulation(conds, w_mod, b_mod)

    def mod_parts(l, lo, hi):
        m = mod[l, lo:hi]
        return m[:, None, :d], m[:, None, d:2 * d], m[:, None, 2 * d:]

    def finish_layer(l, lo, hi, x, h, ogs):
        b, s, _ = x.shape
        gate = mod_parts(l, lo, hi)[2]
        y = _merge(h.reshape(b * s, d), [o.reshape(b * s, BR_W) for o in ogs], w_gate, w_br, l, TM_MERGE, TN_MERGE)
        nxt = None
        if l + 1 < depth:
            shift_n, scale_n, _ = mod_parts(l + 1, lo, hi)
            nxt = (norm_pre[l + 1], shift_n, scale_n)
        return _out_proj(y.reshape(b, s, d), w_o, l, x, gate, norm_post[l], nxt)

    def start_pass(x, lo, hi):
        shift, scale, _ = mod_parts(0, lo, hi)
        return _modnorm(x, norm_pre[0], shift, scale)

    def in_proj(x, h, l):
        b, s, _ = x.shape
        return _in_proj(h.reshape(b * s, d), w_mix, l, TM_IN, TN_IN).reshape(b, s, N_MIX)

    yp = x_prompt
    bc, sc, _ = yp.shape
    ckv_l, kr_l, st_l, nk_l, nv_l, sk_l, sv_l = [], [], [], [], [], [], []
    h = start_pass(yp, 0, 1)
    for l in range(depth):
        z = in_proj(yp, h, l)
        q, kf, vf, ckv = _mla_prep(z, mla_q_norm[l], mla_kv_norm[l], wq[l], wk[l], wv[l], p128, vones, None)
        og_mla, og_nat, og_swa = _ctx_attn(q, kf, vf, z, swa_sink[l])
        og_ret, st = _retention(z, ret_decay[l], ret_norm[l], None, 0, True)
        yp, h = finish_layer(l, 0, 1, yp, h, (og_mla, og_ret, og_nat, og_swa))
        ckv_l.append(ckv)
        kr_l.append(z[:, :, OFF_KR:OFF_KR + MLA_ROPE])
        st_l.append(st)
        nk_l.append(z[:, :, OFF_NK:OFF_NK + 512].reshape(bc, sc, NAT_H, NAT_D))
        nv_l.append(z[:, :, OFF_NV:OFF_NV + 512].reshape(bc, sc, NAT_H, NAT_D))
        sk_l.append(z[:, :, OFF_SK:OFF_SK + 128].reshape(bc, sc, SWA_KVH, SWA_D))
        sv_l.append(z[:, :, OFF_SV:OFF_SV + 128].reshape(bc, sc, SWA_KVH, SWA_D))

    ys = x_sample
    h = start_pass(ys, 1, 1 + nb)
    for l in range(depth):
        z = in_proj(ys, h, l)
        q, kf, vf, _ = _mla_prep(z, mla_q_norm[l], mla_kv_norm[l], wq[l], wk[l], wv[l], p128, vones, mla_tabs)
        kc, vc = _mla_expand(cache_mla_ckv, cache_mla_krope, l, wk[l], wv[l], p64, vones)
        og_mla = _mla_attn(q, kf, vf, z, OFF_GP // BR_W, ctx=(kc, vc))
        og_ret, _ = _retention(z, ret_decay[l], ret_norm[l], state_ret, l, False)
        og_nat = _nat(z, nat_k_c, nat_v_c, l, nat_bias)
        og_swa = _swa(z, swa_k_c, swa_v_c, l, swa_sink[l], cos2, sin2)
        ys, h = finish_layer(l, 1, 1 + nb, ys, h, (og_mla, og_ret, og_nat, og_swa))

    return (yp, ys, jnp.stack(ckv_l, axis=1), jnp.stack(kr_l, axis=1), jnp.stack(st_l, axis=1),
            jnp.stack(nk_l, axis=1), jnp.stack(nv_l, axis=1), jnp.stack(sk_l, axis=1), jnp.stack(sv_l, axis=1))
```

```python
import functools

import jax
import jax.numpy as jnp
from jax import lax
from jax.experimental import pallas as pl
from jax.experimental.pallas import tpu as pltpu

F32 = jnp.float32
BF16 = jnp.bfloat16

GRID_W = 64
BLK = 128
ROPE_BASE = 10000.0
EPS = 1e-6
NEG_INF = -1e30
N_BRANCH = 4
BR_W = 512

MLA_H = 4
MLA_Q_LORA = 512
MLA_KV_LORA = 256
MLA_NOPE = 128
MLA_ROPE = 64
MLA_V = 128
MLA_DQ = 256
MLA_DV = 256
MLA_QSCALE = (MLA_NOPE + MLA_ROPE) ** -0.5 * 1.4426950408889634

RET_H = 4
RET_DK = 128
RET_DV = 128
RET_HPS = 2

NAT_H = 4
NAT_D = 128
NAT_ROWS = 8
NAT_COLS = 16
NAT_QR = 4
NAT_WR = NAT_ROWS + NAT_QR

SWA_H = 8
SWA_KVH = 2
SWA_D = 64
SWA_WINDOW = 128

OFF_QA = 0
OFF_RQ = 512
OFF_RK = 1024
OFF_RV = 1536
OFF_NQ = 2048
OFF_NK = 2560
OFF_NV = 3072
OFF_SQ = 3584
OFF_GP = 4096
OFF_KVA = 6144
OFF_SK = 6400
OFF_SV = 6528
OFF_KR = 6656
N_MIX_USED = 6784
N_MIX = 6912

V7X_VMEM_BYTES = 64 * 1024 * 1024


def _cp(sem, vmem_mb):
    assert vmem_mb * 1024 * 1024 < V7X_VMEM_BYTES
    return pltpu.CompilerParams(dimension_semantics=sem, vmem_limit_bytes=vmem_mb * 1024 * 1024)


def _sigmoid(x):
    return 1.0 / (1.0 + jnp.exp(-x))


def _silu(x):
    return x * _sigmoid(x)


def _swap_halves(x):
    n = x.shape[-1]
    ax = x.ndim - 1
    lane = lax.broadcasted_iota(jnp.int32, x.shape, ax)
    return jnp.where((lane & 32) == 0, pltpu.roll(x, n - 32, ax), pltpu.roll(x, 32, ax))


def _dot_nt(a, b):
    return lax.dot_general(a, b, (((1,), (1,)), ((), ())), preferred_element_type=F32)


def _dot(a, b):
    return jnp.dot(a, b, preferred_element_type=F32)


def _mod_body(c_ref, w_ref, b_ref, o_ref):
    a = _silu(c_ref[...]).astype(BF16)
    o_ref[0] = _dot(a, w_ref[0].astype(BF16)) + b_ref[0]


def _modulation(conds, w_mod, b_mod):
    depth, d, n = w_mod.shape
    rows = conds.shape[0]
    tn = 1024 if n % 1024 == 0 else n
    return pl.pallas_call(
        _mod_body,
        grid=(depth, n // tn),
        in_specs=[pl.BlockSpec((rows, d), lambda l, j: (0, 0)),
                  pl.BlockSpec((1, d, tn), lambda l, j: (l, 0, j)),
                  pl.BlockSpec((1, 1, tn), lambda l, j: (l, 0, j))],
        out_specs=pl.BlockSpec((1, rows, tn), lambda l, j: (l, 0, j)),
        out_shape=jax.ShapeDtypeStruct((depth, rows, n), F32),
        compiler_params=_cp(("parallel", "parallel"), 40),
        name="modulation",
    )(conds, w_mod, b_mod.reshape(depth, 1, n))


def _modnorm_body(x_ref, g_ref, sh_ref, sc_ref, h_ref):
    x = x_ref[0]
    y = x * lax.rsqrt(jnp.mean(x * x, axis=-1, keepdims=True) + EPS)
    y = y * g_ref[...]
    h_ref[0] = (y * (1.0 + sc_ref[0]) + sh_ref[0]).astype(BF16)


def _modnorm(x, g, shift, scale):
    b, s, d = x.shape
    ts = min(s, 512)
    per_b = shift.shape[0] > 1
    cidx = (lambda bi, i: (bi, 0, 0)) if per_b else (lambda bi, i: (0, 0, 0))
    return pl.pallas_call(
        _modnorm_body,
        grid=(b, s // ts),
        in_specs=[pl.BlockSpec((1, ts, d), lambda bi, i: (bi, i, 0)),
                  pl.BlockSpec((1, d), lambda bi, i: (0, 0)),
                  pl.BlockSpec((1, 1, d), cidx),
                  pl.BlockSpec((1, 1, d), cidx)],
        out_specs=pl.BlockSpec((1, ts, d), lambda bi, i: (bi, i, 0)),
        out_shape=jax.ShapeDtypeStruct((b, s, d), BF16),
        compiler_params=_cp(("parallel", "parallel"), 32),
        name="modnorm",
    )(x, g.reshape(1, d), shift, scale)


def _mm_body(a_ref, b_ref, o_ref):
    o_ref[...] = _dot_nt(a_ref[...], b_ref[...])


def _in_proj(a, w_mix_t, layer, tm, tn):
    m, k = a.shape
    tm = min(tm, m)
    return pl.pallas_call(
        _mm_body,
        grid=(m // tm, N_MIX // tn),
        in_specs=[pl.BlockSpec((tm, k), lambda i, j: (i, 0)),
                  pl.BlockSpec((None, tn, k), lambda i, j: (layer, j, 0))],
        out_specs=pl.BlockSpec((tm, tn), lambda i, j: (i, j)),
        out_shape=jax.ShapeDtypeStruct((m, N_MIX), F32),
        compiler_params=_cp(("parallel", "arbitrary"), 48),
        name="in_proj",
    )(a, w_mix_t)


def _mla_prep_body(*refs, rope):
    if rope:
        (qa_ref, kva_ref, kr_ref, gq_ref, gkv_ref, wq_ref, wk_ref, wv_ref, p_ref, one_ref,
         cq_ref, sq_ref, ck_ref, sk_ref, q_out, k_out, v_out, ckv_out) = refs
    else:
        (qa_ref, kva_ref, kr_ref, gq_ref, gkv_ref, wq_ref, wk_ref, wv_ref, p_ref, one_ref,
         q_out, k_out, v_out, ckv_out) = refs
    qa = qa_ref[0]
    cq = qa * lax.rsqrt(jnp.mean(qa * qa, axis=-1, keepdims=True) + EPS) * gq_ref[...]
    q = _dot(cq.astype(BF16), wq_ref[...])
    kva = kva_ref[0]
    ckv = kva * lax.rsqrt(jnp.mean(kva * kva, axis=-1, keepdims=True) + EPS) * gkv_ref[...]
    ckv_out[0] = ckv
    kr = kr_ref[0]
    if rope:
        cq_t = jnp.concatenate([cq_ref[...]] * MLA_H, axis=-1)
        sq_t = jnp.concatenate([sq_ref[...]] * MLA_H, axis=-1)
        q = q * cq_t + _swap_halves(q) * sq_t
        kr = kr * ck_ref[...] + _swap_halves(kr) * sk_ref[...]
    q_out[0] = (q * MLA_QSCALE).astype(BF16)
    cb = ckv.astype(BF16)
    k_out[0] = (_dot(cb, wk_ref[...]) + _dot(kr.astype(BF16), p_ref[...])).astype(BF16)
    v_out[0] = (_dot(cb, wv_ref[...]) + one_ref[...]).astype(BF16)


def _mla_prep(z, gq, gkv, wq, wk, wv, p128, vones, tabs):
    b, s, _ = z.shape
    tm = min(s, 512)
    rope = tabs is not None
    full = lambda shape: pl.BlockSpec(shape, lambda bi, i: (0,) * len(shape))
    in_specs = [pl.BlockSpec((1, tm, MLA_Q_LORA), lambda bi, i: (bi, i, OFF_QA // MLA_Q_LORA)),
                pl.BlockSpec((1, tm, MLA_KV_LORA), lambda bi, i: (bi, i, OFF_KVA // MLA_KV_LORA)),
                pl.BlockSpec((1, tm, 128), lambda bi, i: (bi, i, OFF_KR // 128)),
                full((1, MLA_Q_LORA)), full((1, MLA_KV_LORA)),
                full(wq.shape), full(wk.shape), full(wv.shape), full(p128.shape), full(vones.shape)]
    args = [z, z, z, gq.reshape(1, -1), gkv.reshape(1, -1), wq, wk, wv, p128, vones]
    if rope:
        cq, sq, ck, sk = tabs
        in_specs += [pl.BlockSpec((tm, MLA_DQ), lambda bi, i: (i, 0))] * 2
        in_specs += [pl.BlockSpec((tm, 128), lambda bi, i: (i, 0))] * 2
        args += [cq, sq, ck, sk]
    hq = MLA_H * MLA_DQ
    hv = MLA_H * MLA_DV
    return pl.pallas_call(
        functools.partial(_mla_prep_body, rope=rope),
        grid=(b, s // tm),
        in_specs=in_specs,
        out_specs=[pl.BlockSpec((1, tm, hq), lambda bi, i: (bi, i, 0)),
                   pl.BlockSpec((1, tm, hq), lambda bi, i: (bi, i, 0)),
                   pl.BlockSpec((1, tm, hv), lambda bi, i: (bi, i, 0)),
                   pl.BlockSpec((1, tm, MLA_KV_LORA), lambda bi, i: (bi, i, 0))],
        out_shape=[jax.ShapeDtypeStruct((b, s, hq), BF16),
                   jax.ShapeDtypeStruct((b, s, hq), BF16),
                   jax.ShapeDtypeStruct((b, s, hv), BF16),
                   jax.ShapeDtypeStruct((b, s, MLA_KV_LORA), F32)],
        compiler_params=_cp(("parallel", "parallel"), 40),
        name="mla_prep",
    )(*args)


def _mla_expand_body(ckv_ref, kr_ref, wk_ref, wv_ref, p_ref, one_ref, k_out, v_out):
    cb = ckv_ref[0, 0].astype(BF16)
    k_out[0] = (_dot(cb, wk_ref[...]) + _dot(kr_ref[0, 0].astype(BF16), p_ref[...])).astype(BF16)
    v_out[0] = (_dot(cb, wv_ref[...]) + one_ref[...]).astype(BF16)


def _mla_expand(cache_ckv, cache_kr, layer, wk, wv, p64, vones):
    b, _, s, _ = cache_ckv.shape
    full = lambda shape: pl.BlockSpec(shape, lambda bi: (0,) * len(shape))
    hq = MLA_H * MLA_DQ
    hv = MLA_H * MLA_DV
    return pl.pallas_call(
        _mla_expand_body,
        grid=(b,),
        in_specs=[pl.BlockSpec((1, 1, s, MLA_KV_LORA), lambda bi: (bi, layer, 0, 0)),
                  pl.BlockSpec((1, 1, s, MLA_ROPE), lambda bi: (bi, layer, 0, 0)),
                  full(wk.shape), full(wv.shape), full(p64.shape), full(vones.shape)],
        out_specs=[pl.BlockSpec((1, s, hq), lambda bi: (bi, 0, 0)),
                   pl.BlockSpec((1, s, hv), lambda bi: (bi, 0, 0))],
        out_shape=[jax.ShapeDtypeStruct((b, s, hq), BF16),
                   jax.ShapeDtypeStruct((b, s, hv), BF16)],
        compiler_params=_cp(("parallel",), 32),
        name="mla_expand",
    )(cache_ckv, cache_kr, wk, wv, p64, vones)


def _mla_attn_body(*refs, has_ctx, nkv):
    it = iter(refs)
    q_ref, k_ref, v_ref = next(it), next(it), next(it)
    kc_ref = vc_ref = None
    if has_ctx:
        kc_ref, vc_ref = next(it), next(it)
    gp_ref, o_ref, m_scr, acc_scr = next(it), next(it), next(it), next(it)
    j = pl.program_id(2)

    def process(kr, vr):
        tk = kr.shape[1]
        for h in range(MLA_H):
            qk = slice(h * MLA_DQ, (h + 1) * MLA_DQ)
            s = _dot_nt(q_ref[0, :, qk], kr[0, :, qk])
            m_prev = m_scr[h]
            m_new = jnp.maximum(m_prev, jnp.max(s, axis=-1, keepdims=True))
            alpha = jnp.exp2(m_prev - m_new)
            p = jnp.exp2(s - jnp.concatenate([m_new] * (tk // 128), axis=-1))
            pv = _dot(p.astype(BF16), vr[0, :, h * MLA_DV:(h + 1) * MLA_DV])
            acc_scr[h] = jnp.concatenate([alpha] * (MLA_DV // 128), axis=-1) * acc_scr[h] + pv
            m_scr[h] = m_new

    @pl.when(j == 0)
    def _():
        m_scr[...] = jnp.full(m_scr.shape, NEG_INF, F32)
        acc_scr[...] = jnp.zeros(acc_scr.shape, F32)
        if has_ctx:
            process(kc_ref, vc_ref)

    process(k_ref, v_ref)

    @pl.when(j == nkv - 1)
    def _():
        outs = []
        for h in range(MLA_H):
            a = acc_scr[h]
            outs.append(a[:, :MLA_V] / a[:, MLA_V:MLA_V + 1])
        o_ref[0] = (jnp.concatenate(outs, axis=-1) * _silu(gp_ref[0])).astype(BF16)


def _mla_attn(q, k, v, gp, gpcol, ctx=None, tq=1024, tk=1024):
    b, sq, _ = q.shape
    sk = k.shape[1]
    tq = min(tq, sq)
    tk = min(tk, sk)
    nkv = sk // tk
    hq = MLA_H * MLA_DQ
    hv = MLA_H * MLA_DV
    in_specs = [pl.BlockSpec((1, tq, hq), lambda bi, i, j: (bi, i, 0)),
                pl.BlockSpec((1, tk, hq), lambda bi, i, j: (bi, j, 0)),
                pl.BlockSpec((1, tk, hv), lambda bi, i, j: (bi, j, 0))]
    args = [q, k, v]
    if ctx is not None:
        kc, vc = ctx
        in_specs += [pl.BlockSpec((1,) + kc.shape[1:], lambda bi, i, j: (bi, 0, 0)),
                     pl.BlockSpec((1,) + vc.shape[1:], lambda bi, i, j: (bi, 0, 0))]
        args += [kc, vc]
    in_specs += [pl.BlockSpec((1, tq, MLA_H * MLA_V), lambda bi, i, j: (bi, i, gpcol))]
    args += [gp]
    return pl.pallas_call(
        functools.partial(_mla_attn_body, has_ctx=ctx is not None, nkv=nkv),
        grid=(b, sq // tq, nkv),
        in_specs=in_specs,
        out_specs=pl.BlockSpec((1, tq, MLA_H * MLA_V), lambda bi, i, j: (bi, i, 0)),
        out_shape=jax.ShapeDtypeStruct((b, sq, MLA_H * MLA_V), BF16),
        scratch_shapes=[pltpu.VMEM((MLA_H, tq, 128), F32), pltpu.VMEM((MLA_H, tq, MLA_DV), F32)],
        compiler_params=_cp(("parallel", "parallel", "arbitrary"), 48),
        name="mla_attn",
    )(*args)


CTX_SPS = 2


def _ctx_attn_body(qm_ref, km_ref, vm_ref, nq_ref, nk_ref, nv_ref, sq_ref, sk_ref, sv_ref, sink_ref,
                   gpm_ref, gpn_ref, gps_ref, om_ref, on_ref, os_ref):
    for bb in range(qm_ref.shape[0]):
        view = lambda r: r.at[bb:bb + 1]
        _ctx_attn_one(view(qm_ref), view(km_ref), view(vm_ref), view(nq_ref), view(nk_ref), view(nv_ref),
                      view(sq_ref), view(sk_ref), view(sv_ref), sink_ref,
                      view(gpm_ref), view(gpn_ref), view(gps_ref), view(om_ref), view(on_ref), view(os_ref))


def _ctx_attn_one(qm_ref, km_ref, vm_ref, nq_ref, nk_ref, nv_ref, sq_ref, sk_ref, sv_ref, sink_ref,
                  gpm_ref, gpn_ref, gps_ref, om_ref, on_ref, os_ref):
    outs = []
    for h in range(MLA_H):
        qk = slice(h * MLA_DQ, (h + 1) * MLA_DQ)
        s = _dot_nt(qm_ref[0, :, qk], km_ref[0, :, qk])
        p = jnp.exp2(s - jnp.max(s, axis=-1, keepdims=True))
        pv = _dot(p.astype(BF16), vm_ref[0, :, h * MLA_DV:(h + 1) * MLA_DV])
        outs.append(pv[:, :MLA_V] / pv[:, MLA_V:MLA_V + 1])
    om_ref[0] = (jnp.concatenate(outs, axis=-1) * _silu(gpm_ref[0])).astype(BF16)

    outs = []
    for h in range(NAT_H):
        sl = slice(h * NAT_D, (h + 1) * NAT_D)
        s = _dot_nt(nq_ref[0, :, sl].astype(BF16), nk_ref[0, :, sl].astype(BF16)) * NAT_D ** -0.5
        p = jnp.exp(s - jnp.max(s, axis=-1, keepdims=True))
        den = jnp.sum(p, axis=-1, keepdims=True)
        outs.append(_dot(p.astype(BF16), nv_ref[0, :, sl].astype(BF16)) / den)
    on_ref[0] = (jnp.concatenate(outs, axis=-1) * _silu(gpn_ref[0])).astype(BF16)

    n_tok = sq_ref.shape[1]
    group = SWA_H // SWA_KVH
    q = sq_ref[0] * SWA_D ** -0.5
    outs = []
    for g in range(SWA_KVH):
        heads = range(g * group, (g + 1) * group)
        qg = jnp.concatenate([q[:, h * SWA_D:(h + 1) * SWA_D] for h in heads], axis=0).astype(BF16)
        sink = jnp.concatenate([jnp.broadcast_to(sink_ref[0:1, h:h + 1], (n_tok, 1)) for h in heads], axis=0)
        ksl = slice(g * SWA_D, (g + 1) * SWA_D)
        s = _dot_nt(qg, sk_ref[0, :, ksl].astype(BF16))
        m = jnp.maximum(jnp.max(s, axis=-1, keepdims=True), sink)
        p = jnp.exp(s - m)
        den = jnp.sum(p, axis=-1, keepdims=True) + jnp.exp(sink - m)
        o = _dot(p.astype(BF16), sv_ref[0, :, ksl].astype(BF16)) / den
        outs += [o[t * n_tok:(t + 1) * n_tok] for t in range(group)]
    os_ref[0] = (jnp.concatenate(outs, axis=-1) * _silu(gps_ref[0])).astype(BF16)


def _ctx_attn(q, k, v, z, sink):
    b, n_tok, _ = z.shape
    sps = CTX_SPS if b % CTX_SPS == 0 else 1
    blk = lambda w, off: pl.BlockSpec((sps, n_tok, w), lambda bi: (bi, 0, off // w))
    kd = SWA_KVH * SWA_D
    out = jax.ShapeDtypeStruct((b, n_tok, BR_W), BF16)
    return pl.pallas_call(
        _ctx_attn_body,
        grid=(b // sps,),
        in_specs=[blk(MLA_H * MLA_DQ, 0), blk(MLA_H * MLA_DQ, 0), blk(MLA_H * MLA_DV, 0),
                  blk(BR_W, OFF_NQ), blk(BR_W, OFF_NK), blk(BR_W, OFF_NV),
                  blk(BR_W, OFF_SQ), blk(kd, OFF_SK), blk(kd, OFF_SV),
                  pl.BlockSpec((1, SWA_H), lambda bi: (0, 0)),
                  blk(BR_W, OFF_GP), blk(BR_W, OFF_GP + 2 * BR_W), blk(BR_W, OFF_GP + 3 * BR_W)],
        out_specs=[blk(BR_W, 0)] * 3,
        out_shape=[out, out, out],
        compiler_params=_cp(("parallel",), 48),
        name="ctx_attn",
    )(q, k, v, z, z, z, z, z, z, sink.reshape(1, SWA_H), z, z, z)


def _ret_body(*refs, nc, has_s0, has_st):
    it = iter(refs)
    dec_ref, norm_ref, q_ref, k_ref, v_ref = next(it), next(it), next(it), next(it), next(it)
    s0_ref = next(it) if has_s0 else None
    gp_ref, o_ref = next(it), next(it)
    st_ref = next(it) if has_st else None
    o_scr, kv_scr = next(it), next(it)
    hgrp = pl.program_id(1)

    d = dec_ref[0]
    lg_all = jnp.minimum(d, 0.0) - jnp.log1p(jnp.exp(-jnp.abs(d)))
    hcol = lax.broadcasted_iota(jnp.int32, d.shape, 1)
    row = lax.broadcasted_iota(jnp.int32, (BLK, BLK), 0).astype(F32)
    col = lax.broadcasted_iota(jnp.int32, (BLK, BLK), 1).astype(F32)
    a = lax.broadcasted_iota(jnp.int32, (BLK, 1), 0).astype(F32)
    kscale = RET_DK ** -0.5

    heads = []
    for hh in range(RET_HPS):
        lg = jnp.sum(jnp.where(hcol == hgrp * RET_HPS + hh, lg_all, 0.0), axis=1, keepdims=True)
        lg_f, lg_b = lg[0:1, :], lg[1:2, :]
        dm = (jnp.where(row - col >= 0, jnp.exp(lg_f * jnp.maximum(row - col, 0.0)), 0.0)
              + jnp.where(col - row > 0, jnp.exp(lg_b * jnp.maximum(col - row, 0.0)), 0.0))
        fwd = (jnp.exp(lg_f * (a + 1.0)), jnp.exp(lg_f * (BLK - 1.0 - a)), jnp.exp(lg_f * BLK))
        bwd = (jnp.exp(lg_b * (BLK - a)), jnp.exp(lg_b * a), jnp.exp(lg_b * BLK))
        heads.append((hh, slice(hh * RET_DK, (hh + 1) * RET_DK), dm, fwd, bwd))

    def intra(c, _):
        off = pl.multiple_of(c * BLK, BLK)
        for hh, sl, dm, fwd, bwd in heads:
            q = q_ref[0, pl.ds(off, BLK), sl]
            k = k_ref[0, pl.ds(off, BLK), sl] * kscale
            v = v_ref[0, pl.ds(off, BLK), sl].astype(BF16)
            inner = _dot_nt(q.astype(BF16), k.astype(BF16)) * dm
            lhs = jnp.concatenate([inner.astype(BF16), jnp.transpose(k * fwd[1]).astype(BF16),
                                   jnp.transpose(k * bwd[1]).astype(BF16)], axis=0)
            r = _dot(lhs, v)
            o_scr[pl.ds(off, BLK), sl] = r[:BLK]
            kv_scr[hh, 0, c] = r[BLK:2 * BLK]
            kv_scr[hh, 1, c] = r[2 * BLK:]
        return 0

    lax.fori_loop(0, nc, intra, 0, unroll=2)

    def scan(j, carry):
        cf = j
        cb = nc - 1 - j
        off_f = pl.multiple_of(cf * BLK, BLK)
        off_b = pl.multiple_of(cb * BLK, BLK)
        new = []
        for (hh, sl, _, fwd, bwd), (sf, sb) in zip(heads, carry):
            qf = (q_ref[0, pl.ds(off_f, BLK), sl] * fwd[0]).astype(BF16)
            o_scr[pl.ds(off_f, BLK), sl] += _dot(qf, sf.astype(BF16))
            sf = sf * fwd[2] + kv_scr[hh, 0, cf]
            qb = (q_ref[0, pl.ds(off_b, BLK), sl] * bwd[0]).astype(BF16)
            o_scr[pl.ds(off_b, BLK), sl] += _dot(qb, sb.astype(BF16))
            sb = sb * bwd[2] + kv_scr[hh, 1, cb]
            new.append((sf, sb))
        return tuple(new)

    if has_s0:
        init = tuple((s0_ref[0, 0, 0, hh], s0_ref[0, 0, 1, hh]) for hh in range(RET_HPS))
    else:
        zero = jnp.zeros((RET_DK, RET_DV), F32)
        init = tuple((zero, zero) for _ in range(RET_HPS))
    fin = lax.fori_loop(0, nc, scan, init, unroll=2)
    if has_st:
        for hh in range(RET_HPS):
            st_ref[0, 0, hh] = fin[hh][0]
            st_ref[0, 1, hh] = fin[hh][1]
    for _, sl, _, _, _ in heads:
        o = o_scr[:, sl]
        o = o * lax.rsqrt(jnp.mean(o * o, axis=-1, keepdims=True) + EPS) * norm_ref[:, sl]
        o_ref[0, :, sl] = (o * _silu(gp_ref[0, :, sl])).astype(BF16)


def _retention(z, decay, norm, s0, layer, want_state):
    b, s, _ = z.shape
    nc = s // BLK
    w = RET_HPS * RET_DK
    colspec = lambda off: pl.BlockSpec((1, s, w), lambda bi, g: (bi, 0, off // w + g))
    in_specs = [pl.BlockSpec((1, 2, RET_H), lambda bi, g: (0, 0, 0)),
                pl.BlockSpec((1, w), lambda bi, g: (0, g)),
                colspec(OFF_RQ), colspec(OFF_RK), colspec(OFF_RV)]
    args = [decay.reshape(1, 2, RET_H), norm.reshape(1, -1), z, z, z]
    if s0 is not None:
        in_specs.append(pl.BlockSpec((1, 1, 2, RET_HPS, RET_DK, RET_DV), lambda bi, g: (bi, layer, 0, g, 0, 0)))
        args.append(s0)
    in_specs.append(colspec(OFF_GP + BR_W))
    args.append(z)
    out_specs = [pl.BlockSpec((1, s, w), lambda bi, g: (bi, 0, g))]
    out_shape = [jax.ShapeDtypeStruct((b, s, RET_H * RET_DV), BF16)]
    if want_state:
        out_specs.append(pl.BlockSpec((1, 2, RET_HPS, RET_DK, RET_DV), lambda bi, g: (bi, 0, g, 0, 0)))
        out_shape.append(jax.ShapeDtypeStruct((b, 2, RET_H, RET_DK, RET_DV), F32))
    res = pl.pallas_call(
        functools.partial(_ret_body, nc=nc, has_s0=s0 is not None, has_st=want_state),
        grid=(b, RET_H // RET_HPS),
        in_specs=in_specs,
        out_specs=out_specs,
        out_shape=out_shape,
        scratch_shapes=[pltpu.VMEM((s, w), F32), pltpu.VMEM((RET_HPS, 2, nc, RET_DK, RET_DV), F32)],
        compiler_params=_cp(("parallel", "parallel"), 56),
        name="retention",
    )(*args)
    return (res[0], res[1]) if want_state else (res[0], None)


def _nat_body(q_ref, k_ref, v_ref, kc_ref, vc_ref, bias_ref, gp_ref, o_ref, kb_scr, vb_scr, o_scr, *, rows):
    scale = NAT_D ** -0.5
    kb_scr[...] = k_ref[0].astype(BF16)
    vb_scr[...] = v_ref[0].astype(BF16)
    kc = kc_ref[0, 0].astype(BF16)
    vc = vc_ref[0, 0].astype(BF16)
    nq = NAT_QR * GRID_W
    win = NAT_WR * GRID_W
    nblk = rows // NAT_QR

    def step(t, _):
        r0 = t * NAT_QR
        ws = jnp.clip(r0 - NAT_ROWS // 2, 0, rows - NAT_WR)
        pat = jnp.where(t == 0, 0, jnp.where(t == nblk - 1, 2, 1))
        qoff = pl.multiple_of(r0 * GRID_W, nq)
        koff = pl.multiple_of(ws * GRID_W, GRID_W)
        q = q_ref[0, pl.ds(qoff, nq), :].astype(BF16)
        kw = kb_scr[pl.ds(koff, win), :]
        vw = vb_scr[pl.ds(koff, win), :]
        s_loc = _dot_nt(q, kw) * scale + bias_ref[0, 0, pat]
        s_ctx = _dot_nt(q, kc) * scale
        m = jnp.maximum(jnp.max(s_loc, axis=-1, keepdims=True), jnp.max(s_ctx, axis=-1, keepdims=True))
        p_loc = jnp.exp(s_loc - m)
        p_ctx = jnp.exp(s_ctx - m)
        den = jnp.sum(p_loc, axis=-1, keepdims=True) + jnp.sum(p_ctx, axis=-1, keepdims=True)
        o = _dot(p_loc.astype(BF16), vw) + _dot(p_ctx.astype(BF16), vc)
        o_scr[pl.ds(qoff, nq), :] = o / den
        return 0

    lax.fori_loop(0, nblk, step, 0, unroll=2)
    o_ref[0] = (o_scr[...] * _silu(gp_ref[0])).astype(BF16)


def _nat_bias_tables(rpb, rows):
    depth, n_h = rpb.shape[:2]
    c = jnp.arange(GRID_W)[:, None]
    kc = jnp.arange(GRID_W)[None, :]
    cs = jnp.clip(c - NAT_COLS // 2, 0, GRID_W - NAT_COLS)
    valid = (kc >= cs) & (kc < cs + NAT_COLS)
    onehot = ((kc - c + (NAT_COLS - 1))[:, :, None] == jnp.arange(2 * NAT_COLS - 1)[None, None, :]).astype(F32)
    toep = jnp.einsum("dhrj,ckj->dhrck", rpb.astype(F32), onehot, precision=lax.Precision.HIGHEST)
    toep = jnp.where(valid, toep, NEG_INF)
    neg = jnp.full((depth, n_h, GRID_W, GRID_W), NEG_INF, F32)
    pats = []
    for r0, ws in ((0, 0), (NAT_QR, 0), (rows - NAT_QR, rows - NAT_WR)):
        qrows = []
        for qi in range(NAT_QR):
            r = r0 + qi
            rs = min(max(r - NAT_ROWS // 2, 0), rows - NAT_ROWS)
            blocks = []
            for wm in range(NAT_WR):
                kr = ws + wm
                blocks.append(toep[:, :, kr - r + NAT_ROWS - 1] if rs <= kr < rs + NAT_ROWS else neg)
            qrows.append(jnp.concatenate(blocks, axis=-1))
        pats.append(jnp.concatenate(qrows, axis=-2))
    return jnp.stack(pats, axis=2)


def _nat(z, cache_k, cache_v, layer, bias):
    b, s, _ = z.shape
    rows = s // GRID_W
    assert rows % NAT_QR == 0 and rows >= 2 * NAT_WR - NAT_ROWS
    lc = cache_k.shape[2]
    colspec = lambda blk: pl.BlockSpec((1, s, NAT_D), lambda bi, h: (bi, 0, blk + h))
    cspec = pl.BlockSpec((1, 1, lc, NAT_D), lambda bi, h: (bi, layer, 0, h))
    return pl.pallas_call(
        functools.partial(_nat_body, rows=rows),
        grid=(b, NAT_H),
        in_specs=[colspec(OFF_NQ // 128), colspec(OFF_NK // 128), colspec(OFF_NV // 128), cspec, cspec,
                  pl.BlockSpec((1, 1) + bias.shape[2:], lambda bi, h: (layer, h, 0, 0, 0)),
                  colspec((OFF_GP + 2 * BR_W) // 128)],
        out_specs=pl.BlockSpec((1, s, NAT_D), lambda bi, h: (bi, 0, h)),
        out_shape=jax.ShapeDtypeStruct((b, s, NAT_H * NAT_D), BF16),
        scratch_shapes=[pltpu.VMEM((s, NAT_D), BF16), pltpu.VMEM((s, NAT_D), BF16),
                        pltpu.VMEM((s, NAT_D), F32)],
        compiler_params=_cp(("parallel", "parallel"), 48),
        name="nat_attn",
    )(z, z, z, cache_k, cache_v, bias, z)


def _swa_values_with_ones(v):
    lane = lax.broadcasted_iota(jnp.int32, v.shape, 1)
    halves = [jnp.where(lane < SWA_D, v if g == 0 else pltpu.roll(v, (SWA_KVH - g) * SWA_D, 1), 1.0)
              for g in range(SWA_KVH)]
    return jnp.concatenate(halves, axis=-1).astype(BF16)


def _swa_body(q_ref, k_ref, v_ref, kc_ref, vc_ref, sink_ref, cos_ref, sin_ref, gp_ref, o_ref,
              kr_scr, vl_scr, kc_scr, vc_scr, *, seq):
    scale = SWA_D ** -0.5
    group = SWA_H // SWA_KVH
    win = 3 * BLK
    i = pl.program_id(1)

    @pl.when(i == 0)
    def _():
        k = k_ref[0]
        kr_scr[...] = (k * cos_ref[...] + _swap_halves(k) * sin_ref[...]).astype(BF16)
        vl_scr[...] = _swa_values_with_ones(v_ref[0])
        kc_scr[...] = kc_ref[0, 0].astype(BF16)
        vc_scr[...] = _swa_values_with_ones(vc_ref[0, 0])

    start = pl.multiple_of(jnp.clip((i - 1) * BLK, 0, seq - win), BLK)
    qoff = pl.multiple_of(i * BLK, BLK)
    cq = cos_ref[pl.ds(qoff, BLK), :]
    sq = sin_ref[pl.ds(qoff, BLK), :]
    q = q_ref[0]
    q = (q * jnp.concatenate([cq] * (SWA_H // 2), axis=-1)
         + _swap_halves(q) * jnp.concatenate([sq] * (SWA_H // 2), axis=-1))
    q = q * scale
    kw = kr_scr[pl.ds(start, win), :]
    vw = vl_scr[pl.ds(start, win), :]

    qpos = qoff + lax.broadcasted_iota(jnp.int32, (BLK, win), 0)
    kpos = start + lax.broadcasted_iota(jnp.int32, (BLK, win), 1)
    valid = jnp.abs(qpos - kpos) <= SWA_WINDOW
    valid = jnp.concatenate([valid] * group, axis=0)

    outs = []
    for g in range(SWA_KVH):
        heads = range(g * group, (g + 1) * group)
        qg = jnp.concatenate([q[:, h * SWA_D:(h + 1) * SWA_D] for h in heads], axis=0).astype(BF16)
        sink = jnp.concatenate([jnp.broadcast_to(sink_ref[0:1, h:h + 1], (BLK, 1)) for h in heads], axis=0)
        ksl = slice(g * SWA_D, (g + 1) * SWA_D)
        vsl = slice(g * 2 * SWA_D, (g + 1) * 2 * SWA_D)
        s = jnp.concatenate([jnp.where(valid, _dot_nt(qg, kw[:, ksl]), NEG_INF),
                             _dot_nt(qg, kc_scr[:, ksl])], axis=-1)
        m = jnp.maximum(jnp.max(s, axis=-1, keepdims=True), sink)
        p = jnp.exp(s - m).astype(BF16)
        pv = _dot(p[:, :win], vw[:, vsl]) + _dot(p[:, win:], vc_scr[:, vsl])
        o = pv[:, :SWA_D] / (pv[:, SWA_D:SWA_D + 1] + jnp.exp(sink - m))
        outs += [o[t * BLK:(t + 1) * BLK] for t in range(group)]
    o = jnp.concatenate(outs, axis=-1)
    o_ref[0] = (o * _silu(gp_ref[0])).astype(BF16)


def _swa(z, cache_k, cache_v, layer, sink, cos2, sin2):
    b, s, _ = z.shape
    lc = cache_k.shape[2]
    hd = SWA_H * SWA_D
    kd = SWA_KVH * SWA_D
    cspec = pl.BlockSpec((1, 1, lc, kd), lambda bi, i: (bi, layer, 0, 0))
    tq = BLK
    return pl.pallas_call(
        functools.partial(_swa_body, seq=s),
        grid=(b, s // tq),
        in_specs=[pl.BlockSpec((1, tq, hd), lambda bi, i: (bi, i, OFF_SQ // hd)),
                  pl.BlockSpec((1, s, kd), lambda bi, i: (bi, 0, OFF_SK // kd)),
                  pl.BlockSpec((1, s, kd), lambda bi, i: (bi, 0, OFF_SV // kd)),
                  cspec, cspec,
                  pl.BlockSpec((1, SWA_H), lambda bi, i: (0, 0)),
                  pl.BlockSpec((s, 128), lambda bi, i: (0, 0)),
                  pl.BlockSpec((s, 128), lambda bi, i: (0, 0)),
                  pl.BlockSpec((1, tq, hd), lambda bi, i: (bi, i, (OFF_GP + 3 * BR_W) // hd))],
        out_specs=pl.BlockSpec((1, tq, hd), lambda bi, i: (bi, i, 0)),
        out_shape=jax.ShapeDtypeStruct((b, s, hd), BF16),
        scratch_shapes=[pltpu.VMEM((s, kd), BF16), pltpu.VMEM((s, 2 * kd), BF16),
                        pltpu.VMEM((lc, kd), BF16), pltpu.VMEM((lc, 2 * kd), BF16)],
        compiler_params=_cp(("parallel", "arbitrary"), 48),
        name="swa_attn",
    )(z, z, z, cache_k, cache_v, sink.reshape(1, SWA_H), cos2, sin2, z)


MERGE_SUB = 2


def _merge_body(h_ref, o0_ref, o1_ref, o2_ref, o3_ref, g0_ref, g1_ref, g2_ref, g3_ref, wb_ref, y_ref):
    h = h_ref[...]
    o_refs = (o0_ref, o1_ref, o2_ref, o3_ref)
    g_refs = (g0_ref, g1_ref, g2_ref, g3_ref)
    tsub = y_ref.shape[1] // MERGE_SUB
    for c in range(MERGE_SUB):
        cols = slice(c * tsub, (c + 1) * tsub)
        y = None
        for n in range(N_BRANCH):
            t = _sigmoid(_dot_nt(h, g_refs[n][cols, :])) * _dot(o_refs[n][...], wb_ref[n, :, cols])
            y = t if y is None else y + t
        y_ref[:, cols] = y.astype(BF16)


def _merge(h, ogs, w_gate_t, w_branch, layer, tm, tn):
    m, d = h.shape
    tm = min(tm, m)
    tn = min(tn, d)
    nj = d // tn
    gspec = lambda n: pl.BlockSpec((None, tn, d), lambda i, j: (layer, n * nj + j, 0))
    return pl.pallas_call(
        _merge_body,
        grid=(m // tm, nj),
        in_specs=[pl.BlockSpec((tm, d), lambda i, j: (i, 0))]
        + [pl.BlockSpec((tm, BR_W), lambda i, j: (i, 0))] * N_BRANCH
        + [gspec(n) for n in range(N_BRANCH)]
        + [pl.BlockSpec((None, N_BRANCH, BR_W, tn), lambda i, j: (layer, 0, 0, j))],
        out_specs=pl.BlockSpec((tm, tn), lambda i, j: (i, j)),
        out_shape=jax.ShapeDtypeStruct((m, d), BF16),
        compiler_params=_cp(("parallel", "arbitrary"), 56),
        name="merge",
    )(h, *ogs, w_gate_t, w_gate_t, w_gate_t, w_gate_t, w_branch)


OUT_SUB = 2


def _out_body(*refs, has_next):
    if has_next:
        y_ref, w_ref, x_ref, gate_ref, g_ref, gn_ref, sh_ref, sc_ref, o_ref, h_ref = refs
    else:
        y_ref, w_ref, x_ref, gate_ref, g_ref, o_ref = refs
    ts = y_ref.shape[1]
    tsub = ts // OUT_SUB
    for sub in range(OUT_SUB):
        rows = slice(sub * tsub, (sub + 1) * tsub)
        t = _dot(y_ref[0, rows, :], w_ref[...])
        t = t * lax.rsqrt(jnp.mean(t * t, axis=-1, keepdims=True) + EPS) * g_ref[...]
        x = x_ref[0, rows, :] + gate_ref[0] * t
        o_ref[0, rows, :] = x
        if has_next:
            y = x * lax.rsqrt(jnp.mean(x * x, axis=-1, keepdims=True) + EPS)
            y = y * gn_ref[...]
            h_ref[0, rows, :] = (y * (1.0 + sc_ref[0]) + sh_ref[0]).astype(BF16)


def _out_proj(y, w_out, layer, x, gate, g, nxt):
    b, s, d = x.shape
    ts = min(s, 512)
    per_b = gate.shape[0] > 1
    cidx = (lambda bi, i: (bi, 0, 0)) if per_b else (lambda bi, i: (0, 0, 0))
    tile = pl.BlockSpec((1, ts, d), lambda bi, i: (bi, i, 0))
    vec = pl.BlockSpec((1, d), lambda bi, i: (0, 0))
    in_specs = [tile, pl.BlockSpec((None, d, d), lambda bi, i: (layer, 0, 0), pipeline_mode=pl.Buffered(1)),
                tile, pl.BlockSpec((1, 1, d), cidx), vec]
    args = [y, w_out, x, gate, g.reshape(1, d)]
    out_specs = [tile]
    out_shape = [jax.ShapeDtypeStruct((b, s, d), F32)]
    if nxt is not None:
        gn, shift, scale = nxt
        in_specs += [vec, pl.BlockSpec((1, 1, d), cidx), pl.BlockSpec((1, 1, d), cidx)]
        args += [gn.reshape(1, d), shift, scale]
        out_specs.append(tile)
        out_shape.append(jax.ShapeDtypeStruct((b, s, d), BF16))
    res = pl.pallas_call(
        functools.partial(_out_body, has_next=nxt is not None),
        grid=(b, s // ts),
        in_specs=in_specs,
        out_specs=out_specs,
        out_shape=out_shape,
        compiler_params=_cp(("parallel", "parallel"), 56),
        name="out_proj",
    )(*args)
    return (res[0], res[1]) if nxt is not None else (res[0], None)


_MIX_SRC = ((0, 512), (832, 4416), (4672, 6720), (512, 768), (4416, 4672), (768, 832))
_MIX_COLS = 6720


RELAYOUT_ROWS = 256
RELAYOUT_ALIGN = 64


def _relayout_body(off_ref, valid_ref, w_ref, o_ref):
    j = pl.program_id(1)
    row = lax.broadcasted_iota(jnp.int32, w_ref.shape, 0)
    o_ref[...] = jnp.where(row < valid_ref[j], w_ref[...], 0.0).astype(BF16)


def _relayout_rows(w_t, offs, valid):
    depth, _, d = w_t.shape
    nblk = len(offs)
    grid_spec = pltpu.PrefetchScalarGridSpec(
        num_scalar_prefetch=2,
        grid=(depth, nblk),
        in_specs=[pl.BlockSpec((pl.Element(1), pl.Element(RELAYOUT_ROWS), pl.Element(d)),
                               lambda l, j, off, val: (l, off[j] * RELAYOUT_ALIGN, 0))],
        out_specs=pl.BlockSpec((1, RELAYOUT_ROWS, d), lambda l, j, off, val: (l, j, 0)),
    )
    return pl.pallas_call(
        _relayout_body,
        grid_spec=grid_spec,
        out_shape=jax.ShapeDtypeStruct((depth, nblk * RELAYOUT_ROWS, d), BF16),
        compiler_params=_cp(("parallel", "arbitrary"), 32),
        name="relayout_w_in",
    )(jnp.asarray([o // RELAYOUT_ALIGN for o in offs], jnp.int32), jnp.asarray(valid, jnp.int32), w_t)


def _relayout_w_in(w_in):
    n = w_in.shape[2]
    w_t = jnp.swapaxes(w_in, 1, 2)
    r = RELAYOUT_ROWS
    offs, valid = [], []
    for a, b in _MIX_SRC:
        assert len(offs) * r == sum(valid)
        for o in range(a, b, r):
            offs.append(o)
            valid.append(min(r, b - o))
    while len(offs) * r < N_MIX:
        offs.append(0)
        valid.append(0)
    assert len(offs) * r == N_MIX and sum(valid) == _MIX_COLS
    w_mix_t = _relayout_rows(w_t, offs, valid)
    ng = (n - _MIX_COLS) // r
    w_gate_t = _relayout_rows(w_t, [_MIX_COLS + r * j for j in range(ng)], [r] * ng)
    return w_mix_t, w_gate_t


def _mla_weights(w_q_up, w_kv_up):
    depth = w_q_up.shape[0]
    wq = w_q_up.reshape(depth, MLA_Q_LORA, MLA_H, MLA_NOPE + MLA_ROPE)
    wq = jnp.pad(wq, ((0, 0), (0, 0), (0, 0), (0, MLA_DQ - MLA_NOPE - MLA_ROPE)))
    wq = wq.reshape(depth, MLA_Q_LORA, MLA_H * MLA_DQ).astype(BF16)
    wkv = w_kv_up.reshape(depth, MLA_KV_LORA, MLA_H, MLA_NOPE + MLA_V)
    wk = jnp.pad(wkv[..., :MLA_NOPE], ((0, 0), (0, 0), (0, 0), (0, MLA_DQ - MLA_NOPE)))
    wk = wk.reshape(depth, MLA_KV_LORA, MLA_H * MLA_DQ).astype(BF16)
    wv = jnp.pad(wkv[..., MLA_NOPE:], ((0, 0), (0, 0), (0, 0), (0, MLA_DV - MLA_V)))
    wv = wv.reshape(depth, MLA_KV_LORA, MLA_H * MLA_DV).astype(BF16)
    j = jnp.arange(128)[:, None]
    cidx = jnp.arange(MLA_H * MLA_DQ)[None, :]
    p128 = ((cidx % MLA_DQ == MLA_NOPE + j) & (j < MLA_ROPE)).astype(BF16)
    vones = (jnp.arange(MLA_H * MLA_DV)[None, :] % MLA_DV == MLA_V).astype(F32)
    return wq, wk, wv, p128, vones


def _rope_tables(n_tok):
    pos = jnp.arange(n_tok)
    row = (pos // GRID_W).astype(F32)
    col = (pos % GRID_W).astype(F32)
    n_freq = MLA_ROPE // 4
    inv = ROPE_BASE ** (-jnp.arange(n_freq, dtype=F32) / n_freq)
    ang = jnp.concatenate([row[:, None] * inv[None], col[:, None] * inv[None]], axis=-1)
    cos, sin = jnp.cos(ang), jnp.sin(ang)
    c64 = jnp.concatenate([cos, cos], axis=-1)
    s64 = jnp.concatenate([-sin, sin], axis=-1)
    one, zero = jnp.ones_like(c64), jnp.zeros_like(c64)
    cq = jnp.concatenate([one, one, c64, one], axis=-1)
    sq = jnp.concatenate([zero, zero, s64, zero], axis=-1)
    ck = jnp.concatenate([c64, one], axis=-1)
    sk = jnp.concatenate([s64, zero], axis=-1)
    c2 = jnp.concatenate([c64, c64], axis=-1)
    s2 = jnp.concatenate([s64, s64], axis=-1)
    return (cq, sq, ck, sk), (c2, s2)


TM_IN = 2048
TN_IN = 768
TM_MERGE = 1024
TN_MERGE = 512


def kernel(x_prompt, x_sample, cache_mla_ckv, cache_mla_krope, state_ret, cache_nat_k, cache_nat_v,
           cache_swa_k, cache_swa_v, c, c_ctx, w_mod, b_mod, norm_pre, norm_post, w_in, mla_q_norm,
           mla_kv_norm, mla_w_q_up, mla_w_kv_up, ret_decay, ret_norm, nat_rpb, swa_sink, w_branch, w_out):
    depth = w_mod.shape[0]
    d = x_prompt.shape[-1]
    nb = x_sample.shape[0]
    s_lat = x_sample.shape[1]
    l_ctx = cache_mla_ckv.shape[2]

    w_mix, w_gate = _relayout_w_in(w_in)
    w_br = w_branch.astype(BF16)
    w_o = w_out.astype(BF16)
    wq, wk, wv, p128, vones = _mla_weights(mla_w_q_up, mla_w_kv_up)
    p64 = p128[:MLA_ROPE]
    mla_tabs, (cos2, sin2) = _rope_tables(s_lat)
    nat_bias = _nat_bias_tables(nat_rpb, s_lat // GRID_W)
    nat_k_c = cache_nat_k.reshape(nb, depth, l_ctx, NAT_H * NAT_D)
    nat_v_c = cache_nat_v.reshape(nb, depth, l_ctx, NAT_H * NAT_D)
    swa_k_c = cache_swa_k.reshape(nb, depth, l_ctx, SWA_KVH * SWA_D)
    swa_v_c = cache_swa_v.reshape(nb, depth, l_ctx, SWA_KVH * SWA_D)

    rows = 8
    conds = jnp.concatenate([c_ctx[None, :], c, jnp.zeros((rows - 1 - nb, d), F32)], axis=0)
    mod = _modulation(conds, w_mod, b_mod)

    def mod_parts(l, lo, hi):
        m = mod[l, lo:hi]
        return m[:, None, :d], m[:, None, d:2 * d], m[:, None, 2 * d:]

    def finish_layer(l, lo, hi, x, h, ogs):
        b, s, _ = x.shape
        gate = mod_parts(l, lo, hi)[2]
        y = _merge(h.reshape(b * s, d), [o.reshape(b * s, BR_W) for o in ogs], w_gate, w_br, l, TM_MERGE, TN_MERGE)
        nxt = None
        if l + 1 < depth:
            shift_n, scale_n, _ = mod_parts(l + 1, lo, hi)
            nxt = (norm_pre[l + 1], shift_n, scale_n)
        return _out_proj(y.reshape(b, s, d), w_o, l, x, gate, norm_post[l], nxt)

    def start_pass(x, lo, hi):
        shift, scale, _ = mod_parts(0, lo, hi)
        return _modnorm(x, norm_pre[0], shift, scale)

    def in_proj(x, h, l):
        b, s, _ = x.shape
        return _in_proj(h.reshape(b * s, d), w_mix, l, TM_IN, TN_IN).reshape(b, s, N_MIX)

    yp = x_prompt
    bc, sc, _ = yp.shape
    ckv_l, kr_l, st_l, nk_l, nv_l, sk_l, sv_l = [], [], [], [], [], [], []
    h = start_pass(yp, 0, 1)
    for l in range(depth):
        z = in_proj(yp, h, l)
        q, kf, vf, ckv = _mla_prep(z, mla_q_norm[l], mla_kv_norm[l], wq[l], wk[l], wv[l], p128, vones, None)
        og_mla, og_nat, og_swa = _ctx_attn(q, kf, vf, z, swa_sink[l])
        og_ret, st = _retention(z, ret_decay[l], ret_norm[l], None, 0, True)
        yp, h = finish_layer(l, 0, 1, yp, h, (og_mla, og_ret, og_nat, og_swa))
        ckv_l.append(ckv)
        kr_l.append(z[:, :, OFF_KR:OFF_KR + MLA_ROPE])
        st_l.append(st)
        nk_l.append(z[:, :, OFF_NK:OFF_NK + 512].reshape(bc, sc, NAT_H, NAT_D))
        nv_l.append(z[:, :, OFF_NV:OFF_NV + 512].reshape(bc, sc, NAT_H, NAT_D))
        sk_l.append(z[:, :, OFF_SK:OFF_SK + 128].reshape(bc, sc, SWA_KVH, SWA_D))
        sv_l.append(z[:, :, OFF_SV:OFF_SV + 128].reshape(bc, sc, SWA_KVH, SWA_D))

    ys = x_sample
    h = start_pass(ys, 1, 1 + nb)
    for l in range(depth):
        z = in_proj(ys, h, l)
        q, kf, vf, _ = _mla_prep(z, mla_q_norm[l], mla_kv_norm[l], wq[l], wk[l], wv[l], p128, vones, mla_tabs)
        kc, vc = _mla_expand(cache_mla_ckv, cache_mla_krope, l, wk[l], wv[l], p64, vones)
        og_mla = _mla_attn(q, kf, vf, z, OFF_GP // BR_W, ctx=(kc, vc))
        og_ret, _ = _retention(z, ret_decay[l], ret_norm[l], state_ret, l, False)
        og_nat = _nat(z, nat_k_c, nat_v_c, l, nat_bias)
        og_swa = _swa(z, swa_k_c, swa_v_c, l, swa_sink[l], cos2, sin2)
        ys, h = finish_layer(l, 1, 1 + nb, ys, h, (og_mla, og_ret, og_nat, og_swa))

    return (yp, ys, jnp.stack(ckv_l, axis=1), jnp.stack(kr_l, axis=1), jnp.stack(st_l, axis=1),
            jnp.stack(nk_l, axis=1), jnp.stack(nv_l, axis=1), jnp.stack(sk_l, axis=1), jnp.stack(sv_l, axis=1))
```

```python
import functools

import jax
import jax.numpy as jnp
from jax import lax
from jax.experimental import pallas as pl
from jax.experimental.pallas import tpu as pltpu

F32 = jnp.float32
BF16 = jnp.bfloat16

GRID_W = 64
BLK = 128
ROPE_BASE = 10000.0
EPS = 1e-6
NEG_INF = -1e30
N_BRANCH = 4
BR_W = 512

MLA_H = 4
MLA_Q_LORA = 512
MLA_KV_LORA = 256
MLA_NOPE = 128
MLA_ROPE = 64
MLA_V = 128
MLA_DQ = 256
MLA_DV = 256
MLA_QSCALE = (MLA_NOPE + MLA_ROPE) ** -0.5 * 1.4426950408889634

RET_H = 4
RET_DK = 128
RET_DV = 128
RET_HPS = 2

NAT_H = 4
NAT_D = 128
NAT_ROWS = 8
NAT_COLS = 16
NAT_QR = 4
NAT_WR = NAT_ROWS + NAT_QR

SWA_H = 8
SWA_KVH = 2
SWA_D = 64
SWA_WINDOW = 128

OFF_QA = 0
OFF_RQ = 512
OFF_RK = 1024
OFF_RV = 1536
OFF_NQ = 2048
OFF_NK = 2560
OFF_NV = 3072
OFF_SQ = 3584
OFF_GP = 4096
OFF_KVA = 6144
OFF_SK = 6400
OFF_SV = 6528
OFF_KR = 6656
N_MIX_USED = 6784
N_MIX = 6912

V7X_VMEM_BYTES = 64 * 1024 * 1024


def _cp(sem, vmem_mb):
    assert vmem_mb * 1024 * 1024 < V7X_VMEM_BYTES
    return pltpu.CompilerParams(dimension_semantics=sem, vmem_limit_bytes=vmem_mb * 1024 * 1024)


def _sigmoid(x):
    return 1.0 / (1.0 + jnp.exp(-x))


def _silu(x):
    return x * _sigmoid(x)


def _swap_halves(x):
    n = x.shape[-1]
    ax = x.ndim - 1
    lane = lax.broadcasted_iota(jnp.int32, x.shape, ax)
    return jnp.where((lane & 32) == 0, pltpu.roll(x, n - 32, ax), pltpu.roll(x, 32, ax))


def _dot_nt(a, b):
    return lax.dot_general(a, b, (((1,), (1,)), ((), ())), preferred_element_type=F32)


def _dot(a, b):
    return jnp.dot(a, b, preferred_element_type=F32)


def _mod_body(c_ref, w_ref, b_ref, o_ref):
    a = _silu(c_ref[...]).astype(BF16)
    o_ref[0] = _dot(a, w_ref[0].astype(BF16)) + b_ref[0]


def _modulation(conds, w_mod, b_mod):
    depth, d, n = w_mod.shape
    rows = conds.shape[0]
    tn = 1024 if n % 1024 == 0 else n
    return pl.pallas_call(
        _mod_body,
        grid=(depth, n // tn),
        in_specs=[pl.BlockSpec((rows, d), lambda l, j: (0, 0)),
                  pl.BlockSpec((1, d, tn), lambda l, j: (l, 0, j)),
                  pl.BlockSpec((1, 1, tn), lambda l, j: (l, 0, j))],
        out_specs=pl.BlockSpec((1, rows, tn), lambda l, j: (l, 0, j)),
        out_shape=jax.ShapeDtypeStruct((depth, rows, n), F32),
        compiler_params=_cp(("parallel", "parallel"), 40),
        name="modulation",
    )(conds, w_mod, b_mod.reshape(depth, 1, n))


def _modnorm_body(x_ref, g_ref, sh_ref, sc_ref, h_ref):
    x = x_ref[0]
    y = x * lax.rsqrt(jnp.mean(x * x, axis=-1, keepdims=True) + EPS)
    y = y * g_ref[...]
    h_ref[0] = (y * (1.0 + sc_ref[0]) + sh_ref[0]).astype(BF16)


def _modnorm(x, g, shift, scale):
    b, s, d = x.shape
    ts = min(s, 512)
    per_b = shift.shape[0] > 1
    cidx = (lambda bi, i: (bi, 0, 0)) if per_b else (lambda bi, i: (0, 0, 0))
    return pl.pallas_call(
        _modnorm_body,
        grid=(b, s // ts),
        in_specs=[pl.BlockSpec((1, ts, d), lambda bi, i: (bi, i, 0)),
                  pl.BlockSpec((1, d), lambda bi, i: (0, 0)),
                  pl.BlockSpec((1, 1, d), cidx),
                  pl.BlockSpec((1, 1, d), cidx)],
        out_specs=pl.BlockSpec((1, ts, d), lambda bi, i: (bi, i, 0)),
        out_shape=jax.ShapeDtypeStruct((b, s, d), BF16),
        compiler_params=_cp(("parallel", "parallel"), 32),
        name="modnorm",
    )(x, g.reshape(1, d), shift, scale)


def _mm_body(a_ref, b_ref, o_ref):
    o_ref[...] = _dot_nt(a_ref[...], b_ref[...])


def _in_proj(a, w_mix_t, layer, tm, tn):
    m, k = a.shape
    tm = min(tm, m)
    return pl.pallas_call(
        _mm_body,
        grid=(m // tm, N_MIX // tn),
        in_specs=[pl.BlockSpec((tm, k), lambda i, j: (i, 0)),
                  pl.BlockSpec((None, tn, k), lambda i, j: (layer, j, 0))],
        out_specs=pl.BlockSpec((tm, tn), lambda i, j: (i, j)),
        out_shape=jax.ShapeDtypeStruct((m, N_MIX), F32),
        compiler_params=_cp(("parallel", "arbitrary"), 48),
        name="in_proj",
    )(a, w_mix_t)


def _mla_prep_body(*refs, rope):
    if rope:
        (qa_ref, kva_ref, kr_ref, gq_ref, gkv_ref, wq_ref, wk_ref, wv_ref, p_ref, one_ref,
         cq_ref, sq_ref, ck_ref, sk_ref, q_out, k_out, v_out, ckv_out) = refs
    else:
        (qa_ref, kva_ref, kr_ref, gq_ref, gkv_ref, wq_ref, wk_ref, wv_ref, p_ref, one_ref,
         q_out, k_out, v_out, ckv_out) = refs
    qa = qa_ref[0]
    cq = qa * lax.rsqrt(jnp.mean(qa * qa, axis=-1, keepdims=True) + EPS) * gq_ref[...]
    q = _dot(cq.astype(BF16), wq_ref[...])
    kva = kva_ref[0]
    ckv = kva * lax.rsqrt(jnp.mean(kva * kva, axis=-1, keepdims=True) + EPS) * gkv_ref[...]
    ckv_out[0] = ckv
    kr = kr_ref[0]
    if rope:
        cq_t = jnp.concatenate([cq_ref[...]] * MLA_H, axis=-1)
        sq_t = jnp.concatenate([sq_ref[...]] * MLA_H, axis=-1)
        q = q * cq_t + _swap_halves(q) * sq_t
        kr = kr * ck_ref[...] + _swap_halves(kr) * sk_ref[...]
    q_out[0] = (q * MLA_QSCALE).astype(BF16)
    cb = ckv.astype(BF16)
    k_out[0] = (_dot(cb, wk_ref[...]) + _dot(kr.astype(BF16), p_ref[...])).astype(BF16)
    v_out[0] = (_dot(cb, wv_ref[...]) + one_ref[...]).astype(BF16)


def _mla_prep(z, gq, gkv, wq, wk, wv, p128, vones, tabs):
    b, s, _ = z.shape
    tm = min(s, 512)
    rope = tabs is not None
    full = lambda shape: pl.BlockSpec(shape, lambda bi, i: (0,) * len(shape))
    in_specs = [pl.BlockSpec((1, tm, MLA_Q_LORA), lambda bi, i: (bi, i, OFF_QA // MLA_Q_LORA)),
                pl.BlockSpec((1, tm, MLA_KV_LORA), lambda bi, i: (bi, i, OFF_KVA // MLA_KV_LORA)),
                pl.BlockSpec((1, tm, 128), lambda bi, i: (bi, i, OFF_KR // 128)),
                full((1, MLA_Q_LORA)), full((1, MLA_KV_LORA)),
                full(wq.shape), full(wk.shape), full(wv.shape), full(p128.shape), full(vones.shape)]
    args = [z, z, z, gq.reshape(1, -1), gkv.reshape(1, -1), wq, wk, wv, p128, vones]
    if rope:
        cq, sq, ck, sk = tabs
        in_specs += [pl.BlockSpec((tm, MLA_DQ), lambda bi, i: (i, 0))] * 2
        in_specs += [pl.BlockSpec((tm, 128), lambda bi, i: (i, 0))] * 2
        args += [cq, sq, ck, sk]
    hq = MLA_H * MLA_DQ
    hv = MLA_H * MLA_DV
    return pl.pallas_call(
        functools.partial(_mla_prep_body, rope=rope),
        grid=(b, s // tm),
        in_specs=in_specs,
        out_specs=[pl.BlockSpec((1, tm, hq), lambda bi, i: (bi, i, 0)),
                   pl.BlockSpec((1, tm, hq), lambda bi, i: (bi, i, 0)),
                   pl.BlockSpec((1, tm, hv), lambda bi, i: (bi, i, 0)),
                   pl.BlockSpec((1, tm, MLA_KV_LORA), lambda bi, i: (bi, i, 0))],
        out_shape=[jax.ShapeDtypeStruct((b, s, hq), BF16),
                   jax.ShapeDtypeStruct((b, s, hq), BF16),
                   jax.ShapeDtypeStruct((b, s, hv), BF16),
                   jax.ShapeDtypeStruct((b, s, MLA_KV_LORA), F32)],
        compiler_params=_cp(("parallel", "parallel"), 40),
        name="mla_prep",
    )(*args)


def _mla_expand_body(ckv_ref, kr_ref, wk_ref, wv_ref, p_ref, one_ref, k_out, v_out):
    cb = ckv_ref[0, 0].astype(BF16)
    k_out[0] = (_dot(cb, wk_ref[...]) + _dot(kr_ref[0, 0].astype(BF16), p_ref[...])).astype(BF16)
    v_out[0] = (_dot(cb, wv_ref[...]) + one_ref[...]).astype(BF16)


def _mla_expand(cache_ckv, cache_kr, layer, wk, wv, p64, vones):
    b, _, s, _ = cache_ckv.shape
    full = lambda shape: pl.BlockSpec(shape, lambda bi: (0,) * len(shape))
    hq = MLA_H * MLA_DQ
    hv = MLA_H * MLA_DV
    return pl.pallas_call(
        _mla_expand_body,
        grid=(b,),
        in_specs=[pl.BlockSpec((1, 1, s, MLA_KV_LORA), lambda bi: (bi, layer, 0, 0)),
                  pl.BlockSpec((1, 1, s, MLA_ROPE), lambda bi: (bi, layer, 0, 0)),
                  full(wk.shape), full(wv.shape), full(p64.shape), full(vones.shape)],
        out_specs=[pl.BlockSpec((1, s, hq), lambda bi: (bi, 0, 0)),
                   pl.BlockSpec((1, s, hv), lambda bi: (bi, 0, 0))],
        out_shape=[jax.ShapeDtypeStruct((b, s, hq), BF16),
                   jax.ShapeDtypeStruct((b, s, hv), BF16)],
        compiler_params=_cp(("parallel",), 32),
        name="mla_expand",
    )(cache_ckv, cache_kr, wk, wv, p64, vones)


def _mla_attn_body(*refs, has_ctx, nkv):
    it = iter(refs)
    q_ref, k_ref, v_ref = next(it), next(it), next(it)
    kc_ref = vc_ref = None
    if has_ctx:
        kc_ref, vc_ref = next(it), next(it)
    gp_ref, o_ref, m_scr, acc_scr = next(it), next(it), next(it), next(it)
    j = pl.program_id(2)

    def process(kr, vr):
        tk = kr.shape[1]
        for h in range(MLA_H):
            qk = slice(h * MLA_DQ, (h + 1) * MLA_DQ)
            s = _dot_nt(q_ref[0, :, qk], kr[0, :, qk])
            m_prev = m_scr[h]
            m_new = jnp.maximum(m_prev, jnp.max(s, axis=-1, keepdims=True))
            alpha = jnp.exp2(m_prev - m_new)
            p = jnp.exp2(s - jnp.concatenate([m_new] * (tk // 128), axis=-1))
            pv = _dot(p.astype(BF16), vr[0, :, h * MLA_DV:(h + 1) * MLA_DV])
            acc_scr[h] = jnp.concatenate([alpha] * (MLA_DV // 128), axis=-1) * acc_scr[h] + pv
            m_scr[h] = m_new

    @pl.when(j == 0)
    def _():
        m_scr[...] = jnp.full(m_scr.shape, NEG_INF, F32)
        acc_scr[...] = jnp.zeros(acc_scr.shape, F32)
        if has_ctx:
            process(kc_ref, vc_ref)

    process(k_ref, v_ref)

    @pl.when(j == nkv - 1)
    def _():
        outs = []
        for h in range(MLA_H):
            a = acc_scr[h]
            outs.append(a[:, :MLA_V] / a[:, MLA_V:MLA_V + 1])
        o_ref[0] = (jnp.concatenate(outs, axis=-1) * _silu(gp_ref[0])).astype(BF16)


def _mla_attn(q, k, v, gp, gpcol, ctx=None, tq=1024, tk=1024):
    b, sq, _ = q.shape
    sk = k.shape[1]
    tq = min(tq, sq)
    tk = min(tk, sk)
    nkv = sk // tk
    hq = MLA_H * MLA_DQ
    hv = MLA_H * MLA_DV
    in_specs = [pl.BlockSpec((1, tq, hq), lambda bi, i, j: (bi, i, 0)),
                pl.BlockSpec((1, tk, hq), lambda bi, i, j: (bi, j, 0)),
                pl.BlockSpec((1, tk, hv), lambda bi, i, j: (bi, j, 0))]
    args = [q, k, v]
    if ctx is not None:
        kc, vc = ctx
        in_specs += [pl.BlockSpec((1,) + kc.shape[1:], lambda bi, i, j: (bi, 0, 0)),
                     pl.BlockSpec((1,) + vc.shape[1:], lambda bi, i, j: (bi, 0, 0))]
        args += [kc, vc]
    in_specs += [pl.BlockSpec((1, tq, MLA_H * MLA_V), lambda bi, i, j: (bi, i, gpcol))]
    args += [gp]
    return pl.pallas_call(
        functools.partial(_mla_attn_body, has_ctx=ctx is not None, nkv=nkv),
        grid=(b, sq // tq, nkv),
        in_specs=in_specs,
        out_specs=pl.BlockSpec((1, tq, MLA_H * MLA_V), lambda bi, i, j: (bi, i, 0)),
        out_shape=jax.ShapeDtypeStruct((b, sq, MLA_H * MLA_V), BF16),
        scratch_shapes=[pltpu.VMEM((MLA_H, tq, 128), F32), pltpu.VMEM((MLA_H, tq, MLA_DV), F32)],
        compiler_params=_cp(("parallel", "parallel", "arbitrary"), 48),
        name="mla_attn",
    )(*args)


CTX_SPS = 2


def _ctx_attn_body(qm_ref, km_ref, vm_ref, nq_ref, nk_ref, nv_ref, sq_ref, sk_ref, sv_ref, sink_ref,
                   gpm_ref, gpn_ref, gps_ref, om_ref, on_ref, os_ref):
    for bb in range(qm_ref.shape[0]):
        view = lambda r: r.at[bb:bb + 1]
        _ctx_attn_one(view(qm_ref), view(km_ref), view(vm_ref), view(nq_ref), view(nk_ref), view(nv_ref),
                      view(sq_ref), view(sk_ref), view(sv_ref), sink_ref,
                      view(gpm_ref), view(gpn_ref), view(gps_ref), view(om_ref), view(on_ref), view(os_ref))


def _ctx_attn_one(qm_ref, km_ref, vm_ref, nq_ref, nk_ref, nv_ref, sq_ref, sk_ref, sv_ref, sink_ref,
                  gpm_ref, gpn_ref, gps_ref, om_ref, on_ref, os_ref):
    outs = []
    for h in range(MLA_H):
        qk = slice(h * MLA_DQ, (h + 1) * MLA_DQ)
        s = _dot_nt(qm_ref[0, :, qk], km_ref[0, :, qk])
        p = jnp.exp2(s - jnp.max(s, axis=-1, keepdims=True))
        pv = _dot(p.astype(BF16), vm_ref[0, :, h * MLA_DV:(h + 1) * MLA_DV])
        outs.append(pv[:, :MLA_V] / pv[:, MLA_V:MLA_V + 1])
    om_ref[0] = (jnp.concatenate(outs, axis=-1) * _silu(gpm_ref[0])).astype(BF16)

    outs = []
    for h in range(NAT_H):
        sl = slice(h * NAT_D, (h + 1) * NAT_D)
        s = _dot_nt(nq_ref[0, :, sl].astype(BF16), nk_ref[0, :, sl].astype(BF16)) * NAT_D ** -0.5
        p = jnp.exp(s - jnp.max(s, axis=-1, keepdims=True))
        den = jnp.sum(p, axis=-1, keepdims=True)
        outs.append(_dot(p.astype(BF16), nv_ref[0, :, sl].astype(BF16)) / den)
    on_ref[0] = (jnp.concatenate(outs, axis=-1) * _silu(gpn_ref[0])).astype(BF16)

    n_tok = sq_ref.shape[1]
    group = SWA_H // SWA_KVH
    q = sq_ref[0] * SWA_D ** -0.5
    outs = []
    for g in range(SWA_KVH):
        heads = range(g * group, (g + 1) * group)
        qg = jnp.concatenate([q[:, h * SWA_D:(h + 1) * SWA_D] for h in heads], axis=0).astype(BF16)
        sink = jnp.concatenate([jnp.broadcast_to(sink_ref[0:1, h:h + 1], (n_tok, 1)) for h in heads], axis=0)
        ksl = slice(g * SWA_D, (g + 1) * SWA_D)
        s = _dot_nt(qg, sk_ref[0, :, ksl].astype(BF16))
        m = jnp.maximum(jnp.max(s, axis=-1, keepdims=True), sink)
        p = jnp.exp(s - m)
        den = jnp.sum(p, axis=-1, keepdims=True) + jnp.exp(sink - m)
        o = _dot(p.astype(BF16), sv_ref[0, :, ksl].astype(BF16)) / den
        outs += [o[t * n_tok:(t + 1) * n_tok] for t in range(group)]
    os_ref[0] = (jnp.concatenate(outs, axis=-1) * _silu(gps_ref[0])).astype(BF16)


def _ctx_attn(q, k, v, z, sink):
    b, n_tok, _ = z.shape
    sps = CTX_SPS if b % CTX_SPS == 0 else 1
    blk = lambda w, off: pl.BlockSpec((sps, n_tok, w), lambda bi: (bi, 0, off // w))
    kd = SWA_KVH * SWA_D
    out = jax.ShapeDtypeStruct((b, n_tok, BR_W), BF16)
    return pl.pallas_call(
        _ctx_attn_body,
        grid=(b // sps,),
        in_specs=[blk(MLA_H * MLA_DQ, 0), blk(MLA_H * MLA_DQ, 0), blk(MLA_H * MLA_DV, 0),
                  blk(BR_W, OFF_NQ), blk(BR_W, OFF_NK), blk(BR_W, OFF_NV),
                  blk(BR_W, OFF_SQ), blk(kd, OFF_SK), blk(kd, OFF_SV),
                  pl.BlockSpec((1, SWA_H), lambda bi: (0, 0)),
                  blk(BR_W, OFF_GP), blk(BR_W, OFF_GP + 2 * BR_W), blk(BR_W, OFF_GP + 3 * BR_W)],
        out_specs=[blk(BR_W, 0)] * 3,
        out_shape=[out, out, out],
        compiler_params=_cp(("parallel",), 48),
        name="ctx_attn",
    )(q, k, v, z, z, z, z, z, z, sink.reshape(1, SWA_H), z, z, z)


def _ret_body(*refs, nc, has_s0, has_st):
    it = iter(refs)
    dec_ref, norm_ref, q_ref, k_ref, v_ref = next(it), next(it), next(it), next(it), next(it)
    s0_ref = next(it) if has_s0 else None
    gp_ref, o_ref = next(it), next(it)
    st_ref = next(it) if has_st else None
    o_scr, kv_scr = next(it), next(it)
    hgrp = pl.program_id(1)

    d = dec_ref[0]
    lg_all = jnp.minimum(d, 0.0) - jnp.log1p(jnp.exp(-jnp.abs(d)))
    hcol = lax.broadcasted_iota(jnp.int32, d.shape, 1)
    row = lax.broadcasted_iota(jnp.int32, (BLK, BLK), 0).astype(F32)
    col = lax.broadcasted_iota(jnp.int32, (BLK, BLK), 1).astype(F32)
    a = lax.broadcasted_iota(jnp.int32, (BLK, 1), 0).astype(F32)
    kscale = RET_DK ** -0.5

    heads = []
    for hh in range(RET_HPS):
        lg = jnp.sum(jnp.where(hcol == hgrp * RET_HPS + hh, lg_all, 0.0), axis=1, keepdims=True)
        lg_f, lg_b = lg[0:1, :], lg[1:2, :]
        dm = (jnp.where(row - col >= 0, jnp.exp(lg_f * jnp.maximum(row - col, 0.0)), 0.0)
              + jnp.where(col - row > 0, jnp.exp(lg_b * jnp.maximum(col - row, 0.0)), 0.0))
        fwd = (jnp.exp(lg_f * (a + 1.0)), jnp.exp(lg_f * (BLK - 1.0 - a)), jnp.exp(lg_f * BLK))
        bwd = (jnp.exp(lg_b * (BLK - a)), jnp.exp(lg_b * a), jnp.exp(lg_b * BLK))
        heads.append((hh, slice(hh * RET_DK, (hh + 1) * RET_DK), dm, fwd, bwd))

    def intra(c, _):
        off = pl.multiple_of(c * BLK, BLK)
        for hh, sl, dm, fwd, bwd in heads:
            q = q_ref[0, pl.ds(off, BLK), sl]
            k = k_ref[0, pl.ds(off, BLK), sl] * kscale
            v = v_ref[0, pl.ds(off, BLK), sl].astype(BF16)
            inner = _dot_nt(q.astype(BF16), k.astype(BF16)) * dm
            lhs = jnp.concatenate([inner.astype(BF16), jnp.transpose(k * fwd[1]).astype(BF16),
                                   jnp.transpose(k * bwd[1]).astype(BF16)], axis=0)
            r = _dot(lhs, v)
            o_scr[pl.ds(off, BLK), sl] = r[:BLK]
            kv_scr[hh, 0, c] = r[BLK:2 * BLK]
            kv_scr[hh, 1, c] = r[2 * BLK:]
        return 0

    lax.fori_loop(0, nc, intra, 0, unroll=4 if nc % 4 == 0 else 2)

    def scan(j, carry):
        cf = j
        cb = nc - 1 - j
        off_f = pl.multiple_of(cf * BLK, BLK)
        off_b = pl.multiple_of(cb * BLK, BLK)
        new = []
        for (hh, sl, _, fwd, bwd), (sf, sb) in zip(heads, carry):
            qf = (q_ref[0, pl.ds(off_f, BLK), sl] * fwd[0]).astype(BF16)
            o_scr[pl.ds(off_f, BLK), sl] += _dot(qf, sf.astype(BF16))
            sf = sf * fwd[2] + kv_scr[hh, 0, cf]
            qb = (q_ref[0, pl.ds(off_b, BLK), sl] * bwd[0]).astype(BF16)
            o_scr[pl.ds(off_b, BLK), sl] += _dot(qb, sb.astype(BF16))
            sb = sb * bwd[2] + kv_scr[hh, 1, cb]
            new.append((sf, sb))
        return tuple(new)

    if has_s0:
        init = tuple((s0_ref[0, 0, 0, hh], s0_ref[0, 0, 1, hh]) for hh in range(RET_HPS))
    else:
        zero = jnp.zeros((RET_DK, RET_DV), F32)
        init = tuple((zero, zero) for _ in range(RET_HPS))
    fin = lax.fori_loop(0, nc, scan, init, unroll=4 if nc % 4 == 0 else 2)
    if has_st:
        for hh in range(RET_HPS):
            st_ref[0, 0, hh] = fin[hh][0]
            st_ref[0, 1, hh] = fin[hh][1]
    for _, sl, _, _, _ in heads:
        o = o_scr[:, sl]
        o = o * lax.rsqrt(jnp.mean(o * o, axis=-1, keepdims=True) + EPS) * norm_ref[:, sl]
        o_ref[0, :, sl] = (o * _silu(gp_ref[0, :, sl])).astype(BF16)


def _retention(z, decay, norm, s0, layer, want_state):
    b, s, _ = z.shape
    nc = s // BLK
    w = RET_HPS * RET_DK
    colspec = lambda off: pl.BlockSpec((1, s, w), lambda bi, g: (bi, 0, off // w + g))
    in_specs = [pl.BlockSpec((1, 2, RET_H), lambda bi, g: (0, 0, 0)),
                pl.BlockSpec((1, w), lambda bi, g: (0, g)),
                colspec(OFF_RQ), colspec(OFF_RK), colspec(OFF_RV)]
    args = [decay.reshape(1, 2, RET_H), norm.reshape(1, -1), z, z, z]
    if s0 is not None:
        in_specs.append(pl.BlockSpec((1, 1, 2, RET_HPS, RET_DK, RET_DV), lambda bi, g: (bi, layer, 0, g, 0, 0)))
        args.append(s0)
    in_specs.append(colspec(OFF_GP + BR_W))
    args.append(z)
    out_specs = [pl.BlockSpec((1, s, w), lambda bi, g: (bi, 0, g))]
    out_shape = [jax.ShapeDtypeStruct((b, s, RET_H * RET_DV), BF16)]
    if want_state:
        out_specs.append(pl.BlockSpec((1, 2, RET_HPS, RET_DK, RET_DV), lambda bi, g: (bi, 0, g, 0, 0)))
        out_shape.append(jax.ShapeDtypeStruct((b, 2, RET_H, RET_DK, RET_DV), F32))
    res = pl.pallas_call(
        functools.partial(_ret_body, nc=nc, has_s0=s0 is not None, has_st=want_state),
        grid=(b, RET_H // RET_HPS),
        in_specs=in_specs,
        out_specs=out_specs,
        out_shape=out_shape,
        scratch_shapes=[pltpu.VMEM((s, w), F32), pltpu.VMEM((RET_HPS, 2, nc, RET_DK, RET_DV), F32)],
        compiler_params=_cp(("parallel", "parallel"), 56),
        name="retention",
    )(*args)
    return (res[0], res[1]) if want_state else (res[0], None)


def _nat_body(q_ref, k_ref, v_ref, kc_ref, vc_ref, bias_ref, gp_ref, o_ref, kb_scr, vb_scr, o_scr, *, rows):
    scale = NAT_D ** -0.5
    kb_scr[...] = k_ref[0].astype(BF16)
    vb_scr[...] = v_ref[0].astype(BF16)
    kc = kc_ref[0, 0].astype(BF16)
    vc = vc_ref[0, 0].astype(BF16)
    nq = NAT_QR * GRID_W
    win = NAT_WR * GRID_W
    nblk = rows // NAT_QR

    def step(t, _):
        r0 = t * NAT_QR
        ws = jnp.clip(r0 - NAT_ROWS // 2, 0, rows - NAT_WR)
        pat = jnp.where(t == 0, 0, jnp.where(t == nblk - 1, 2, 1))
        qoff = pl.multiple_of(r0 * GRID_W, nq)
        koff = pl.multiple_of(ws * GRID_W, GRID_W)
        q = q_ref[0, pl.ds(qoff, nq), :].astype(BF16)
        kw = kb_scr[pl.ds(koff, win), :]
        vw = vb_scr[pl.ds(koff, win), :]
        s_loc = _dot_nt(q, kw) * scale + bias_ref[0, 0, pat]
        s_ctx = _dot_nt(q, kc) * scale
        m = jnp.maximum(jnp.max(s_loc, axis=-1, keepdims=True), jnp.max(s_ctx, axis=-1, keepdims=True))
        p_loc = jnp.exp(s_loc - m)
        p_ctx = jnp.exp(s_ctx - m)
        den = jnp.sum(p_loc, axis=-1, keepdims=True) + jnp.sum(p_ctx, axis=-1, keepdims=True)
        o = _dot(p_loc.astype(BF16), vw) + _dot(p_ctx.astype(BF16), vc)
        o_scr[pl.ds(qoff, nq), :] = o / den
        return 0

    lax.fori_loop(0, nblk, step, 0, unroll=4)
    o_ref[0] = (o_scr[...] * _silu(gp_ref[0])).astype(BF16)


def _nat_bias_tables(rpb, rows):
    depth, n_h = rpb.shape[:2]
    c = jnp.arange(GRID_W)[:, None]
    kc = jnp.arange(GRID_W)[None, :]
    cs = jnp.clip(c - NAT_COLS // 2, 0, GRID_W - NAT_COLS)
    valid = (kc >= cs) & (kc < cs + NAT_COLS)
    onehot = ((kc - c + (NAT_COLS - 1))[:, :, None] == jnp.arange(2 * NAT_COLS - 1)[None, None, :]).astype(F32)
    toep = jnp.einsum("dhrj,ckj->dhrck", rpb.astype(F32), onehot, precision=lax.Precision.HIGHEST)
    toep = jnp.where(valid, toep, NEG_INF)
    neg = jnp.full((depth, n_h, GRID_W, GRID_W), NEG_INF, F32)
    pats = []
    for r0, ws in ((0, 0), (NAT_QR, 0), (rows - NAT_QR, rows - NAT_WR)):
        qrows = []
        for qi in range(NAT_QR):
            r = r0 + qi
            rs = min(max(r - NAT_ROWS // 2, 0), rows - NAT_ROWS)
            blocks = []
            for wm in range(NAT_WR):
                kr = ws + wm
                blocks.append(toep[:, :, kr - r + NAT_ROWS - 1] if rs <= kr < rs + NAT_ROWS else neg)
            qrows.append(jnp.concatenate(blocks, axis=-1))
        pats.append(jnp.concatenate(qrows, axis=-2))
    return jnp.stack(pats, axis=2)


def _nat(z, cache_k, cache_v, layer, bias):
    b, s, _ = z.shape
    rows = s // GRID_W
    assert rows % NAT_QR == 0 and rows >= 2 * NAT_WR - NAT_ROWS
    lc = cache_k.shape[2]
    colspec = lambda blk: pl.BlockSpec((1, s, NAT_D), lambda bi, h: (bi, 0, blk + h))
    cspec = pl.BlockSpec((1, 1, lc, NAT_D), lambda bi, h: (bi, layer, 0, h))
    return pl.pallas_call(
        functools.partial(_nat_body, rows=rows),
        grid=(b, NAT_H),
        in_specs=[colspec(OFF_NQ // 128), colspec(OFF_NK // 128), colspec(OFF_NV // 128), cspec, cspec,
                  pl.BlockSpec((1, 1) + bias.shape[2:], lambda bi, h: (layer, h, 0, 0, 0)),
                  colspec((OFF_GP + 2 * BR_W) // 128)],
        out_specs=pl.BlockSpec((1, s, NAT_D), lambda bi, h: (bi, 0, h)),
        out_shape=jax.ShapeDtypeStruct((b, s, NAT_H * NAT_D), BF16),
        scratch_shapes=[pltpu.VMEM((s, NAT_D), BF16), pltpu.VMEM((s, NAT_D), BF16),
                        pltpu.VMEM((s, NAT_D), F32)],
        compiler_params=_cp(("parallel", "parallel"), 48),
        name="nat_attn",
    )(z, z, z, cache_k, cache_v, bias, z)


def _swa_values_with_ones(v):
    lane = lax.broadcasted_iota(jnp.int32, v.shape, 1)
    halves = [jnp.where(lane < SWA_D, v if g == 0 else pltpu.roll(v, (SWA_KVH - g) * SWA_D, 1), 1.0)
              for g in range(SWA_KVH)]
    return jnp.concatenate(halves, axis=-1).astype(BF16)


def _swa_body(q_ref, k_ref, v_ref, kc_ref, vc_ref, sink_ref, cos_ref, sin_ref, gp_ref, o_ref,
              kr_scr, vl_scr, kc_scr, vc_scr, *, seq):
    scale = SWA_D ** -0.5
    group = SWA_H // SWA_KVH
    win = 3 * BLK
    i = pl.program_id(1)

    @pl.when(i == 0)
    def _():
        k = k_ref[0]
        kr_scr[...] = (k * cos_ref[...] + _swap_halves(k) * sin_ref[...]).astype(BF16)
        vl_scr[...] = _swa_values_with_ones(v_ref[0])
        kc_scr[...] = kc_ref[0, 0].astype(BF16)
        vc_scr[...] = _swa_values_with_ones(vc_ref[0, 0])

    start = pl.multiple_of(jnp.clip((i - 1) * BLK, 0, seq - win), BLK)
    qoff = pl.multiple_of(i * BLK, BLK)
    cq = cos_ref[pl.ds(qoff, BLK), :]
    sq = sin_ref[pl.ds(qoff, BLK), :]
    q = q_ref[0]
    q = (q * jnp.concatenate([cq] * (SWA_H // 2), axis=-1)
         + _swap_halves(q) * jnp.concatenate([sq] * (SWA_H // 2), axis=-1))
    q = q * scale
    kw = kr_scr[pl.ds(start, win), :]
    vw = vl_scr[pl.ds(start, win), :]

    qpos = qoff + lax.broadcasted_iota(jnp.int32, (BLK, win), 0)
    kpos = start + lax.broadcasted_iota(jnp.int32, (BLK, win), 1)
    valid = jnp.abs(qpos - kpos) <= SWA_WINDOW
    valid = jnp.concatenate([valid] * group, axis=0)

    outs = []
    for g in range(SWA_KVH):
        heads = range(g * group, (g + 1) * group)
        qg = jnp.concatenate([q[:, h * SWA_D:(h + 1) * SWA_D] for h in heads], axis=0).astype(BF16)
        sink = jnp.concatenate([jnp.broadcast_to(sink_ref[0:1, h:h + 1], (BLK, 1)) for h in heads], axis=0)
        ksl = slice(g * SWA_D, (g + 1) * SWA_D)
        vsl = slice(g * 2 * SWA_D, (g + 1) * 2 * SWA_D)
        s = jnp.concatenate([jnp.where(valid, _dot_nt(qg, kw[:, ksl]), NEG_INF),
                             _dot_nt(qg, kc_scr[:, ksl])], axis=-1)
        m = jnp.maximum(jnp.max(s, axis=-1, keepdims=True), sink)
        p = jnp.exp(s - m).astype(BF16)
        pv = _dot(p[:, :win], vw[:, vsl]) + _dot(p[:, win:], vc_scr[:, vsl])
        o = pv[:, :SWA_D] / (pv[:, SWA_D:SWA_D + 1] + jnp.exp(sink - m))
        outs += [o[t * BLK:(t + 1) * BLK] for t in range(group)]
    o = jnp.concatenate(outs, axis=-1)
    o_ref[0] = (o * _silu(gp_ref[0])).astype(BF16)


def _swa(z, cache_k, cache_v, layer, sink, cos2, sin2):
    b, s, _ = z.shape
    lc = cache_k.shape[2]
    hd = SWA_H * SWA_D
    kd = SWA_KVH * SWA_D
    cspec = pl.BlockSpec((1, 1, lc, kd), lambda bi, i: (bi, layer, 0, 0))
    tq = BLK
    return pl.pallas_call(
        functools.partial(_swa_body, seq=s),
        grid=(b, s // tq),
        in_specs=[pl.BlockSpec((1, tq, hd), lambda bi, i: (bi, i, OFF_SQ // hd)),
                  pl.BlockSpec((1, s, kd), lambda bi, i: (bi, 0, OFF_SK // kd)),
                  pl.BlockSpec((1, s, kd), lambda bi, i: (bi, 0, OFF_SV // kd)),
                  cspec, cspec,
                  pl.BlockSpec((1, SWA_H), lambda bi, i: (0, 0)),
                  pl.BlockSpec((s, 128), lambda bi, i: (0, 0)),
                  pl.BlockSpec((s, 128), lambda bi, i: (0, 0)),
                  pl.BlockSpec((1, tq, hd), lambda bi, i: (bi, i, (OFF_GP + 3 * BR_W) // hd))],
        out_specs=pl.BlockSpec((1, tq, hd), lambda bi, i: (bi, i, 0)),
        out_shape=jax.ShapeDtypeStruct((b, s, hd), BF16),
        scratch_shapes=[pltpu.VMEM((s, kd), BF16), pltpu.VMEM((s, 2 * kd), BF16),
                        pltpu.VMEM((lc, kd), BF16), pltpu.VMEM((lc, 2 * kd), BF16)],
        compiler_params=_cp(("parallel", "arbitrary"), 48),
        name="swa_attn",
    )(z, z, z, cache_k, cache_v, sink.reshape(1, SWA_H), cos2, sin2, z)


MERGE_SUB = 2


def _merge_body(h_ref, o0_ref, o1_ref, o2_ref, o3_ref, g0_ref, g1_ref, g2_ref, g3_ref, wb_ref, y_ref):
    h = h_ref[...]
    o_refs = (o0_ref, o1_ref, o2_ref, o3_ref)
    g_refs = (g0_ref, g1_ref, g2_ref, g3_ref)
    tsub = y_ref.shape[1] // MERGE_SUB
    for c in range(MERGE_SUB):
        cols = slice(c * tsub, (c + 1) * tsub)
        y = None
        for n in range(N_BRANCH):
            t = _sigmoid(_dot_nt(h, g_refs[n][cols, :])) * _dot(o_refs[n][...], wb_ref[n, :, cols])
            y = t if y is None else y + t
        y_ref[:, cols] = y.astype(BF16)


def _merge(h, ogs, w_gate_t, w_branch, layer, tm, tn):
    m, d = h.shape
    tm = min(tm, m)
    tn = min(tn, d)
    nj = d // tn
    gspec = lambda n: pl.BlockSpec((None, tn, d), lambda i, j: (layer, n * nj + j, 0))
    return pl.pallas_call(
        _merge_body,
        grid=(m // tm, nj),
        in_specs=[pl.BlockSpec((tm, d), lambda i, j: (i, 0))]
        + [pl.BlockSpec((tm, BR_W), lambda i, j: (i, 0))] * N_BRANCH
        + [gspec(n) for n in range(N_BRANCH)]
        + [pl.BlockSpec((None, N_BRANCH, BR_W, tn), lambda i, j: (layer, 0, 0, j))],
        out_specs=pl.BlockSpec((tm, tn), lambda i, j: (i, j)),
        out_shape=jax.ShapeDtypeStruct((m, d), BF16),
        compiler_params=_cp(("parallel", "arbitrary"), 56),
        name="merge",
    )(h, *ogs, w_gate_t, w_gate_t, w_gate_t, w_gate_t, w_branch)


OUT_SUB = 2


def _out_body(*refs, has_next):
    if has_next:
        y_ref, w_ref, x_ref, gate_ref, g_ref, gn_ref, sh_ref, sc_ref, o_ref, h_ref = refs
    else:
        y_ref, w_ref, x_ref, gate_ref, g_ref, o_ref = refs
    ts = y_ref.shape[1]
    tsub = ts // OUT_SUB
    for sub in range(OUT_SUB):
        rows = slice(sub * tsub, (sub + 1) * tsub)
        t = _dot(y_ref[0, rows, :], w_ref[...])
        t = t * lax.rsqrt(jnp.mean(t * t, axis=-1, keepdims=True) + EPS) * g_ref[...]
        x = x_ref[0, rows, :] + gate_ref[0] * t
        o_ref[0, rows, :] = x
        if has_next:
            y = x * lax.rsqrt(jnp.mean(x * x, axis=-1, keepdims=True) + EPS)
            y = y * gn_ref[...]
            h_ref[0, rows, :] = (y * (1.0 + sc_ref[0]) + sh_ref[0]).astype(BF16)


def _out_proj(y, w_out, layer, x, gate, g, nxt):
    b, s, d = x.shape
    ts = min(s, 512)
    per_b = gate.shape[0] > 1
    cidx = (lambda bi, i: (bi, 0, 0)) if per_b else (lambda bi, i: (0, 0, 0))
    tile = pl.BlockSpec((1, ts, d), lambda bi, i: (bi, i, 0))
    vec = pl.BlockSpec((1, d), lambda bi, i: (0, 0))
    in_specs = [tile, pl.BlockSpec((None, d, d), lambda bi, i: (layer, 0, 0), pipeline_mode=pl.Buffered(1)),
                tile, pl.BlockSpec((1, 1, d), cidx), vec]
    args = [y, w_out, x, gate, g.reshape(1, d)]
    out_specs = [tile]
    out_shape = [jax.ShapeDtypeStruct((b, s, d), F32)]
    if nxt is not None:
        gn, shift, scale = nxt
        in_specs += [vec, pl.BlockSpec((1, 1, d), cidx), pl.BlockSpec((1, 1, d), cidx)]
        args += [gn.reshape(1, d), shift, scale]
        out_specs.append(tile)
        out_shape.append(jax.ShapeDtypeStruct((b, s, d), BF16))
    res = pl.pallas_call(
        functools.partial(_out_body, has_next=nxt is not None),
        grid=(b, s // ts),
        in_specs=in_specs,
        out_specs=out_specs,
        out_shape=out_shape,
        compiler_params=_cp(("parallel", "parallel"), 56),
        name="out_proj",
    )(*args)
    return (res[0], res[1]) if nxt is not None else (res[0], None)


_MIX_SRC = ((0, 512), (832, 4416), (4672, 6720), (512, 768), (4416, 4672), (768, 832))
_MIX_COLS = 6720


RELAYOUT_ROWS = 256
RELAYOUT_ALIGN = 64


def _relayout_body(off_ref, valid_ref, w_ref, o_ref):
    j = pl.program_id(1)
    row = lax.broadcasted_iota(jnp.int32, w_ref.shape, 0)
    o_ref[...] = jnp.where(row < valid_ref[j], w_ref[...], 0.0).astype(BF16)


def _relayout_rows(w_t, offs, valid):
    depth, _, d = w_t.shape
    nblk = len(offs)
    grid_spec = pltpu.PrefetchScalarGridSpec(
        num_scalar_prefetch=2,
        grid=(depth, nblk),
        in_specs=[pl.BlockSpec((pl.Element(1), pl.Element(RELAYOUT_ROWS), pl.Element(d)),
                               lambda l, j, off, val: (l, off[j] * RELAYOUT_ALIGN, 0))],
        out_specs=pl.BlockSpec((1, RELAYOUT_ROWS, d), lambda l, j, off, val: (l, j, 0)),
    )
    return pl.pallas_call(
        _relayout_body,
        grid_spec=grid_spec,
        out_shape=jax.ShapeDtypeStruct((depth, nblk * RELAYOUT_ROWS, d), BF16),
        compiler_params=_cp(("parallel", "arbitrary"), 32),
        name="relayout_w_in",
    )(jnp.asarray([o // RELAYOUT_ALIGN for o in offs], jnp.int32), jnp.asarray(valid, jnp.int32), w_t)


def _relayout_w_in(w_in):
    n = w_in.shape[2]
    w_t = jnp.swapaxes(w_in, 1, 2)
    r = RELAYOUT_ROWS
    offs, valid = [], []
    for a, b in _MIX_SRC:
        assert len(offs) * r == sum(valid)
        for o in range(a, b, r):
            offs.append(o)
            valid.append(min(r, b - o))
    while len(offs) * r < N_MIX:
        offs.append(0)
        valid.append(0)
    assert len(offs) * r == N_MIX and sum(valid) == _MIX_COLS
    w_mix_t = _relayout_rows(w_t, offs, valid)
    ng = (n - _MIX_COLS) // r
    w_gate_t = _relayout_rows(w_t, [_MIX_COLS + r * j for j in range(ng)], [r] * ng)
    return w_mix_t, w_gate_t


def _mla_weights(w_q_up, w_kv_up):
    depth = w_q_up.shape[0]
    wq = w_q_up.reshape(depth, MLA_Q_LORA, MLA_H, MLA_NOPE + MLA_ROPE)
    wq = jnp.pad(wq, ((0, 0), (0, 0), (0, 0), (0, MLA_DQ - MLA_NOPE - MLA_ROPE)))
    wq = wq.reshape(depth, MLA_Q_LORA, MLA_H * MLA_DQ).astype(BF16)
    wkv = w_kv_up.reshape(depth, MLA_KV_LORA, MLA_H, MLA_NOPE + MLA_V)
    wk = jnp.pad(wkv[..., :MLA_NOPE], ((0, 0), (0, 0), (0, 0), (0, MLA_DQ - MLA_NOPE)))
    wk = wk.reshape(depth, MLA_KV_LORA, MLA_H * MLA_DQ).astype(BF16)
    wv = jnp.pad(wkv[..., MLA_NOPE:], ((0, 0), (0, 0), (0, 0), (0, MLA_DV - MLA_V)))
    wv = wv.reshape(depth, MLA_KV_LORA, MLA_H * MLA_DV).astype(BF16)
    j = jnp.arange(128)[:, None]
    cidx = jnp.arange(MLA_H * MLA_DQ)[None, :]
    p128 = ((cidx % MLA_DQ == MLA_NOPE + j) & (j < MLA_ROPE)).astype(BF16)
    vones = (jnp.arange(MLA_H * MLA_DV)[None, :] % MLA_DV == MLA_V).astype(F32)
    return wq, wk, wv, p128, vones


def _rope_tables(n_tok):
    pos = jnp.arange(n_tok)
    row = (pos // GRID_W).astype(F32)
    col = (pos % GRID_W).astype(F32)
    n_freq = MLA_ROPE // 4
    inv = ROPE_BASE ** (-jnp.arange(n_freq, dtype=F32) / n_freq)
    ang = jnp.concatenate([row[:, None] * inv[None], col[:, None] * inv[None]], axis=-1)
    cos, sin = jnp.cos(ang), jnp.sin(ang)
    c64 = jnp.concatenate([cos, cos], axis=-1)
    s64 = jnp.concatenate([-sin, sin], axis=-1)
    one, zero = jnp.ones_like(c64), jnp.zeros_like(c64)
    cq = jnp.concatenate([one, one, c64, one], axis=-1)
    sq = jnp.concatenate([zero, zero, s64, zero], axis=-1)
    ck = jnp.concatenate([c64, one], axis=-1)
    sk = jnp.concatenate([s64, zero], axis=-1)
    c2 = jnp.concatenate([c64, c64], axis=-1)
    s2 = jnp.concatenate([s64, s64], axis=-1)
    return (cq, sq, ck, sk), (c2, s2)


TM_IN = 2048
TN_IN = 768
TM_MERGE = 1024
TN_MERGE = 512


def kernel(x_prompt, x_sample, cache_mla_ckv, cache_mla_krope, state_ret, cache_nat_k, cache_nat_v,
           cache_swa_k, cache_swa_v, c, c_ctx, w_mod, b_mod, norm_pre, norm_post, w_in, mla_q_norm,
           mla_kv_norm, mla_w_q_up, mla_w_kv_up, ret_decay, ret_norm, nat_rpb, swa_sink, w_branch, w_out):
    depth = w_mod.shape[0]
    d = x_prompt.shape[-1]
    nb = x_sample.shape[0]
    s_lat = x_sample.shape[1]
    l_ctx = cache_mla_ckv.shape[2]

    w_mix, w_gate = _relayout_w_in(w_in)
    w_br = w_branch.astype(BF16)
    w_o = w_out.astype(BF16)
    wq, wk, wv, p128, vones = _mla_weights(mla_w_q_up, mla_w_kv_up)
    p64 = p128[:MLA_ROPE]
    mla_tabs, (cos2, sin2) = _rope_tables(s_lat)
    nat_bias = _nat_bias_tables(nat_rpb, s_lat // GRID_W)
    nat_k_c = cache_nat_k.reshape(nb, depth, l_ctx, NAT_H * NAT_D)
    nat_v_c = cache_nat_v.reshape(nb, depth, l_ctx, NAT_H * NAT_D)
    swa_k_c = cache_swa_k.reshape(nb, depth, l_ctx, SWA_KVH * SWA_D)
    swa_v_c = cache_swa_v.reshape(nb, depth, l_ctx, SWA_KVH * SWA_D)

    rows = 8
    conds = jnp.concatenate([c_ctx[None, :], c, jnp.zeros((rows - 1 - nb, d), F32)], axis=0)
    mod = _modulation(conds, w_mod, b_mod)

    def mod_parts(l, lo, hi):
        m = mod[l, lo:hi]
        return m[:, None, :d], m[:, None, d:2 * d], m[:, None, 2 * d:]

    def finish_layer(l, lo, hi, x, h, ogs):
        b, s, _ = x.shape
        gate = mod_parts(l, lo, hi)[2]
        y = _merge(h.reshape(b * s, d), [o.reshape(b * s, BR_W) for o in ogs], w_gate, w_br, l, TM_MERGE, TN_MERGE)
        nxt = None
        if l + 1 < depth:
            shift_n, scale_n, _ = mod_parts(l + 1, lo, hi)
            nxt = (norm_pre[l + 1], shift_n, scale_n)
        return _out_proj(y.reshape(b, s, d), w_o, l, x, gate, norm_post[l], nxt)

    def start_pass(x, lo, hi):
        shift, scale, _ = mod_parts(0, lo, hi)
        return _modnorm(x, norm_pre[0], shift, scale)

    def in_proj(x, h, l):
        b, s, _ = x.shape
        return _in_proj(h.reshape(b * s, d), w_mix, l, TM_IN, TN_IN).reshape(b, s, N_MIX)

    yp = x_prompt
    bc, sc, _ = yp.shape
    ckv_l, kr_l, st_l, nk_l, nv_l, sk_l, sv_l = [], [], [], [], [], [], []
    h = start_pass(yp, 0, 1)
    for l in range(depth):
        z = in_proj(yp, h, l)
        q, kf, vf, ckv = _mla_prep(z, mla_q_norm[l], mla_kv_norm[l], wq[l], wk[l], wv[l], p128, vones, None)
        og_mla, og_nat, og_swa = _ctx_attn(q, kf, vf, z, swa_sink[l])
        og_ret, st = _retention(z, ret_decay[l], ret_norm[l], None, 0, True)
        yp, h = finish_layer(l, 0, 1, yp, h, (og_mla, og_ret, og_nat, og_swa))
        ckv_l.append(ckv)
        kr_l.append(z[:, :, OFF_KR:OFF_KR + MLA_ROPE])
        st_l.append(st)
        nk_l.append(z[:, :, OFF_NK:OFF_NK + 512].reshape(bc, sc, NAT_H, NAT_D))
        nv_l.append(z[:, :, OFF_NV:OFF_NV + 512].reshape(bc, sc, NAT_H, NAT_D))
        sk_l.append(z[:, :, OFF_SK:OFF_SK + 128].reshape(bc, sc, SWA_KVH, SWA_D))
        sv_l.append(z[:, :, OFF_SV:OFF_SV + 128].reshape(bc, sc, SWA_KVH, SWA_D))

    ys = x_sample
    h = start_pass(ys, 1, 1 + nb)
    for l in range(depth):
        z = in_proj(ys, h, l)
        q, kf, vf, _ = _mla_prep(z, mla_q_norm[l], mla_kv_norm[l], wq[l], wk[l], wv[l], p128, vones, mla_tabs)
        kc, vc = _mla_expand(cache_mla_ckv, cache_mla_krope, l, wk[l], wv[l], p64, vones)
        og_mla = _mla_attn(q, kf, vf, z, OFF_GP // BR_W, ctx=(kc, vc))
        og_ret, _ = _retention(z, ret_decay[l], ret_norm[l], state_ret, l, False)
        og_nat = _nat(z, nat_k_c, nat_v_c, l, nat_bias)
        og_swa = _swa(z, swa_k_c, swa_v_c, l, swa_sink[l], cos2, sin2)
        ys, h = finish_layer(l, 1, 1 + nb, ys, h, (og_mla, og_ret, og_nat, og_swa))

    return (yp, ys, jnp.stack(ckv_l, axis=1), jnp.stack(kr_l, axis=1), jnp.stack(st_l, axis=1),
            jnp.stack(nk_l, axis=1), jnp.stack(nv_l, axis=1), jnp.stack(sk_l, axis=1), jnp.stack(sv_l, axis=1))
```

```python
import functools

import jax
import jax.numpy as jnp
from jax import lax
from jax.experimental import pallas as pl
from jax.experimental.pallas import tpu as pltpu

F32 = jnp.float32
BF16 = jnp.bfloat16

GRID_W = 64
BLK = 128
ROPE_BASE = 10000.0
EPS = 1e-6
NEG_INF = -1e30
N_BRANCH = 4
BR_W = 512

MLA_H = 4
MLA_Q_LORA = 512
MLA_KV_LORA = 256
MLA_NOPE = 128
MLA_ROPE = 64
MLA_V = 128
MLA_DQ = 256
MLA_DV = 256
MLA_QSCALE = (MLA_NOPE + MLA_ROPE) ** -0.5 * 1.4426950408889634

RET_H = 4
RET_DK = 128
RET_DV = 128
RET_HPS = 2

NAT_H = 4
NAT_D = 128
NAT_ROWS = 8
NAT_COLS = 16
NAT_QR = 4
NAT_WR = NAT_ROWS + NAT_QR

SWA_H = 8
SWA_KVH = 2
SWA_D = 64
SWA_WINDOW = 128

OFF_QA = 0
OFF_RQ = 512
OFF_RK = 1024
OFF_RV = 1536
OFF_NQ = 2048
OFF_NK = 2560
OFF_NV = 3072
OFF_SQ = 3584
OFF_GP = 4096
OFF_KVA = 6144
OFF_SK = 6400
OFF_SV = 6528
OFF_KR = 6656
N_MIX_USED = 6784
N_MIX = 6912

V7X_VMEM_BYTES = 64 * 1024 * 1024


def _cp(sem, vmem_mb):
    assert vmem_mb * 1024 * 1024 < V7X_VMEM_BYTES
    return pltpu.CompilerParams(dimension_semantics=sem, vmem_limit_bytes=vmem_mb * 1024 * 1024)


def _sigmoid(x):
    return 1.0 / (1.0 + jnp.exp(-x))


def _silu(x):
    return x * _sigmoid(x)


def _swap_halves(x):
    n = x.shape[-1]
    ax = x.ndim - 1
    lane = lax.broadcasted_iota(jnp.int32, x.shape, ax)
    return jnp.where((lane & 32) == 0, pltpu.roll(x, n - 32, ax), pltpu.roll(x, 32, ax))


def _dot_nt(a, b):
    return lax.dot_general(a, b, (((1,), (1,)), ((), ())), preferred_element_type=F32)


def _dot(a, b):
    return jnp.dot(a, b, preferred_element_type=F32)


def _mod_body(c_ref, w_ref, b_ref, o_ref):
    a = _silu(c_ref[...]).astype(BF16)
    o_ref[0] = _dot(a, w_ref[0].astype(BF16)) + b_ref[0]


def _modulation(conds, w_mod, b_mod):
    depth, d, n = w_mod.shape
    rows = conds.shape[0]
    tn = 1024 if n % 1024 == 0 else n
    return pl.pallas_call(
        _mod_body,
        grid=(depth, n // tn),
        in_specs=[pl.BlockSpec((rows, d), lambda l, j: (0, 0)),
                  pl.BlockSpec((1, d, tn), lambda l, j: (l, 0, j)),
                  pl.BlockSpec((1, 1, tn), lambda l, j: (l, 0, j))],
        out_specs=pl.BlockSpec((1, rows, tn), lambda l, j: (l, 0, j)),
        out_shape=jax.ShapeDtypeStruct((depth, rows, n), F32),
        compiler_params=_cp(("parallel", "parallel"), 40),
        name="modulation",
    )(conds, w_mod, b_mod.reshape(depth, 1, n))


def _modnorm_body(x_ref, g_ref, sh_ref, sc_ref, h_ref):
    x = x_ref[0]
    y = x * lax.rsqrt(jnp.mean(x * x, axis=-1, keepdims=True) + EPS)
    y = y * g_ref[...]
    h_ref[0] = (y * (1.0 + sc_ref[0]) + sh_ref[0]).astype(BF16)


def _modnorm(x, g, shift, scale):
    b, s, d = x.shape
    ts = min(s, 512)
    per_b = shift.shape[0] > 1
    cidx = (lambda bi, i: (bi, 0, 0)) if per_b else (lambda bi, i: (0, 0, 0))
    return pl.pallas_call(
        _modnorm_body,
        grid=(b, s // ts),
        in_specs=[pl.BlockSpec((1, ts, d), lambda bi, i: (bi, i, 0)),
                  pl.BlockSpec((1, d), lambda bi, i: (0, 0)),
                  pl.BlockSpec((1, 1, d), cidx),
                  pl.BlockSpec((1, 1, d), cidx)],
        out_specs=pl.BlockSpec((1, ts, d), lambda bi, i: (bi, i, 0)),
        out_shape=jax.ShapeDtypeStruct((b, s, d), BF16),
        compiler_params=_cp(("parallel", "parallel"), 32),
        name="modnorm",
    )(x, g.reshape(1, d), shift, scale)


def _mm_body(a_ref, b_ref, o_ref):
    o_ref[...] = _dot_nt(a_ref[...], b_ref[...])


def _in_proj(a, w_mix_t, layer, tm, tn):
    m, k = a.shape
    tm = min(tm, m)
    return pl.pallas_call(
        _mm_body,
        grid=(m // tm, N_MIX // tn),
        in_specs=[pl.BlockSpec((tm, k), lambda i, j: (i, 0)),
                  pl.BlockSpec((None, tn, k), lambda i, j: (layer, j, 0))],
        out_specs=pl.BlockSpec((tm, tn), lambda i, j: (i, j)),
        out_shape=jax.ShapeDtypeStruct((m, N_MIX), F32),
        compiler_params=_cp(("parallel", "arbitrary"), 48),
        name="in_proj",
    )(a, w_mix_t)


def _mla_prep_body(*refs, rope):
    if rope:
        (qa_ref, kva_ref, kr_ref, gq_ref, gkv_ref, wq_ref, wk_ref, wv_ref, p_ref, one_ref,
         cq_ref, sq_ref, ck_ref, sk_ref, q_out, k_out, v_out, ckv_out) = refs
    else:
        (qa_ref, kva_ref, kr_ref, gq_ref, gkv_ref, wq_ref, wk_ref, wv_ref, p_ref, one_ref,
         q_out, k_out, v_out, ckv_out) = refs
    qa = qa_ref[0]
    cq = qa * lax.rsqrt(jnp.mean(qa * qa, axis=-1, keepdims=True) + EPS) * gq_ref[...]
    q = _dot(cq.astype(BF16), wq_ref[...])
    kva = kva_ref[0]
    ckv = kva * lax.rsqrt(jnp.mean(kva * kva, axis=-1, keepdims=True) + EPS) * gkv_ref[...]
    ckv_out[0] = ckv
    kr = kr_ref[0]
    if rope:
        cq_t = jnp.concatenate([cq_ref[...]] * MLA_H, axis=-1)
        sq_t = jnp.concatenate([sq_ref[...]] * MLA_H, axis=-1)
        q = q * cq_t + _swap_halves(q) * sq_t
        kr = kr * ck_ref[...] + _swap_halves(kr) * sk_ref[...]
    q_out[0] = (q * MLA_QSCALE).astype(BF16)
    cb = ckv.astype(BF16)
    k_out[0] = (_dot(cb, wk_ref[...]) + _dot(kr.astype(BF16), p_ref[...])).astype(BF16)
    v_out[0] = (_dot(cb, wv_ref[...]) + one_ref[...]).astype(BF16)


def _mla_prep(z, gq, gkv, wq, wk, wv, p128, vones, tabs):
    b, s, _ = z.shape
    tm = min(s, 512)
    rope = tabs is not None
    full = lambda shape: pl.BlockSpec(shape, lambda bi, i: (0,) * len(shape))
    in_specs = [pl.BlockSpec((1, tm, MLA_Q_LORA), lambda bi, i: (bi, i, OFF_QA // MLA_Q_LORA)),
                pl.BlockSpec((1, tm, MLA_KV_LORA), lambda bi, i: (bi, i, OFF_KVA // MLA_KV_LORA)),
                pl.BlockSpec((1, tm, 128), lambda bi, i: (bi, i, OFF_KR // 128)),
                full((1, MLA_Q_LORA)), full((1, MLA_KV_LORA)),
                full(wq.shape), full(wk.shape), full(wv.shape), full(p128.shape), full(vones.shape)]
    args = [z, z, z, gq.reshape(1, -1), gkv.reshape(1, -1), wq, wk, wv, p128, vones]
    if rope:
        cq, sq, ck, sk = tabs
        in_specs += [pl.BlockSpec((tm, MLA_DQ), lambda bi, i: (i, 0))] * 2
        in_specs += [pl.BlockSpec((tm, 128), lambda bi, i: (i, 0))] * 2
        args += [cq, sq, ck, sk]
    hq = MLA_H * MLA_DQ
    hv = MLA_H * MLA_DV
    return pl.pallas_call(
        functools.partial(_mla_prep_body, rope=rope),
        grid=(b, s // tm),
        in_specs=in_specs,
        out_specs=[pl.BlockSpec((1, tm, hq), lambda bi, i: (bi, i, 0)),
                   pl.BlockSpec((1, tm, hq), lambda bi, i: (bi, i, 0)),
                   pl.BlockSpec((1, tm, hv), lambda bi, i: (bi, i, 0)),
                   pl.BlockSpec((1, tm, MLA_KV_LORA), lambda bi, i: (bi, i, 0))],
        out_shape=[jax.ShapeDtypeStruct((b, s, hq), BF16),
                   jax.ShapeDtypeStruct((b, s, hq), BF16),
                   jax.ShapeDtypeStruct((b, s, hv), BF16),
                   jax.ShapeDtypeStruct((b, s, MLA_KV_LORA), F32)],
        compiler_params=_cp(("parallel", "parallel"), 40),
        name="mla_prep",
    )(*args)


def _mla_expand_body(ckv_ref, kr_ref, wk_ref, wv_ref, p_ref, one_ref, k_out, v_out):
    cb = ckv_ref[0, 0].astype(BF16)
    k_out[0] = (_dot(cb, wk_ref[...]) + _dot(kr_ref[0, 0].astype(BF16), p_ref[...])).astype(BF16)
    v_out[0] = (_dot(cb, wv_ref[...]) + one_ref[...]).astype(BF16)


def _mla_expand(cache_ckv, cache_kr, layer, wk, wv, p64, vones):
    b, _, s, _ = cache_ckv.shape
    full = lambda shape: pl.BlockSpec(shape, lambda bi: (0,) * len(shape))
    hq = MLA_H * MLA_DQ
    hv = MLA_H * MLA_DV
    return pl.pallas_call(
        _mla_expand_body,
        grid=(b,),
        in_specs=[pl.BlockSpec((1, 1, s, MLA_KV_LORA), lambda bi: (bi, layer, 0, 0)),
                  pl.BlockSpec((1, 1, s, MLA_ROPE), lambda bi: (bi, layer, 0, 0)),
                  full(wk.shape), full(wv.shape), full(p64.shape), full(vones.shape)],
        out_specs=[pl.BlockSpec((1, s, hq), lambda bi: (bi, 0, 0)),
                   pl.BlockSpec((1, s, hv), lambda bi: (bi, 0, 0))],
        out_shape=[jax.ShapeDtypeStruct((b, s, hq), BF16),
                   jax.ShapeDtypeStruct((b, s, hv), BF16)],
        compiler_params=_cp(("parallel",), 32),
        name="mla_expand",
    )(cache_ckv, cache_kr, wk, wv, p64, vones)


def _mla_attn_body(*refs, has_ctx, nkv):
    it = iter(refs)
    q_ref, k_ref, v_ref = next(it), next(it), next(it)
    kc_ref = vc_ref = None
    if has_ctx:
        kc_ref, vc_ref = next(it), next(it)
    gp_ref, o_ref, m_scr, acc_scr = next(it), next(it), next(it), next(it)
    j = pl.program_id(2)

    def process(kr, vr):
        tk = kr.shape[1]
        for h in range(MLA_H):
            qk = slice(h * MLA_DQ, (h + 1) * MLA_DQ)
            s = _dot_nt(q_ref[0, :, qk], kr[0, :, qk])
            m_prev = m_scr[h]
            m_new = jnp.maximum(m_prev, jnp.max(s, axis=-1, keepdims=True))
            alpha = jnp.exp2(m_prev - m_new)
            p = jnp.exp2(s - jnp.concatenate([m_new] * (tk // 128), axis=-1))
            pv = _dot(p.astype(BF16), vr[0, :, h * MLA_DV:h * MLA_DV + MLA_V])
            acc_scr[h, :, :MLA_V] = alpha * acc_scr[h, :, :MLA_V] + pv
            acc_scr[h, :, MLA_V:] = alpha * acc_scr[h, :, MLA_V:] + jnp.sum(p, axis=-1, keepdims=True)
            m_scr[h] = m_new

    @pl.when(j == 0)
    def _():
        m_scr[...] = jnp.full(m_scr.shape, NEG_INF, F32)
        acc_scr[...] = jnp.zeros(acc_scr.shape, F32)
        if has_ctx:
            process(kc_ref, vc_ref)

    process(k_ref, v_ref)

    @pl.when(j == nkv - 1)
    def _():
        outs = []
        for h in range(MLA_H):
            a = acc_scr[h]
            outs.append(a[:, :MLA_V] / a[:, MLA_V:MLA_V + 1])
        o_ref[0] = (jnp.concatenate(outs, axis=-1) * _silu(gp_ref[0])).astype(BF16)


def _mla_attn(q, k, v, gp, gpcol, ctx=None, tq=1024, tk=1024):
    b, sq, _ = q.shape
    sk = k.shape[1]
    tq = min(tq, sq)
    tk = min(tk, sk)
    nkv = sk // tk
    hq = MLA_H * MLA_DQ
    hv = MLA_H * MLA_DV
    in_specs = [pl.BlockSpec((1, tq, hq), lambda bi, i, j: (bi, i, 0)),
                pl.BlockSpec((1, tk, hq), lambda bi, i, j: (bi, j, 0)),
                pl.BlockSpec((1, tk, hv), lambda bi, i, j: (bi, j, 0))]
    args = [q, k, v]
    if ctx is not None:
        kc, vc = ctx
        in_specs += [pl.BlockSpec((1,) + kc.shape[1:], lambda bi, i, j: (bi, 0, 0)),
                     pl.BlockSpec((1,) + vc.shape[1:], lambda bi, i, j: (bi, 0, 0))]
        args += [kc, vc]
    in_specs += [pl.BlockSpec((1, tq, MLA_H * MLA_V), lambda bi, i, j: (bi, i, gpcol))]
    args += [gp]
    return pl.pallas_call(
        functools.partial(_mla_attn_body, has_ctx=ctx is not None, nkv=nkv),
        grid=(b, sq // tq, nkv),
        in_specs=in_specs,
        out_specs=pl.BlockSpec((1, tq, MLA_H * MLA_V), lambda bi, i, j: (bi, i, 0)),
        out_shape=jax.ShapeDtypeStruct((b, sq, MLA_H * MLA_V), BF16),
        scratch_shapes=[pltpu.VMEM((MLA_H, tq, 128), F32), pltpu.VMEM((MLA_H, tq, MLA_DV), F32)],
        compiler_params=_cp(("parallel", "parallel", "arbitrary"), 48),
        name="mla_attn",
    )(*args)


CTX_SPS = 2


def _ctx_attn_body(qm_ref, km_ref, vm_ref, nq_ref, nk_ref, nv_ref, sq_ref, sk_ref, sv_ref, sink_ref,
                   gpm_ref, gpn_ref, gps_ref, om_ref, on_ref, os_ref):
    for bb in range(qm_ref.shape[0]):
        view = lambda r: r.at[bb:bb + 1]
        _ctx_attn_one(view(qm_ref), view(km_ref), view(vm_ref), view(nq_ref), view(nk_ref), view(nv_ref),
                      view(sq_ref), view(sk_ref), view(sv_ref), sink_ref,
                      view(gpm_ref), view(gpn_ref), view(gps_ref), view(om_ref), view(on_ref), view(os_ref))


def _ctx_attn_one(qm_ref, km_ref, vm_ref, nq_ref, nk_ref, nv_ref, sq_ref, sk_ref, sv_ref, sink_ref,
                  gpm_ref, gpn_ref, gps_ref, om_ref, on_ref, os_ref):
    outs = []
    for h in range(MLA_H):
        qk = slice(h * MLA_DQ, (h + 1) * MLA_DQ)
        s = _dot_nt(qm_ref[0, :, qk], km_ref[0, :, qk])
        p = jnp.exp2(s - jnp.max(s, axis=-1, keepdims=True))
        pv = _dot(p.astype(BF16), vm_ref[0, :, h * MLA_DV:(h + 1) * MLA_DV])
        outs.append(pv[:, :MLA_V] / pv[:, MLA_V:MLA_V + 1])
    om_ref[0] = (jnp.concatenate(outs, axis=-1) * _silu(gpm_ref[0])).astype(BF16)

    outs = []
    for h in range(NAT_H):
        sl = slice(h * NAT_D, (h + 1) * NAT_D)
        s = _dot_nt(nq_ref[0, :, sl].astype(BF16), nk_ref[0, :, sl].astype(BF16)) * NAT_D ** -0.5
        p = jnp.exp(s - jnp.max(s, axis=-1, keepdims=True))
        den = jnp.sum(p, axis=-1, keepdims=True)
        outs.append(_dot(p.astype(BF16), nv_ref[0, :, sl].astype(BF16)) / den)
    on_ref[0] = (jnp.concatenate(outs, axis=-1) * _silu(gpn_ref[0])).astype(BF16)

    n_tok = sq_ref.shape[1]
    group = SWA_H // SWA_KVH
    q = sq_ref[0] * SWA_D ** -0.5
    outs = []
    for g in range(SWA_KVH):
        heads = range(g * group, (g + 1) * group)
        qg = jnp.concatenate([q[:, h * SWA_D:(h + 1) * SWA_D] for h in heads], axis=0).astype(BF16)
        sink = jnp.concatenate([jnp.broadcast_to(sink_ref[0:1, h:h + 1], (n_tok, 1)) for h in heads], axis=0)
        ksl = slice(g * SWA_D, (g + 1) * SWA_D)
        s = _dot_nt(qg, sk_ref[0, :, ksl].astype(BF16))
        m = jnp.maximum(jnp.max(s, axis=-1, keepdims=True), sink)
        p = jnp.exp(s - m)
        den = jnp.sum(p, axis=-1, keepdims=True) + jnp.exp(sink - m)
        o = _dot(p.astype(BF16), sv_ref[0, :, ksl].astype(BF16)) / den
        outs += [o[t * n_tok:(t + 1) * n_tok] for t in range(group)]
    os_ref[0] = (jnp.concatenate(outs, axis=-1) * _silu(gps_ref[0])).astype(BF16)


def _ctx_attn(q, k, v, z, sink):
    b, n_tok, _ = z.shape
    sps = CTX_SPS if b % CTX_SPS == 0 else 1
    blk = lambda w, off: pl.BlockSpec((sps, n_tok, w), lambda bi: (bi, 0, off // w))
    kd = SWA_KVH * SWA_D
    out = jax.ShapeDtypeStruct((b, n_tok, BR_W), BF16)
    return pl.pallas_call(
        _ctx_attn_body,
        grid=(b // sps,),
        in_specs=[blk(MLA_H * MLA_DQ, 0), blk(MLA_H * MLA_DQ, 0), blk(MLA_H * MLA_DV, 0),
                  blk(BR_W, OFF_NQ), blk(BR_W, OFF_NK), blk(BR_W, OFF_NV),
                  blk(BR_W, OFF_SQ), blk(kd, OFF_SK), blk(kd, OFF_SV),
                  pl.BlockSpec((1, SWA_H), lambda bi: (0, 0)),
                  blk(BR_W, OFF_GP), blk(BR_W, OFF_GP + 2 * BR_W), blk(BR_W, OFF_GP + 3 * BR_W)],
        out_specs=[blk(BR_W, 0)] * 3,
        out_shape=[out, out, out],
        compiler_params=_cp(("parallel",), 48),
        name="ctx_attn",
    )(q, k, v, z, z, z, z, z, z, sink.reshape(1, SWA_H), z, z, z)


def _ret_body(*refs, nc, has_s0, has_st):
    it = iter(refs)
    dec_ref, norm_ref, q_ref, k_ref, v_ref = next(it), next(it), next(it), next(it), next(it)
    s0_ref = next(it) if has_s0 else None
    gp_ref, o_ref = next(it), next(it)
    st_ref = next(it) if has_st else None
    o_scr, kv_scr = next(it), next(it)
    hgrp = pl.program_id(1)

    d = dec_ref[0]
    lg_all = jnp.minimum(d, 0.0) - jnp.log1p(jnp.exp(-jnp.abs(d)))
    hcol = lax.broadcasted_iota(jnp.int32, d.shape, 1)
    row = lax.broadcasted_iota(jnp.int32, (BLK, BLK), 0).astype(F32)
    col = lax.broadcasted_iota(jnp.int32, (BLK, BLK), 1).astype(F32)
    a = lax.broadcasted_iota(jnp.int32, (BLK, 1), 0).astype(F32)
    kscale = RET_DK ** -0.5

    heads = []
    for hh in range(RET_HPS):
        lg = jnp.sum(jnp.where(hcol == hgrp * RET_HPS + hh, lg_all, 0.0), axis=1, keepdims=True)
        lg_f, lg_b = lg[0:1, :], lg[1:2, :]
        dm = (jnp.where(row - col >= 0, jnp.exp(lg_f * jnp.maximum(row - col, 0.0)), 0.0)
              + jnp.where(col - row > 0, jnp.exp(lg_b * jnp.maximum(col - row, 0.0)), 0.0))
        fwd = (jnp.exp(lg_f * (a + 1.0)), jnp.exp(lg_f * (BLK - 1.0 - a)), jnp.exp(lg_f * BLK))
        bwd = (jnp.exp(lg_b * (BLK - a)), jnp.exp(lg_b * a), jnp.exp(lg_b * BLK))
        heads.append((hh, slice(hh * RET_DK, (hh + 1) * RET_DK), dm, fwd, bwd))

    def intra(c, _):
        off = pl.multiple_of(c * BLK, BLK)
        for hh, sl, dm, fwd, bwd in heads:
            q = q_ref[0, pl.ds(off, BLK), sl]
            k = k_ref[0, pl.ds(off, BLK), sl] * kscale
            v = v_ref[0, pl.ds(off, BLK), sl].astype(BF16)
            inner = _dot_nt(q.astype(BF16), k.astype(BF16)) * dm
            lhs = jnp.concatenate([inner.astype(BF16), jnp.transpose(k * fwd[1]).astype(BF16),
                                   jnp.transpose(k * bwd[1]).astype(BF16)], axis=0)
            r = _dot(lhs, v)
            o_scr[pl.ds(off, BLK), sl] = r[:BLK]
            kv_scr[hh, 0, c] = r[BLK:2 * BLK]
            kv_scr[hh, 1, c] = r[2 * BLK:]
        return 0

    lax.fori_loop(0, nc, intra, 0, unroll=4 if nc % 4 == 0 else 2)

    def scan(j, carry):
        cf = j
        cb = nc - 1 - j
        off_f = pl.multiple_of(cf * BLK, BLK)
        off_b = pl.multiple_of(cb * BLK, BLK)
        new = []
        for (hh, sl, _, fwd, bwd), (sf, sb) in zip(heads, carry):
            qf = (q_ref[0, pl.ds(off_f, BLK), sl] * fwd[0]).astype(BF16)
            o_scr[pl.ds(off_f, BLK), sl] += _dot(qf, sf.astype(BF16))
            sf = sf * fwd[2] + kv_scr[hh, 0, cf]
            qb = (q_ref[0, pl.ds(off_b, BLK), sl] * bwd[0]).astype(BF16)
            o_scr[pl.ds(off_b, BLK), sl] += _dot(qb, sb.astype(BF16))
            sb = sb * bwd[2] + kv_scr[hh, 1, cb]
            new.append((sf, sb))
        return tuple(new)

    if has_s0:
        init = tuple((s0_ref[0, 0, 0, hh], s0_ref[0, 0, 1, hh]) for hh in range(RET_HPS))
    else:
        zero = jnp.zeros((RET_DK, RET_DV), F32)
        init = tuple((zero, zero) for _ in range(RET_HPS))
    fin = lax.fori_loop(0, nc, scan, init, unroll=4 if nc % 4 == 0 else 2)
    if has_st:
        for hh in range(RET_HPS):
            st_ref[0, 0, hh] = fin[hh][0]
            st_ref[0, 1, hh] = fin[hh][1]
    for _, sl, _, _, _ in heads:
        o = o_scr[:, sl]
        o = o * lax.rsqrt(jnp.mean(o * o, axis=-1, keepdims=True) + EPS) * norm_ref[:, sl]
        o_ref[0, :, sl] = (o * _silu(gp_ref[0, :, sl])).astype(BF16)


def _retention(z, decay, norm, s0, layer, want_state):
    b, s, _ = z.shape
    nc = s // BLK
    w = RET_HPS * RET_DK
    colspec = lambda off: pl.BlockSpec((1, s, w), lambda bi, g: (bi, 0, off // w + g))
    in_specs = [pl.BlockSpec((1, 2, RET_H), lambda bi, g: (0, 0, 0)),
                pl.BlockSpec((1, w), lambda bi, g: (0, g)),
                colspec(OFF_RQ), colspec(OFF_RK), colspec(OFF_RV)]
    args = [decay.reshape(1, 2, RET_H), norm.reshape(1, -1), z, z, z]
    if s0 is not None:
        in_specs.append(pl.BlockSpec((1, 1, 2, RET_HPS, RET_DK, RET_DV), lambda bi, g: (bi, layer, 0, g, 0, 0)))
        args.append(s0)
    in_specs.append(colspec(OFF_GP + BR_W))
    args.append(z)
    out_specs = [pl.BlockSpec((1, s, w), lambda bi, g: (bi, 0, g))]
    out_shape = [jax.ShapeDtypeStruct((b, s, RET_H * RET_DV), BF16)]
    if want_state:
        out_specs.append(pl.BlockSpec((1, 2, RET_HPS, RET_DK, RET_DV), lambda bi, g: (bi, 0, g, 0, 0)))
        out_shape.append(jax.ShapeDtypeStruct((b, 2, RET_H, RET_DK, RET_DV), F32))
    res = pl.pallas_call(
        functools.partial(_ret_body, nc=nc, has_s0=s0 is not None, has_st=want_state),
        grid=(b, RET_H // RET_HPS),
        in_specs=in_specs,
        out_specs=out_specs,
        out_shape=out_shape,
        scratch_shapes=[pltpu.VMEM((s, w), F32), pltpu.VMEM((RET_HPS, 2, nc, RET_DK, RET_DV), F32)],
        compiler_params=_cp(("parallel", "parallel"), 56),
        name="retention",
    )(*args)
    return (res[0], res[1]) if want_state else (res[0], None)


def _nat_body(q_ref, k_ref, v_ref, kc_ref, vc_ref, bias_ref, gp_ref, o_ref, kb_scr, vb_scr, o_scr, *, rows):
    scale = NAT_D ** -0.5
    kb_scr[...] = k_ref[0].astype(BF16)
    vb_scr[...] = v_ref[0].astype(BF16)
    kc = kc_ref[0, 0].astype(BF16)
    vc = vc_ref[0, 0].astype(BF16)
    nq = NAT_QR * GRID_W
    win = NAT_WR * GRID_W
    nblk = rows // NAT_QR

    def step(t, _):
        r0 = t * NAT_QR
        ws = jnp.clip(r0 - NAT_ROWS // 2, 0, rows - NAT_WR)
        pat = jnp.where(t == 0, 0, jnp.where(t == nblk - 1, 2, 1))
        qoff = pl.multiple_of(r0 * GRID_W, nq)
        koff = pl.multiple_of(ws * GRID_W, GRID_W)
        q = q_ref[0, pl.ds(qoff, nq), :].astype(BF16)
        kw = kb_scr[pl.ds(koff, win), :]
        vw = vb_scr[pl.ds(koff, win), :]
        s_loc = _dot_nt(q, kw) * scale + bias_ref[0, 0, pat]
        s_ctx = _dot_nt(q, kc) * scale
        m = jnp.maximum(jnp.max(s_loc, axis=-1, keepdims=True), jnp.max(s_ctx, axis=-1, keepdims=True))
        p_loc = jnp.exp(s_loc - m)
        p_ctx = jnp.exp(s_ctx - m)
        den = jnp.sum(p_loc, axis=-1, keepdims=True) + jnp.sum(p_ctx, axis=-1, keepdims=True)
        o = _dot(p_loc.astype(BF16), vw) + _dot(p_ctx.astype(BF16), vc)
        o_scr[pl.ds(qoff, nq), :] = o / den
        return 0

    lax.fori_loop(0, nblk, step, 0, unroll=4)
    o_ref[0] = (o_scr[...] * _silu(gp_ref[0])).astype(BF16)


def _nat_bias_tables(rpb, rows):
    depth, n_h = rpb.shape[:2]
    c = jnp.arange(GRID_W)[:, None]
    kc = jnp.arange(GRID_W)[None, :]
    cs = jnp.clip(c - NAT_COLS // 2, 0, GRID_W - NAT_COLS)
    valid = (kc >= cs) & (kc < cs + NAT_COLS)
    onehot = ((kc - c + (NAT_COLS - 1))[:, :, None] == jnp.arange(2 * NAT_COLS - 1)[None, None, :]).astype(F32)
    toep = jnp.einsum("dhrj,ckj->dhrck", rpb.astype(F32), onehot, precision=lax.Precision.HIGHEST)
    toep = jnp.where(valid, toep, NEG_INF)
    neg = jnp.full((depth, n_h, GRID_W, GRID_W), NEG_INF, F32)
    pats = []
    for r0, ws in ((0, 0), (NAT_QR, 0), (rows - NAT_QR, rows - NAT_WR)):
        qrows = []
        for qi in range(NAT_QR):
            r = r0 + qi
            rs = min(max(r - NAT_ROWS // 2, 0), rows - NAT_ROWS)
            blocks = []
            for wm in range(NAT_WR):
                kr = ws + wm
                blocks.append(toep[:, :, kr - r + NAT_ROWS - 1] if rs <= kr < rs + NAT_ROWS else neg)
            qrows.append(jnp.concatenate(blocks, axis=-1))
        pats.append(jnp.concatenate(qrows, axis=-2))
    return jnp.stack(pats, axis=2)


def _nat(z, cache_k, cache_v, layer, bias):
    b, s, _ = z.shape
    rows = s // GRID_W
    assert rows % NAT_QR == 0 and rows >= 2 * NAT_WR - NAT_ROWS
    lc = cache_k.shape[2]
    colspec = lambda blk: pl.BlockSpec((1, s, NAT_D), lambda bi, h: (bi, 0, blk + h))
    cspec = pl.BlockSpec((1, 1, lc, NAT_D), lambda bi, h: (bi, layer, 0, h))
    return pl.pallas_call(
        functools.partial(_nat_body, rows=rows),
        grid=(b, NAT_H),
        in_specs=[colspec(OFF_NQ // 128), colspec(OFF_NK // 128), colspec(OFF_NV // 128), cspec, cspec,
                  pl.BlockSpec((1, 1) + bias.shape[2:], lambda bi, h: (layer, h, 0, 0, 0)),
                  colspec((OFF_GP + 2 * BR_W) // 128)],
        out_specs=pl.BlockSpec((1, s, NAT_D), lambda bi, h: (bi, 0, h)),
        out_shape=jax.ShapeDtypeStruct((b, s, NAT_H * NAT_D), BF16),
        scratch_shapes=[pltpu.VMEM((s, NAT_D), BF16), pltpu.VMEM((s, NAT_D), BF16),
                        pltpu.VMEM((s, NAT_D), F32)],
        compiler_params=_cp(("parallel", "parallel"), 48),
        name="nat_attn",
    )(z, z, z, cache_k, cache_v, bias, z)


def _swa_values_with_ones(v):
    lane = lax.broadcasted_iota(jnp.int32, v.shape, 1)
    halves = [jnp.where(lane < SWA_D, v if g == 0 else pltpu.roll(v, (SWA_KVH - g) * SWA_D, 1), 1.0)
              for g in range(SWA_KVH)]
    return jnp.concatenate(halves, axis=-1).astype(BF16)


def _swa_body(q_ref, k_ref, v_ref, kc_ref, vc_ref, sink_ref, cos_ref, sin_ref, gp_ref, o_ref,
              kr_scr, vl_scr, kc_scr, vc_scr, *, seq):
    scale = SWA_D ** -0.5
    group = SWA_H // SWA_KVH
    win = 3 * BLK
    i = pl.program_id(1)

    @pl.when(i == 0)
    def _():
        k = k_ref[0]
        kr_scr[...] = (k * cos_ref[...] + _swap_halves(k) * sin_ref[...]).astype(BF16)
        vl_scr[...] = _swa_values_with_ones(v_ref[0])
        kc_scr[...] = kc_ref[0, 0].astype(BF16)
        vc_scr[...] = _swa_values_with_ones(vc_ref[0, 0])

    start = pl.multiple_of(jnp.clip((i - 1) * BLK, 0, seq - win), BLK)
    qoff = pl.multiple_of(i * BLK, BLK)
    cq = cos_ref[pl.ds(qoff, BLK), :]
    sq = sin_ref[pl.ds(qoff, BLK), :]
    q = q_ref[0]
    q = (q * jnp.concatenate([cq] * (SWA_H // 2), axis=-1)
         + _swap_halves(q) * jnp.concatenate([sq] * (SWA_H // 2), axis=-1))
    q = q * scale
    kw = kr_scr[pl.ds(start, win), :]
    vw = vl_scr[pl.ds(start, win), :]

    qpos = qoff + lax.broadcasted_iota(jnp.int32, (BLK, win), 0)
    kpos = start + lax.broadcasted_iota(jnp.int32, (BLK, win), 1)
    valid = jnp.abs(qpos - kpos) <= SWA_WINDOW
    valid = jnp.concatenate([valid] * group, axis=0)

    outs = []
    for g in range(SWA_KVH):
        heads = range(g * group, (g + 1) * group)
        qg = jnp.concatenate([q[:, h * SWA_D:(h + 1) * SWA_D] for h in heads], axis=0).astype(BF16)
        sink = jnp.concatenate([jnp.broadcast_to(sink_ref[0:1, h:h + 1], (BLK, 1)) for h in heads], axis=0)
        ksl = slice(g * SWA_D, (g + 1) * SWA_D)
        vsl = slice(g * 2 * SWA_D, (g + 1) * 2 * SWA_D)
        s = jnp.concatenate([jnp.where(valid, _dot_nt(qg, kw[:, ksl]), NEG_INF),
                             _dot_nt(qg, kc_scr[:, ksl])], axis=-1)
        m = jnp.maximum(jnp.max(s, axis=-1, keepdims=True), sink)
        p = jnp.exp(s - m).astype(BF16)
        pv = _dot(p[:, :win], vw[:, vsl]) + _dot(p[:, win:], vc_scr[:, vsl])
        o = pv[:, :SWA_D] / (pv[:, SWA_D:SWA_D + 1] + jnp.exp(sink - m))
        outs += [o[t * BLK:(t + 1) * BLK] for t in range(group)]
    o = jnp.concatenate(outs, axis=-1)
    o_ref[0] = (o * _silu(gp_ref[0])).astype(BF16)


def _swa(z, cache_k, cache_v, layer, sink, cos2, sin2):
    b, s, _ = z.shape
    lc = cache_k.shape[2]
    hd = SWA_H * SWA_D
    kd = SWA_KVH * SWA_D
    cspec = pl.BlockSpec((1, 1, lc, kd), lambda bi, i: (bi, layer, 0, 0))
    tq = BLK
    return pl.pallas_call(
        functools.partial(_swa_body, seq=s),
        grid=(b, s // tq),
        in_specs=[pl.BlockSpec((1, tq, hd), lambda bi, i: (bi, i, OFF_SQ // hd)),
                  pl.BlockSpec((1, s, kd), lambda bi, i: (bi, 0, OFF_SK // kd)),
                  pl.BlockSpec((1, s, kd), lambda bi, i: (bi, 0, OFF_SV // kd)),
                  cspec, cspec,
                  pl.BlockSpec((1, SWA_H), lambda bi, i: (0, 0)),
                  pl.BlockSpec((s, 128), lambda bi, i: (0, 0)),
                  pl.BlockSpec((s, 128), lambda bi, i: (0, 0)),
                  pl.BlockSpec((1, tq, hd), lambda bi, i: (bi, i, (OFF_GP + 3 * BR_W) // hd))],
        out_specs=pl.BlockSpec((1, tq, hd), lambda bi, i: (bi, i, 0)),
        out_shape=jax.ShapeDtypeStruct((b, s, hd), BF16),
        scratch_shapes=[pltpu.VMEM((s, kd), BF16), pltpu.VMEM((s, 2 * kd), BF16),
                        pltpu.VMEM((lc, kd), BF16), pltpu.VMEM((lc, 2 * kd), BF16)],
        compiler_params=_cp(("parallel", "arbitrary"), 48),
        name="swa_attn",
    )(z, z, z, cache_k, cache_v, sink.reshape(1, SWA_H), cos2, sin2, z)


MERGE_SUB = 2


def _merge_body(h_ref, o0_ref, o1_ref, o2_ref, o3_ref, g0_ref, g1_ref, g2_ref, g3_ref, wb_ref, y_ref):
    h = h_ref[...]
    o_refs = (o0_ref, o1_ref, o2_ref, o3_ref)
    g_refs = (g0_ref, g1_ref, g2_ref, g3_ref)
    tsub = y_ref.shape[1] // MERGE_SUB
    for c in range(MERGE_SUB):
        cols = slice(c * tsub, (c + 1) * tsub)
        y = None
        for n in range(N_BRANCH):
            t = _sigmoid(_dot_nt(h, g_refs[n][cols, :])) * _dot(o_refs[n][...], wb_ref[n, :, cols])
            y = t if y is None else y + t
        y_ref[:, cols] = y.astype(BF16)


def _merge(h, ogs, w_gate_t, w_branch, layer, tm, tn):
    m, d = h.shape
    tm = min(tm, m)
    tn = min(tn, d)
    nj = d // tn
    gspec = lambda n: pl.BlockSpec((None, tn, d), lambda i, j: (layer, n * nj + j, 0))
    return pl.pallas_call(
        _merge_body,
        grid=(m // tm, nj),
        in_specs=[pl.BlockSpec((tm, d), lambda i, j: (i, 0))]
        + [pl.BlockSpec((tm, BR_W), lambda i, j: (i, 0))] * N_BRANCH
        + [gspec(n) for n in range(N_BRANCH)]
        + [pl.BlockSpec((None, N_BRANCH, BR_W, tn), lambda i, j: (layer, 0, 0, j))],
        out_specs=pl.BlockSpec((tm, tn), lambda i, j: (i, j)),
        out_shape=jax.ShapeDtypeStruct((m, d), BF16),
        compiler_params=_cp(("parallel", "arbitrary"), 56),
        name="merge",
    )(h, *ogs, w_gate_t, w_gate_t, w_gate_t, w_gate_t, w_branch)


OUT_SUB = 2


def _out_body(*refs, has_next):
    if has_next:
        y_ref, w_ref, x_ref, gate_ref, g_ref, gn_ref, sh_ref, sc_ref, o_ref, h_ref = refs
    else:
        y_ref, w_ref, x_ref, gate_ref, g_ref, o_ref = refs
    ts = y_ref.shape[1]
    tsub = ts // OUT_SUB
    for sub in range(OUT_SUB):
        rows = slice(sub * tsub, (sub + 1) * tsub)
        t = _dot(y_ref[0, rows, :], w_ref[...])
        t = t * lax.rsqrt(jnp.mean(t * t, axis=-1, keepdims=True) + EPS) * g_ref[...]
        x = x_ref[0, rows, :] + gate_ref[0] * t
        o_ref[0, rows, :] = x
        if has_next:
            y = x * lax.rsqrt(jnp.mean(x * x, axis=-1, keepdims=True) + EPS)
            y = y * gn_ref[...]
            h_ref[0, rows, :] = (y * (1.0 + sc_ref[0]) + sh_ref[0]).astype(BF16)


def _out_proj(y, w_out, layer, x, gate, g, nxt):
    b, s, d = x.shape
    ts = min(s, 512)
    per_b = gate.shape[0] > 1
    cidx = (lambda bi, i: (bi, 0, 0)) if per_b else (lambda bi, i: (0, 0, 0))
    tile = pl.BlockSpec((1, ts, d), lambda bi, i: (bi, i, 0))
    vec = pl.BlockSpec((1, d), lambda bi, i: (0, 0))
    in_specs = [tile, pl.BlockSpec((None, d, d), lambda bi, i: (layer, 0, 0), pipeline_mode=pl.Buffered(1)),
                tile, pl.BlockSpec((1, 1, d), cidx), vec]
    args = [y, w_out, x, gate, g.reshape(1, d)]
    out_specs = [tile]
    out_shape = [jax.ShapeDtypeStruct((b, s, d), F32)]
    if nxt is not None:
        gn, shift, scale = nxt
        in_specs += [vec, pl.BlockSpec((1, 1, d), cidx), pl.BlockSpec((1, 1, d), cidx)]
        args += [gn.reshape(1, d), shift, scale]
        out_specs.append(tile)
        out_shape.append(jax.ShapeDtypeStruct((b, s, d), BF16))
    res = pl.pallas_call(
        functools.partial(_out_body, has_next=nxt is not None),
        grid=(b, s // ts),
        in_specs=in_specs,
        out_specs=out_specs,
        out_shape=out_shape,
        compiler_params=_cp(("parallel", "parallel"), 56),
        name="out_proj",
    )(*args)
    return (res[0], res[1]) if nxt is not None else (res[0], None)


_MIX_SRC = ((0, 512), (832, 4416), (4672, 6720), (512, 768), (4416, 4672), (768, 832))
_MIX_COLS = 6720


RELAYOUT_ROWS = 256
RELAYOUT_ALIGN = 64


def _relayout_body(off_ref, valid_ref, w_ref, o_ref):
    j = pl.program_id(1)
    row = lax.broadcasted_iota(jnp.int32, w_ref.shape, 0)
    o_ref[...] = jnp.where(row < valid_ref[j], w_ref[...], 0.0).astype(BF16)


def _relayout_rows(w_t, offs, valid):
    depth, _, d = w_t.shape
    nblk = len(offs)
    grid_spec = pltpu.PrefetchScalarGridSpec(
        num_scalar_prefetch=2,
        grid=(depth, nblk),
        in_specs=[pl.BlockSpec((pl.Element(1), pl.Element(RELAYOUT_ROWS), pl.Element(d)),
                               lambda l, j, off, val: (l, off[j] * RELAYOUT_ALIGN, 0))],
        out_specs=pl.BlockSpec((1, RELAYOUT_ROWS, d), lambda l, j, off, val: (l, j, 0)),
    )
    return pl.pallas_call(
        _relayout_body,
        grid_spec=grid_spec,
        out_shape=jax.ShapeDtypeStruct((depth, nblk * RELAYOUT_ROWS, d), BF16),
        compiler_params=_cp(("parallel", "arbitrary"), 32),
        name="relayout_w_in",
    )(jnp.asarray([o // RELAYOUT_ALIGN for o in offs], jnp.int32), jnp.asarray(valid, jnp.int32), w_t)


def _relayout_w_in(w_in):
    n = w_in.shape[2]
    w_t = jnp.swapaxes(w_in, 1, 2)
    r = RELAYOUT_ROWS
    offs, valid = [], []
    for a, b in _MIX_SRC:
        assert len(offs) * r == sum(valid)
        for o in range(a, b, r):
            offs.append(o)
            valid.append(min(r, b - o))
    while len(offs) * r < N_MIX:
        offs.append(0)
        valid.append(0)
    assert len(offs) * r == N_MIX and sum(valid) == _MIX_COLS
    w_mix_t = _relayout_rows(w_t, offs, valid)
    ng = (n - _MIX_COLS) // r
    w_gate_t = _relayout_rows(w_t, [_MIX_COLS + r * j for j in range(ng)], [r] * ng)
    return w_mix_t, w_gate_t


def _mla_weights(w_q_up, w_kv_up):
    depth = w_q_up.shape[0]
    wq = w_q_up.reshape(depth, MLA_Q_LORA, MLA_H, MLA_NOPE + MLA_ROPE)
    wq = jnp.pad(wq, ((0, 0), (0, 0), (0, 0), (0, MLA_DQ - MLA_NOPE - MLA_ROPE)))
    wq = wq.reshape(depth, MLA_Q_LORA, MLA_H * MLA_DQ).astype(BF16)
    wkv = w_kv_up.reshape(depth, MLA_KV_LORA, MLA_H, MLA_NOPE + MLA_V)
    wk = jnp.pad(wkv[..., :MLA_NOPE], ((0, 0), (0, 0), (0, 0), (0, MLA_DQ - MLA_NOPE)))
    wk = wk.reshape(depth, MLA_KV_LORA, MLA_H * MLA_DQ).astype(BF16)
    wv = jnp.pad(wkv[..., MLA_NOPE:], ((0, 0), (0, 0), (0, 0), (0, MLA_DV - MLA_V)))
    wv = wv.reshape(depth, MLA_KV_LORA, MLA_H * MLA_DV).astype(BF16)
    j = jnp.arange(128)[:, None]
    cidx = jnp.arange(MLA_H * MLA_DQ)[None, :]
    p128 = ((cidx % MLA_DQ == MLA_NOPE + j) & (j < MLA_ROPE)).astype(BF16)
    vones = (jnp.arange(MLA_H * MLA_DV)[None, :] % MLA_DV == MLA_V).astype(F32)
    return wq, wk, wv, p128, vones


def _rope_tables(n_tok):
    pos = jnp.arange(n_tok)
    row = (pos // GRID_W).astype(F32)
    col = (pos % GRID_W).astype(F32)
    n_freq = MLA_ROPE // 4
    inv = ROPE_BASE ** (-jnp.arange(n_freq, dtype=F32) / n_freq)
    ang = jnp.concatenate([row[:, None] * inv[None], col[:, None] * inv[None]], axis=-1)
    cos, sin = jnp.cos(ang), jnp.sin(ang)
    c64 = jnp.concatenate([cos, cos], axis=-1)
    s64 = jnp.concatenate([-sin, sin], axis=-1)
    one, zero = jnp.ones_like(c64), jnp.zeros_like(c64)
    cq = jnp.concatenate([one, one, c64, one], axis=-1)
    sq = jnp.concatenate([zero, zero, s64, zero], axis=-1)
    ck = jnp.concatenate([c64, one], axis=-1)
    sk = jnp.concatenate([s64, zero], axis=-1)
    c2 = jnp.concatenate([c64, c64], axis=-1)
    s2 = jnp.concatenate([s64, s64], axis=-1)
    return (cq, sq, ck, sk), (c2, s2)


TM_IN = 2048
TN_IN = 768
TM_MERGE = 1024
TN_MERGE = 512


def kernel(x_prompt, x_sample, cache_mla_ckv, cache_mla_krope, state_ret, cache_nat_k, cache_nat_v,
           cache_swa_k, cache_swa_v, c, c_ctx, w_mod, b_mod, norm_pre, norm_post, w_in, mla_q_norm,
           mla_kv_norm, mla_w_q_up, mla_w_kv_up, ret_decay, ret_norm, nat_rpb, swa_sink, w_branch, w_out):
    depth = w_mod.shape[0]
    d = x_prompt.shape[-1]
    nb = x_sample.shape[0]
    s_lat = x_sample.shape[1]
    l_ctx = cache_mla_ckv.shape[2]

    w_mix, w_gate = _relayout_w_in(w_in)
    w_br = w_branch.astype(BF16)
    w_o = w_out.astype(BF16)
    wq, wk, wv, p128, vones = _mla_weights(mla_w_q_up, mla_w_kv_up)
    p64 = p128[:MLA_ROPE]
    mla_tabs, (cos2, sin2) = _rope_tables(s_lat)
    nat_bias = _nat_bias_tables(nat_rpb, s_lat // GRID_W)
    nat_k_c = cache_nat_k.reshape(nb, depth, l_ctx, NAT_H * NAT_D)
    nat_v_c = cache_nat_v.reshape(nb, depth, l_ctx, NAT_H * NAT_D)
    swa_k_c = cache_swa_k.reshape(nb, depth, l_ctx, SWA_KVH * SWA_D)
    swa_v_c = cache_swa_v.reshape(nb, depth, l_ctx, SWA_KVH * SWA_D)

    rows = 8
    conds = jnp.concatenate([c_ctx[None, :], c, jnp.zeros((rows - 1 - nb, d), F32)], axis=0)
    mod = _modulation(conds, w_mod, b_mod)

    def mod_parts(l, lo, hi):
        m = mod[l, lo:hi]
        return m[:, None, :d], m[:, None, d:2 * d], m[:, None, 2 * d:]

    def finish_layer(l, lo, hi, x, h, ogs):
        b, s, _ = x.shape
        gate = mod_parts(l, lo, hi)[2]
        y = _merge(h.reshape(b * s, d), [o.reshape(b * s, BR_W) for o in ogs], w_gate, w_br, l, TM_MERGE, TN_MERGE)
        nxt = None
        if l + 1 < depth:
            shift_n, scale_n, _ = mod_parts(l + 1, lo, hi)
            nxt = (norm_pre[l + 1], shift_n, scale_n)
        return _out_proj(y.reshape(b, s, d), w_o, l, x, gate, norm_post[l], nxt)

    def start_pass(x, lo, hi):
        shift, scale, _ = mod_parts(0, lo, hi)
        return _modnorm(x, norm_pre[0], shift, scale)

    def in_proj(x, h, l):
        b, s, _ = x.shape
        return _in_proj(h.reshape(b * s, d), w_mix, l, TM_IN, TN_IN).reshape(b, s, N_MIX)

    yp = x_prompt
    bc, sc, _ = yp.shape
    ckv_l, kr_l, st_l, nk_l, nv_l, sk_l, sv_l = [], [], [], [], [], [], []
    h = start_pass(yp, 0, 1)
    for l in range(depth):
        z = in_proj(yp, h, l)
        q, kf, vf, ckv = _mla_prep(z, mla_q_norm[l], mla_kv_norm[l], wq[l], wk[l], wv[l], p128, vones, None)
        og_mla, og_nat, og_swa = _ctx_attn(q, kf, vf, z, swa_sink[l])
        og_ret, st = _retention(z, ret_decay[l], ret_norm[l], None, 0, True)
        yp, h = finish_layer(l, 0, 1, yp, h, (og_mla, og_ret, og_nat, og_swa))
        ckv_l.append(ckv)
        kr_l.append(z[:, :, OFF_KR:OFF_KR + MLA_ROPE])
        st_l.append(st)
        nk_l.append(z[:, :, OFF_NK:OFF_NK + 512].reshape(bc, sc, NAT_H, NAT_D))
        nv_l.append(z[:, :, OFF_NV:OFF_NV + 512].reshape(bc, sc, NAT_H, NAT_D))
        sk_l.append(z[:, :, OFF_SK:OFF_SK + 128].reshape(bc, sc, SWA_KVH, SWA_D))
        sv_l.append(z[:, :, OFF_SV:OFF_SV + 128].reshape(bc, sc, SWA_KVH, SWA_D))

    ys = x_sample
    h = start_pass(ys, 1, 1 + nb)
    for l in range(depth):
        z = in_proj(ys, h, l)
        q, kf, vf, _ = _mla_prep(z, mla_q_norm[l], mla_kv_norm[l], wq[l], wk[l], wv[l], p128, vones, mla_tabs)
        kc, vc = _mla_expand(cache_mla_ckv, cache_mla_krope, l, wk[l], wv[l], p64, vones)
        og_mla = _mla_attn(q, kf, vf, z, OFF_GP // BR_W, ctx=(kc, vc))
        og_ret, _ = _retention(z, ret_decay[l], ret_norm[l], state_ret, l, False)
        og_nat = _nat(z, nat_k_c, nat_v_c, l, nat_bias)
        og_swa = _swa(z, swa_k_c, swa_v_c, l, swa_sink[l], cos2, sin2)
        ys, h = finish_layer(l, 1, 1 + nb, ys, h, (og_mla, og_ret, og_nat, og_swa))

    return (yp, ys, jnp.stack(ckv_l, axis=1), jnp.stack(kr_l, axis=1), jnp.stack(st_l, axis=1),
            jnp.stack(nk_l, axis=1), jnp.stack(nv_l, axis=1), jnp.stack(sk_l, axis=1), jnp.stack(sv_l, axis=1))
```

```python
import functools

import jax
import jax.numpy as jnp
from jax import lax
from jax.experimental import pallas as pl
from jax.experimental.pallas import tpu as pltpu

F32 = jnp.float32
BF16 = jnp.bfloat16

GRID_W = 64
BLK = 128
ROPE_BASE = 10000.0
EPS = 1e-6
NEG_INF = -1e30
N_BRANCH = 4
BR_W = 512

MLA_H = 4
MLA_Q_LORA = 512
MLA_KV_LORA = 256
MLA_NOPE = 128
MLA_ROPE = 64
MLA_V = 128
MLA_DQ = 256
MLA_DV = 256
MLA_QSCALE = (MLA_NOPE + MLA_ROPE) ** -0.5 * 1.4426950408889634

RET_H = 4
RET_DK = 128
RET_DV = 128
RET_HPS = 2

NAT_H = 4
NAT_D = 128
NAT_ROWS = 8
NAT_COLS = 16
NAT_QR = 4
NAT_WR = NAT_ROWS + NAT_QR

SWA_H = 8
SWA_KVH = 2
SWA_D = 64
SWA_WINDOW = 128

OFF_QA = 0
OFF_RQ = 512
OFF_RK = 1024
OFF_RV = 1536
OFF_NQ = 2048
OFF_NK = 2560
OFF_NV = 3072
OFF_SQ = 3584
OFF_GP = 4096
OFF_KVA = 6144
OFF_SK = 6400
OFF_SV = 6528
OFF_KR = 6656
N_MIX_USED = 6784
N_MIX = 6912

V7X_VMEM_BYTES = 64 * 1024 * 1024


def _cp(sem, vmem_mb):
    assert vmem_mb * 1024 * 1024 < V7X_VMEM_BYTES
    return pltpu.CompilerParams(dimension_semantics=sem, vmem_limit_bytes=vmem_mb * 1024 * 1024)


def _sigmoid(x):
    return 1.0 / (1.0 + jnp.exp(-x))


def _silu(x):
    return x * _sigmoid(x)


def _swap_halves(x):
    n = x.shape[-1]
    ax = x.ndim - 1
    lane = lax.broadcasted_iota(jnp.int32, x.shape, ax)
    return jnp.where((lane & 32) == 0, pltpu.roll(x, n - 32, ax), pltpu.roll(x, 32, ax))


def _dot_nt(a, b):
    return lax.dot_general(a, b, (((1,), (1,)), ((), ())), preferred_element_type=F32)


def _dot(a, b):
    return jnp.dot(a, b, preferred_element_type=F32)


def _mod_body(c_ref, w_ref, b_ref, o_ref):
    a = _silu(c_ref[...]).astype(BF16)
    o_ref[0] = _dot(a, w_ref[0].astype(BF16)) + b_ref[0]


def _modulation(conds, w_mod, b_mod):
    depth, d, n = w_mod.shape
    rows = conds.shape[0]
    tn = 1024 if n % 1024 == 0 else n
    return pl.pallas_call(
        _mod_body,
        grid=(depth, n // tn),
        in_specs=[pl.BlockSpec((rows, d), lambda l, j: (0, 0)),
                  pl.BlockSpec((1, d, tn), lambda l, j: (l, 0, j)),
                  pl.BlockSpec((1, 1, tn), lambda l, j: (l, 0, j))],
        out_specs=pl.BlockSpec((1, rows, tn), lambda l, j: (l, 0, j)),
        out_shape=jax.ShapeDtypeStruct((depth, rows, n), F32),
        compiler_params=_cp(("parallel", "parallel"), 40),
        name="modulation",
    )(conds, w_mod, b_mod.reshape(depth, 1, n))


def _modnorm_body(x_ref, g_ref, sh_ref, sc_ref, h_ref):
    x = x_ref[0]
    y = x * lax.rsqrt(jnp.mean(x * x, axis=-1, keepdims=True) + EPS)
    y = y * g_ref[...]
    h_ref[0] = (y * (1.0 + sc_ref[0]) + sh_ref[0]).astype(BF16)


def _modnorm(x, g, shift, scale):
    b, s, d = x.shape
    ts = min(s, 512)
    per_b = shift.shape[0] > 1
    cidx = (lambda bi, i: (bi, 0, 0)) if per_b else (lambda bi, i: (0, 0, 0))
    return pl.pallas_call(
        _modnorm_body,
        grid=(b, s // ts),
        in_specs=[pl.BlockSpec((1, ts, d), lambda bi, i: (bi, i, 0)),
                  pl.BlockSpec((1, d), lambda bi, i: (0, 0)),
                  pl.BlockSpec((1, 1, d), cidx),
                  pl.BlockSpec((1, 1, d), cidx)],
        out_specs=pl.BlockSpec((1, ts, d), lambda bi, i: (bi, i, 0)),
        out_shape=jax.ShapeDtypeStruct((b, s, d), BF16),
        compiler_params=_cp(("parallel", "parallel"), 32),
        name="modnorm",
    )(x, g.reshape(1, d), shift, scale)


def _mm_body(a_ref, b_ref, o_ref):
    o_ref[...] = _dot_nt(a_ref[...], b_ref[...])


def _in_proj(a, w_mix_t, layer, tm, tn):
    m, k = a.shape
    tm = min(tm, m)
    return pl.pallas_call(
        _mm_body,
        grid=(m // tm, N_MIX // tn),
        in_specs=[pl.BlockSpec((tm, k), lambda i, j: (i, 0)),
                  pl.BlockSpec((None, tn, k), lambda i, j: (layer, j, 0))],
        out_specs=pl.BlockSpec((tm, tn), lambda i, j: (i, j)),
        out_shape=jax.ShapeDtypeStruct((m, N_MIX), F32),
        compiler_params=_cp(("parallel", "arbitrary"), 48),
        name="in_proj",
    )(a, w_mix_t)


def _mla_prep_body(*refs, rope):
    if rope:
        (qa_ref, kva_ref, kr_ref, gq_ref, gkv_ref, wq_ref, wk_ref, wv_ref, p_ref, one_ref,
         cq_ref, sq_ref, ck_ref, sk_ref, q_out, k_out, v_out, ckv_out) = refs
    else:
        (qa_ref, kva_ref, kr_ref, gq_ref, gkv_ref, wq_ref, wk_ref, wv_ref, p_ref, one_ref,
         q_out, k_out, v_out, ckv_out) = refs
    qa = qa_ref[0]
    cq = qa * lax.rsqrt(jnp.mean(qa * qa, axis=-1, keepdims=True) + EPS) * gq_ref[...]
    q = _dot(cq.astype(BF16), wq_ref[...])
    kva = kva_ref[0]
    ckv = kva * lax.rsqrt(jnp.mean(kva * kva, axis=-1, keepdims=True) + EPS) * gkv_ref[...]
    ckv_out[0] = ckv
    kr = kr_ref[0]
    if rope:
        cq_t = jnp.concatenate([cq_ref[...]] * MLA_H, axis=-1)
        sq_t = jnp.concatenate([sq_ref[...]] * MLA_H, axis=-1)
        q = q * cq_t + _swap_halves(q) * sq_t
        kr = kr * ck_ref[...] + _swap_halves(kr) * sk_ref[...]
    q_out[0] = (q * MLA_QSCALE).astype(BF16)
    cb = ckv.astype(BF16)
    k_out[0] = (_dot(cb, wk_ref[...]) + _dot(kr.astype(BF16), p_ref[...])).astype(BF16)
    v_out[0] = (_dot(cb, wv_ref[...]) + one_ref[...]).astype(BF16)


def _mla_prep(z, gq, gkv, wq, wk, wv, p128, vones, tabs):
    b, s, _ = z.shape
    tm = min(s, 512)
    rope = tabs is not None
    full = lambda shape: pl.BlockSpec(shape, lambda bi, i: (0,) * len(shape))
    in_specs = [pl.BlockSpec((1, tm, MLA_Q_LORA), lambda bi, i: (bi, i, OFF_QA // MLA_Q_LORA)),
                pl.BlockSpec((1, tm, MLA_KV_LORA), lambda bi, i: (bi, i, OFF_KVA // MLA_KV_LORA)),
                pl.BlockSpec((1, tm, 128), lambda bi, i: (bi, i, OFF_KR // 128)),
                full((1, MLA_Q_LORA)), full((1, MLA_KV_LORA)),
                full(wq.shape), full(wk.shape), full(wv.shape), full(p128.shape), full(vones.shape)]
    args = [z, z, z, gq.reshape(1, -1), gkv.reshape(1, -1), wq, wk, wv, p128, vones]
    if rope:
        cq, sq, ck, sk = tabs
        in_specs += [pl.BlockSpec((tm, MLA_DQ), lambda bi, i: (i, 0))] * 2
        in_specs += [pl.BlockSpec((tm, 128), lambda bi, i: (i, 0))] * 2
        args += [cq, sq, ck, sk]
    hq = MLA_H * MLA_DQ
    hv = MLA_H * MLA_DV
    return pl.pallas_call(
        functools.partial(_mla_prep_body, rope=rope),
        grid=(b, s // tm),
        in_specs=in_specs,
        out_specs=[pl.BlockSpec((1, tm, hq), lambda bi, i: (bi, i, 0)),
                   pl.BlockSpec((1, tm, hq), lambda bi, i: (bi, i, 0)),
                   pl.BlockSpec((1, tm, hv), lambda bi, i: (bi, i, 0)),
                   pl.BlockSpec((1, tm, MLA_KV_LORA), lambda bi, i: (bi, i, 0))],
        out_shape=[jax.ShapeDtypeStruct((b, s, hq), BF16),
                   jax.ShapeDtypeStruct((b, s, hq), BF16),
                   jax.ShapeDtypeStruct((b, s, hv), BF16),
                   jax.ShapeDtypeStruct((b, s, MLA_KV_LORA), F32)],
        compiler_params=_cp(("parallel", "parallel"), 40),
        name="mla_prep",
    )(*args)


def _mla_expand_body(ckv_ref, kr_ref, wk_ref, wv_ref, p_ref, one_ref, k_out, v_out):
    cb = ckv_ref[0, 0].astype(BF16)
    k_out[0] = (_dot(cb, wk_ref[...]) + _dot(kr_ref[0, 0].astype(BF16), p_ref[...])).astype(BF16)
    v_out[0] = (_dot(cb, wv_ref[...]) + one_ref[...]).astype(BF16)


def _mla_expand(cache_ckv, cache_kr, layer, wk, wv, p64, vones):
    b, _, s, _ = cache_ckv.shape
    full = lambda shape: pl.BlockSpec(shape, lambda bi: (0,) * len(shape))
    hq = MLA_H * MLA_DQ
    hv = MLA_H * MLA_DV
    return pl.pallas_call(
        _mla_expand_body,
        grid=(b,),
        in_specs=[pl.BlockSpec((1, 1, s, MLA_KV_LORA), lambda bi: (bi, layer, 0, 0)),
                  pl.BlockSpec((1, 1, s, MLA_ROPE), lambda bi: (bi, layer, 0, 0)),
                  full(wk.shape), full(wv.shape), full(p64.shape), full(vones.shape)],
        out_specs=[pl.BlockSpec((1, s, hq), lambda bi: (bi, 0, 0)),
                   pl.BlockSpec((1, s, hv), lambda bi: (bi, 0, 0))],
        out_shape=[jax.ShapeDtypeStruct((b, s, hq), BF16),
                   jax.ShapeDtypeStruct((b, s, hv), BF16)],
        compiler_params=_cp(("parallel",), 32),
        name="mla_expand",
    )(cache_ckv, cache_kr, wk, wv, p64, vones)


def _mla_attn_body(*refs, has_ctx, nkv):
    it = iter(refs)
    q_ref, k_ref, v_ref = next(it), next(it), next(it)
    kc_ref = vc_ref = None
    if has_ctx:
        kc_ref, vc_ref = next(it), next(it)
    gp_ref, o_ref, m_scr, acc_scr = next(it), next(it), next(it), next(it)
    j = pl.program_id(2)

    def process(kr, vr):
        tk = kr.shape[1]
        for h in range(MLA_H):
            qk = slice(h * MLA_DQ, (h + 1) * MLA_DQ)
            s = _dot_nt(q_ref[0, :, qk], kr[0, :, qk])
            m_prev = m_scr[h]
            m_new = jnp.maximum(m_prev, jnp.max(s, axis=-1, keepdims=True))
            alpha = jnp.exp2(m_prev - m_new)
            p = jnp.exp2(s - jnp.concatenate([m_new] * (tk // 128), axis=-1))
            pv = _dot(p.astype(BF16), vr[0, :, h * MLA_DV:h * MLA_DV + MLA_V])
            acc_scr[h, :, :MLA_V] = alpha * acc_scr[h, :, :MLA_V] + pv
            acc_scr[h, :, MLA_V:] = alpha * acc_scr[h, :, MLA_V:] + jnp.sum(p, axis=-1, keepdims=True)
            m_scr[h] = m_new

    @pl.when(j == 0)
    def _():
        m_scr[...] = jnp.full(m_scr.shape, NEG_INF, F32)
        acc_scr[...] = jnp.zeros(acc_scr.shape, F32)
        if has_ctx:
            process(kc_ref, vc_ref)

    process(k_ref, v_ref)

    @pl.when(j == nkv - 1)
    def _():
        outs = []
        for h in range(MLA_H):
            a = acc_scr[h]
            outs.append(a[:, :MLA_V] / a[:, MLA_V:MLA_V + 1])
        o_ref[0] = (jnp.concatenate(outs, axis=-1) * _silu(gp_ref[0])).astype(BF16)


def _mla_attn(q, k, v, gp, gpcol, ctx=None, tq=1024, tk=1024):
    b, sq, _ = q.shape
    sk = k.shape[1]
    tq = min(tq, sq)
    tk = min(tk, sk)
    nkv = sk // tk
    hq = MLA_H * MLA_DQ
    hv = MLA_H * MLA_DV
    in_specs = [pl.BlockSpec((1, tq, hq), lambda bi, i, j: (bi, i, 0)),
                pl.BlockSpec((1, tk, hq), lambda bi, i, j: (bi, j, 0)),
                pl.BlockSpec((1, tk, hv), lambda bi, i, j: (bi, j, 0))]
    args = [q, k, v]
    if ctx is not None:
        kc, vc = ctx
        in_specs += [pl.BlockSpec((1,) + kc.shape[1:], lambda bi, i, j: (bi, 0, 0)),
                     pl.BlockSpec((1,) + vc.shape[1:], lambda bi, i, j: (bi, 0, 0))]
        args += [kc, vc]
    in_specs += [pl.BlockSpec((1, tq, MLA_H * MLA_V), lambda bi, i, j: (bi, i, gpcol))]
    args += [gp]
    return pl.pallas_call(
        functools.partial(_mla_attn_body, has_ctx=ctx is not None, nkv=nkv),
        grid=(b, sq // tq, nkv),
        in_specs=in_specs,
        out_specs=pl.BlockSpec((1, tq, MLA_H * MLA_V), lambda bi, i, j: (bi, i, 0)),
        out_shape=jax.ShapeDtypeStruct((b, sq, MLA_H * MLA_V), BF16),
        scratch_shapes=[pltpu.VMEM((MLA_H, tq, 128), F32), pltpu.VMEM((MLA_H, tq, MLA_DV), F32)],
        compiler_params=_cp(("parallel", "parallel", "arbitrary"), 48),
        name="mla_attn",
    )(*args)


CTX_SPS = 2


def _ctx_attn_body(qm_ref, km_ref, vm_ref, nq_ref, nk_ref, nv_ref, sq_ref, sk_ref, sv_ref, sink_ref,
                   gpm_ref, gpn_ref, gps_ref, om_ref, on_ref, os_ref):
    for bb in range(qm_ref.shape[0]):
        view = lambda r: r.at[bb:bb + 1]
        _ctx_attn_one(view(qm_ref), view(km_ref), view(vm_ref), view(nq_ref), view(nk_ref), view(nv_ref),
                      view(sq_ref), view(sk_ref), view(sv_ref), sink_ref,
                      view(gpm_ref), view(gpn_ref), view(gps_ref), view(om_ref), view(on_ref), view(os_ref))


def _ctx_attn_one(qm_ref, km_ref, vm_ref, nq_ref, nk_ref, nv_ref, sq_ref, sk_ref, sv_ref, sink_ref,
                  gpm_ref, gpn_ref, gps_ref, om_ref, on_ref, os_ref):
    outs = []
    for h in range(MLA_H):
        qk = slice(h * MLA_DQ, (h + 1) * MLA_DQ)
        s = _dot_nt(qm_ref[0, :, qk], km_ref[0, :, qk])
        p = jnp.exp2(s - jnp.max(s, axis=-1, keepdims=True))
        den = jnp.sum(p, axis=-1, keepdims=True)
        outs.append(_dot(p.astype(BF16), vm_ref[0, :, h * MLA_DV:h * MLA_DV + MLA_V]) / den)
    om_ref[0] = (jnp.concatenate(outs, axis=-1) * _silu(gpm_ref[0])).astype(BF16)

    outs = []
    for h in range(NAT_H):
        sl = slice(h * NAT_D, (h + 1) * NAT_D)
        s = _dot_nt(nq_ref[0, :, sl].astype(BF16), nk_ref[0, :, sl].astype(BF16)) * NAT_D ** -0.5
        p = jnp.exp(s - jnp.max(s, axis=-1, keepdims=True))
        den = jnp.sum(p, axis=-1, keepdims=True)
        outs.append(_dot(p.astype(BF16), nv_ref[0, :, sl].astype(BF16)) / den)
    on_ref[0] = (jnp.concatenate(outs, axis=-1) * _silu(gpn_ref[0])).astype(BF16)

    n_tok = sq_ref.shape[1]
    group = SWA_H // SWA_KVH
    q = sq_ref[0] * SWA_D ** -0.5
    outs = []
    for g in range(SWA_KVH):
        heads = range(g * group, (g + 1) * group)
        qg = jnp.concatenate([q[:, h * SWA_D:(h + 1) * SWA_D] for h in heads], axis=0).astype(BF16)
        sink = jnp.concatenate([jnp.broadcast_to(sink_ref[0:1, h:h + 1], (n_tok, 1)) for h in heads], axis=0)
        ksl = slice(g * SWA_D, (g + 1) * SWA_D)
        s = _dot_nt(qg, sk_ref[0, :, ksl].astype(BF16))
        m = jnp.maximum(jnp.max(s, axis=-1, keepdims=True), sink)
        p = jnp.exp(s - m)
        den = jnp.sum(p, axis=-1, keepdims=True) + jnp.exp(sink - m)
        o = _dot(p.astype(BF16), sv_ref[0, :, ksl].astype(BF16)) / den
        outs += [o[t * n_tok:(t + 1) * n_tok] for t in range(group)]
    os_ref[0] = (jnp.concatenate(outs, axis=-1) * _silu(gps_ref[0])).astype(BF16)


def _ctx_attn(q, k, v, z, sink):
    b, n_tok, _ = z.shape
    sps = CTX_SPS if b % CTX_SPS == 0 else 1
    blk = lambda w, off: pl.BlockSpec((sps, n_tok, w), lambda bi: (bi, 0, off // w))
    kd = SWA_KVH * SWA_D
    out = jax.ShapeDtypeStruct((b, n_tok, BR_W), BF16)
    return pl.pallas_call(
        _ctx_attn_body,
        grid=(b // sps,),
        in_specs=[blk(MLA_H * MLA_DQ, 0), blk(MLA_H * MLA_DQ, 0), blk(MLA_H * MLA_DV, 0),
                  blk(BR_W, OFF_NQ), blk(BR_W, OFF_NK), blk(BR_W, OFF_NV),
                  blk(BR_W, OFF_SQ), blk(kd, OFF_SK), blk(kd, OFF_SV),
                  pl.BlockSpec((1, SWA_H), lambda bi: (0, 0)),
                  blk(BR_W, OFF_GP), blk(BR_W, OFF_GP + 2 * BR_W), blk(BR_W, OFF_GP + 3 * BR_W)],
        out_specs=[blk(BR_W, 0)] * 3,
        out_shape=[out, out, out],
        compiler_params=_cp(("parallel",), 48),
        name="ctx_attn",
    )(q, k, v, z, z, z, z, z, z, sink.reshape(1, SWA_H), z, z, z)


def _ret_body(*refs, nc, has_s0, has_st):
    it = iter(refs)
    dec_ref, norm_ref, q_ref, k_ref, v_ref = next(it), next(it), next(it), next(it), next(it)
    s0_ref = next(it) if has_s0 else None
    gp_ref, o_ref = next(it), next(it)
    st_ref = next(it) if has_st else None
    o_scr, kv_scr = next(it), next(it)
    hgrp = pl.program_id(1)

    d = dec_ref[0]
    lg_all = jnp.minimum(d, 0.0) - jnp.log1p(jnp.exp(-jnp.abs(d)))
    hcol = lax.broadcasted_iota(jnp.int32, d.shape, 1)
    row = lax.broadcasted_iota(jnp.int32, (BLK, BLK), 0).astype(F32)
    col = lax.broadcasted_iota(jnp.int32, (BLK, BLK), 1).astype(F32)
    a = lax.broadcasted_iota(jnp.int32, (BLK, 1), 0).astype(F32)
    kscale = RET_DK ** -0.5

    heads = []
    for hh in range(RET_HPS):
        lg = jnp.sum(jnp.where(hcol == hgrp * RET_HPS + hh, lg_all, 0.0), axis=1, keepdims=True)
        lg_f, lg_b = lg[0:1, :], lg[1:2, :]
        dm = (jnp.where(row - col >= 0, jnp.exp(lg_f * jnp.maximum(row - col, 0.0)), 0.0)
              + jnp.where(col - row > 0, jnp.exp(lg_b * jnp.maximum(col - row, 0.0)), 0.0))
        fwd = (jnp.exp(lg_f * (a + 1.0)), jnp.exp(lg_f * (BLK - 1.0 - a)), jnp.exp(lg_f * BLK))
        bwd = (jnp.exp(lg_b * (BLK - a)), jnp.exp(lg_b * a), jnp.exp(lg_b * BLK))
        heads.append((hh, slice(hh * RET_DK, (hh + 1) * RET_DK), dm, fwd, bwd))

    def intra(c, _):
        off = pl.multiple_of(c * BLK, BLK)
        for hh, sl, dm, fwd, bwd in heads:
            q = q_ref[0, pl.ds(off, BLK), sl]
            k = k_ref[0, pl.ds(off, BLK), sl] * kscale
            v = v_ref[0, pl.ds(off, BLK), sl].astype(BF16)
            inner = _dot_nt(q.astype(BF16), k.astype(BF16)) * dm
            lhs = jnp.concatenate([inner.astype(BF16), jnp.transpose(k * fwd[1]).astype(BF16),
                                   jnp.transpose(k * bwd[1]).astype(BF16)], axis=0)
            r = _dot(lhs, v)
            o_scr[pl.ds(off, BLK), sl] = r[:BLK]
            kv_scr[hh, 0, c] = r[BLK:2 * BLK]
            kv_scr[hh, 1, c] = r[2 * BLK:]
        return 0

    lax.fori_loop(0, nc, intra, 0, unroll=4 if nc % 4 == 0 else 2)

    def scan(j, carry):
        cf = j
        cb = nc - 1 - j
        off_f = pl.multiple_of(cf * BLK, BLK)
        off_b = pl.multiple_of(cb * BLK, BLK)
        new = []
        for (hh, sl, _, fwd, bwd), (sf, sb) in zip(heads, carry):
            qf = (q_ref[0, pl.ds(off_f, BLK), sl] * fwd[0]).astype(BF16)
            o_scr[pl.ds(off_f, BLK), sl] += _dot(qf, sf.astype(BF16))
            sf = sf * fwd[2] + kv_scr[hh, 0, cf]
            qb = (q_ref[0, pl.ds(off_b, BLK), sl] * bwd[0]).astype(BF16)
            o_scr[pl.ds(off_b, BLK), sl] += _dot(qb, sb.astype(BF16))
            sb = sb * bwd[2] + kv_scr[hh, 1, cb]
            new.append((sf, sb))
        return tuple(new)

    if has_s0:
        init = tuple((s0_ref[0, 0, 0, hh], s0_ref[0, 0, 1, hh]) for hh in range(RET_HPS))
    else:
        zero = jnp.zeros((RET_DK, RET_DV), F32)
        init = tuple((zero, zero) for _ in range(RET_HPS))
    fin = lax.fori_loop(0, nc, scan, init, unroll=4 if nc % 4 == 0 else 2)
    if has_st:
        for hh in range(RET_HPS):
            st_ref[0, 0, hh] = fin[hh][0]
            st_ref[0, 1, hh] = fin[hh][1]
    for _, sl, _, _, _ in heads:
        o = o_scr[:, sl]
        o = o * lax.rsqrt(jnp.mean(o * o, axis=-1, keepdims=True) + EPS) * norm_ref[:, sl]
        o_ref[0, :, sl] = (o * _silu(gp_ref[0, :, sl])).astype(BF16)


def _retention(z, decay, norm, s0, layer, want_state):
    b, s, _ = z.shape
    nc = s // BLK
    w = RET_HPS * RET_DK
    colspec = lambda off: pl.BlockSpec((1, s, w), lambda bi, g: (bi, 0, off // w + g))
    in_specs = [pl.BlockSpec((1, 2, RET_H), lambda bi, g: (0, 0, 0)),
                pl.BlockSpec((1, w), lambda bi, g: (0, g)),
                colspec(OFF_RQ), colspec(OFF_RK), colspec(OFF_RV)]
    args = [decay.reshape(1, 2, RET_H), norm.reshape(1, -1), z, z, z]
    if s0 is not None:
        in_specs.append(pl.BlockSpec((1, 1, 2, RET_HPS, RET_DK, RET_DV), lambda bi, g: (bi, layer, 0, g, 0, 0)))
        args.append(s0)
    in_specs.append(colspec(OFF_GP + BR_W))
    args.append(z)
    out_specs = [pl.BlockSpec((1, s, w), lambda bi, g: (bi, 0, g))]
    out_shape = [jax.ShapeDtypeStruct((b, s, RET_H * RET_DV), BF16)]
    if want_state:
        out_specs.append(pl.BlockSpec((1, 2, RET_HPS, RET_DK, RET_DV), lambda bi, g: (bi, 0, g, 0, 0)))
        out_shape.append(jax.ShapeDtypeStruct((b, 2, RET_H, RET_DK, RET_DV), F32))
    res = pl.pallas_call(
        functools.partial(_ret_body, nc=nc, has_s0=s0 is not None, has_st=want_state),
        grid=(b, RET_H // RET_HPS),
        in_specs=in_specs,
        out_specs=out_specs,
        out_shape=out_shape,
        scratch_shapes=[pltpu.VMEM((s, w), F32), pltpu.VMEM((RET_HPS, 2, nc, RET_DK, RET_DV), F32)],
        compiler_params=_cp(("parallel", "parallel"), 56),
        name="retention",
    )(*args)
    return (res[0], res[1]) if want_state else (res[0], None)


def _nat_body(q_ref, k_ref, v_ref, kc_ref, vc_ref, bias_ref, gp_ref, o_ref, kb_scr, vb_scr, o_scr, *, rows):
    scale = NAT_D ** -0.5
    kb_scr[...] = k_ref[0].astype(BF16)
    vb_scr[...] = v_ref[0].astype(BF16)
    kc = kc_ref[0, 0].astype(BF16)
    vc = vc_ref[0, 0].astype(BF16)
    nq = NAT_QR * GRID_W
    win = NAT_WR * GRID_W
    nblk = rows // NAT_QR

    def step(t, _):
        r0 = t * NAT_QR
        ws = jnp.clip(r0 - NAT_ROWS // 2, 0, rows - NAT_WR)
        pat = jnp.where(t == 0, 0, jnp.where(t == nblk - 1, 2, 1))
        qoff = pl.multiple_of(r0 * GRID_W, nq)
        koff = pl.multiple_of(ws * GRID_W, GRID_W)
        q = q_ref[0, pl.ds(qoff, nq), :].astype(BF16)
        kw = kb_scr[pl.ds(koff, win), :]
        vw = vb_scr[pl.ds(koff, win), :]
        s_loc = _dot_nt(q, kw) * scale + bias_ref[0, 0, pat]
        s_ctx = _dot_nt(q, kc) * scale
        m = jnp.maximum(jnp.max(s_loc, axis=-1, keepdims=True), jnp.max(s_ctx, axis=-1, keepdims=True))
        p_loc = jnp.exp(s_loc - m)
        p_ctx = jnp.exp(s_ctx - m)
        den = jnp.sum(p_loc, axis=-1, keepdims=True) + jnp.sum(p_ctx, axis=-1, keepdims=True)
        o = _dot(p_loc.astype(BF16), vw) + _dot(p_ctx.astype(BF16), vc)
        o_scr[pl.ds(qoff, nq), :] = o / den
        return 0

    lax.fori_loop(0, nblk, step, 0, unroll=8 if nblk % 8 == 0 else 4)
    o_ref[0] = (o_scr[...] * _silu(gp_ref[0])).astype(BF16)


def _nat_bias_tables(rpb, rows):
    depth, n_h = rpb.shape[:2]
    c = jnp.arange(GRID_W)[:, None]
    kc = jnp.arange(GRID_W)[None, :]
    cs = jnp.clip(c - NAT_COLS // 2, 0, GRID_W - NAT_COLS)
    valid = (kc >= cs) & (kc < cs + NAT_COLS)
    onehot = ((kc - c + (NAT_COLS - 1))[:, :, None] == jnp.arange(2 * NAT_COLS - 1)[None, None, :]).astype(F32)
    toep = jnp.einsum("dhrj,ckj->dhrck", rpb.astype(F32), onehot, precision=lax.Precision.HIGHEST)
    toep = jnp.where(valid, toep, NEG_INF)
    neg = jnp.full((depth, n_h, GRID_W, GRID_W), NEG_INF, F32)
    pats = []
    for r0, ws in ((0, 0), (NAT_QR, 0), (rows - NAT_QR, rows - NAT_WR)):
        qrows = []
        for qi in range(NAT_QR):
            r = r0 + qi
            rs = min(max(r - NAT_ROWS // 2, 0), rows - NAT_ROWS)
            blocks = []
            for wm in range(NAT_WR):
                kr = ws + wm
                blocks.append(toep[:, :, kr - r + NAT_ROWS - 1] if rs <= kr < rs + NAT_ROWS else neg)
            qrows.append(jnp.concatenate(blocks, axis=-1))
        pats.append(jnp.concatenate(qrows, axis=-2))
    return jnp.stack(pats, axis=2)


def _nat(z, cache_k, cache_v, layer, bias):
    b, s, _ = z.shape
    rows = s // GRID_W
    assert rows % NAT_QR == 0 and rows >= 2 * NAT_WR - NAT_ROWS
    lc = cache_k.shape[2]
    colspec = lambda blk: pl.BlockSpec((1, s, NAT_D), lambda bi, h: (bi, 0, blk + h))
    cspec = pl.BlockSpec((1, 1, lc, NAT_D), lambda bi, h: (bi, layer, 0, h))
    return pl.pallas_call(
        functools.partial(_nat_body, rows=rows),
        grid=(b, NAT_H),
        in_specs=[colspec(OFF_NQ // 128), colspec(OFF_NK // 128), colspec(OFF_NV // 128), cspec, cspec,
                  pl.BlockSpec((1, 1) + bias.shape[2:], lambda bi, h: (layer, h, 0, 0, 0)),
                  colspec((OFF_GP + 2 * BR_W) // 128)],
        out_specs=pl.BlockSpec((1, s, NAT_D), lambda bi, h: (bi, 0, h)),
        out_shape=jax.ShapeDtypeStruct((b, s, NAT_H * NAT_D), BF16),
        scratch_shapes=[pltpu.VMEM((s, NAT_D), BF16), pltpu.VMEM((s, NAT_D), BF16),
                        pltpu.VMEM((s, NAT_D), F32)],
        compiler_params=_cp(("parallel", "parallel"), 48),
        name="nat_attn",
    )(z, z, z, cache_k, cache_v, bias, z)


def _swa_values_with_ones(v):
    lane = lax.broadcasted_iota(jnp.int32, v.shape, 1)
    halves = [jnp.where(lane < SWA_D, v if g == 0 else pltpu.roll(v, (SWA_KVH - g) * SWA_D, 1), 1.0)
              for g in range(SWA_KVH)]
    return jnp.concatenate(halves, axis=-1).astype(BF16)


def _swa_body(q_ref, k_ref, v_ref, kc_ref, vc_ref, sink_ref, cos_ref, sin_ref, gp_ref, o_ref,
              kr_scr, vl_scr, kc_scr, vc_scr, *, seq):
    scale = SWA_D ** -0.5
    group = SWA_H // SWA_KVH
    win = 3 * BLK
    i = pl.program_id(1)

    @pl.when(i == 0)
    def _():
        k = k_ref[0]
        kr_scr[...] = (k * cos_ref[...] + _swap_halves(k) * sin_ref[...]).astype(BF16)
        vl_scr[...] = _swa_values_with_ones(v_ref[0])
        kc_scr[...] = kc_ref[0, 0].astype(BF16)
        vc_scr[...] = _swa_values_with_ones(vc_ref[0, 0])

    start = pl.multiple_of(jnp.clip((i - 1) * BLK, 0, seq - win), BLK)
    qoff = pl.multiple_of(i * BLK, BLK)
    cq = cos_ref[pl.ds(qoff, BLK), :]
    sq = sin_ref[pl.ds(qoff, BLK), :]
    q = q_ref[0]
    q = (q * jnp.concatenate([cq] * (SWA_H // 2), axis=-1)
         + _swap_halves(q) * jnp.concatenate([sq] * (SWA_H // 2), axis=-1))
    q = q * scale
    kw = kr_scr[pl.ds(start, win), :]
    vw = vl_scr[pl.ds(start, win), :]

    qpos = qoff + lax.broadcasted_iota(jnp.int32, (BLK, win), 0)
    kpos = start + lax.broadcasted_iota(jnp.int32, (BLK, win), 1)
    valid = jnp.abs(qpos - kpos) <= SWA_WINDOW
    valid = jnp.concatenate([valid] * group, axis=0)

    outs = []
    for g in range(SWA_KVH):
        heads = range(g * group, (g + 1) * group)
        qg = jnp.concatenate([q[:, h * SWA_D:(h + 1) * SWA_D] for h in heads], axis=0).astype(BF16)
        sink = jnp.concatenate([jnp.broadcast_to(sink_ref[0:1, h:h + 1], (BLK, 1)) for h in heads], axis=0)
        ksl = slice(g * SWA_D, (g + 1) * SWA_D)
        vsl = slice(g * 2 * SWA_D, (g + 1) * 2 * SWA_D)
        s = jnp.concatenate([jnp.where(valid, _dot_nt(qg, kw[:, ksl]), NEG_INF),
                             _dot_nt(qg, kc_scr[:, ksl])], axis=-1)
        m = jnp.maximum(jnp.max(s, axis=-1, keepdims=True), sink)
        p = jnp.exp(s - m).astype(BF16)
        pv = _dot(p[:, :win], vw[:, vsl]) + _dot(p[:, win:], vc_scr[:, vsl])
        o = pv[:, :SWA_D] / (pv[:, SWA_D:SWA_D + 1] + jnp.exp(sink - m))
        outs += [o[t * BLK:(t + 1) * BLK] for t in range(group)]
    o = jnp.concatenate(outs, axis=-1)
    o_ref[0] = (o * _silu(gp_ref[0])).astype(BF16)


def _swa(z, cache_k, cache_v, layer, sink, cos2, sin2):
    b, s, _ = z.shape
    lc = cache_k.shape[2]
    hd = SWA_H * SWA_D
    kd = SWA_KVH * SWA_D
    cspec = pl.BlockSpec((1, 1, lc, kd), lambda bi, i: (bi, layer, 0, 0))
    tq = BLK
    return pl.pallas_call(
        functools.partial(_swa_body, seq=s),
        grid=(b, s // tq),
        in_specs=[pl.BlockSpec((1, tq, hd), lambda bi, i: (bi, i, OFF_SQ // hd)),
                  pl.BlockSpec((1, s, kd), lambda bi, i: (bi, 0, OFF_SK // kd)),
                  pl.BlockSpec((1, s, kd), lambda bi, i: (bi, 0, OFF_SV // kd)),
                  cspec, cspec,
                  pl.BlockSpec((1, SWA_H), lambda bi, i: (0, 0)),
                  pl.BlockSpec((s, 128), lambda bi, i: (0, 0)),
                  pl.BlockSpec((s, 128), lambda bi, i: (0, 0)),
                  pl.BlockSpec((1, tq, hd), lambda bi, i: (bi, i, (OFF_GP + 3 * BR_W) // hd))],
        out_specs=pl.BlockSpec((1, tq, hd), lambda bi, i: (bi, i, 0)),
        out_shape=jax.ShapeDtypeStruct((b, s, hd), BF16),
        scratch_shapes=[pltpu.VMEM((s, kd), BF16), pltpu.VMEM((s, 2 * kd), BF16),
                        pltpu.VMEM((lc, kd), BF16), pltpu.VMEM((lc, 2 * kd), BF16)],
        compiler_params=_cp(("parallel", "arbitrary"), 48),
        name="swa_attn",
    )(z, z, z, cache_k, cache_v, sink.reshape(1, SWA_H), cos2, sin2, z)


MERGE_SUB = 2


def _merge_body(h_ref, o0_ref, o1_ref, o2_ref, o3_ref, g0_ref, g1_ref, g2_ref, g3_ref, wb_ref, y_ref):
    h = h_ref[...]
    o_refs = (o0_ref, o1_ref, o2_ref, o3_ref)
    g_refs = (g0_ref, g1_ref, g2_ref, g3_ref)
    tsub = y_ref.shape[1] // MERGE_SUB
    for c in range(MERGE_SUB):
        cols = slice(c * tsub, (c + 1) * tsub)
        y = None
        for n in range(N_BRANCH):
            t = _sigmoid(_dot_nt(h, g_refs[n][cols, :])) * _dot(o_refs[n][...], wb_ref[n, :, cols])
            y = t if y is None else y + t
        y_ref[:, cols] = y.astype(BF16)


def _merge(h, ogs, w_gate_t, w_branch, layer, tm, tn):
    m, d = h.shape
    tm = min(tm, m)
    tn = min(tn, d)
    nj = d // tn
    gspec = lambda n: pl.BlockSpec((None, tn, d), lambda i, j: (layer, n * nj + j, 0))
    return pl.pallas_call(
        _merge_body,
        grid=(m // tm, nj),
        in_specs=[pl.BlockSpec((tm, d), lambda i, j: (i, 0))]
        + [pl.BlockSpec((tm, BR_W), lambda i, j: (i, 0))] * N_BRANCH
        + [gspec(n) for n in range(N_BRANCH)]
        + [pl.BlockSpec((None, N_BRANCH, BR_W, tn), lambda i, j: (layer, 0, 0, j))],
        out_specs=pl.BlockSpec((tm, tn), lambda i, j: (i, j)),
        out_shape=jax.ShapeDtypeStruct((m, d), BF16),
        compiler_params=_cp(("parallel", "arbitrary"), 56),
        name="merge",
    )(h, *ogs, w_gate_t, w_gate_t, w_gate_t, w_gate_t, w_branch)


OUT_SUB = 2


def _out_body(*refs, has_next):
    if has_next:
        y_ref, w_ref, x_ref, gate_ref, g_ref, gn_ref, sh_ref, sc_ref, o_ref, h_ref = refs
    else:
        y_ref, w_ref, x_ref, gate_ref, g_ref, o_ref = refs
    ts = y_ref.shape[1]
    tsub = ts // OUT_SUB
    for sub in range(OUT_SUB):
        rows = slice(sub * tsub, (sub + 1) * tsub)
        t = _dot(y_ref[0, rows, :], w_ref[...])
        t = t * lax.rsqrt(jnp.mean(t * t, axis=-1, keepdims=True) + EPS) * g_ref[...]
        x = x_ref[0, rows, :] + gate_ref[0] * t
        o_ref[0, rows, :] = x
        if has_next:
            y = x * lax.rsqrt(jnp.mean(x * x, axis=-1, keepdims=True) + EPS)
            y = y * gn_ref[...]
            h_ref[0, rows, :] = (y * (1.0 + sc_ref[0]) + sh_ref[0]).astype(BF16)


def _out_proj(y, w_out, layer, x, gate, g, nxt):
    b, s, d = x.shape
    ts = min(s, 512)
    per_b = gate.shape[0] > 1
    cidx = (lambda bi, i: (bi, 0, 0)) if per_b else (lambda bi, i: (0, 0, 0))
    tile = pl.BlockSpec((1, ts, d), lambda bi, i: (bi, i, 0))
    vec = pl.BlockSpec((1, d), lambda bi, i: (0, 0))
    in_specs = [tile, pl.BlockSpec((None, d, d), lambda bi, i: (layer, 0, 0), pipeline_mode=pl.Buffered(1)),
                tile, pl.BlockSpec((1, 1, d), cidx), vec]
    args = [y, w_out, x, gate, g.reshape(1, d)]
    out_specs = [tile]
    out_shape = [jax.ShapeDtypeStruct((b, s, d), F32)]
    if nxt is not None:
        gn, shift, scale = nxt
        in_specs += [vec, pl.BlockSpec((1, 1, d), cidx), pl.BlockSpec((1, 1, d), cidx)]
        args += [gn.reshape(1, d), shift, scale]
        out_specs.append(tile)
        out_shape.append(jax.ShapeDtypeStruct((b, s, d), BF16))
    res = pl.pallas_call(
        functools.partial(_out_body, has_next=nxt is not None),
        grid=(b, s // ts),
        in_specs=in_specs,
        out_specs=out_specs,
        out_shape=out_shape,
        compiler_params=_cp(("parallel", "parallel"), 56),
        name="out_proj",
    )(*args)
    return (res[0], res[1]) if nxt is not None else (res[0], None)


_MIX_SRC = ((0, 512), (832, 4416), (4672, 6720), (512, 768), (4416, 4672), (768, 832))
_MIX_COLS = 6720


RELAYOUT_ROWS = 256
RELAYOUT_ALIGN = 64


def _relayout_body(off_ref, valid_ref, w_ref, o_ref):
    j = pl.program_id(1)
    row = lax.broadcasted_iota(jnp.int32, w_ref.shape, 0)
    o_ref[...] = jnp.where(row < valid_ref[j], w_ref[...], 0.0).astype(BF16)


def _relayout_rows(w_t, offs, valid):
    depth, _, d = w_t.shape
    nblk = len(offs)
    grid_spec = pltpu.PrefetchScalarGridSpec(
        num_scalar_prefetch=2,
        grid=(depth, nblk),
        in_specs=[pl.BlockSpec((pl.Element(1), pl.Element(RELAYOUT_ROWS), pl.Element(d)),
                               lambda l, j, off, val: (l, off[j] * RELAYOUT_ALIGN, 0))],
        out_specs=pl.BlockSpec((1, RELAYOUT_ROWS, d), lambda l, j, off, val: (l, j, 0)),
    )
    return pl.pallas_call(
        _relayout_body,
        grid_spec=grid_spec,
        out_shape=jax.ShapeDtypeStruct((depth, nblk * RELAYOUT_ROWS, d), BF16),
        compiler_params=_cp(("parallel", "arbitrary"), 32),
        name="relayout_w_in",
    )(jnp.asarray([o // RELAYOUT_ALIGN for o in offs], jnp.int32), jnp.asarray(valid, jnp.int32), w_t)


def _relayout_w_in(w_in):
    n = w_in.shape[2]
    w_t = jnp.swapaxes(w_in, 1, 2)
    r = RELAYOUT_ROWS
    offs, valid = [], []
    for a, b in _MIX_SRC:
        assert len(offs) * r == sum(valid)
        for o in range(a, b, r):
            offs.append(o)
            valid.append(min(r, b - o))
    while len(offs) * r < N_MIX:
        offs.append(0)
        valid.append(0)
    assert len(offs) * r == N_MIX and sum(valid) == _MIX_COLS
    w_mix_t = _relayout_rows(w_t, offs, valid)
    ng = (n - _MIX_COLS) // r
    w_gate_t = _relayout_rows(w_t, [_MIX_COLS + r * j for j in range(ng)], [r] * ng)
    return w_mix_t, w_gate_t


def _mla_weights(w_q_up, w_kv_up):
    depth = w_q_up.shape[0]
    wq = w_q_up.reshape(depth, MLA_Q_LORA, MLA_H, MLA_NOPE + MLA_ROPE)
    wq = jnp.pad(wq, ((0, 0), (0, 0), (0, 0), (0, MLA_DQ - MLA_NOPE - MLA_ROPE)))
    wq = wq.reshape(depth, MLA_Q_LORA, MLA_H * MLA_DQ).astype(BF16)
    wkv = w_kv_up.reshape(depth, MLA_KV_LORA, MLA_H, MLA_NOPE + MLA_V)
    wk = jnp.pad(wkv[..., :MLA_NOPE], ((0, 0), (0, 0), (0, 0), (0, MLA_DQ - MLA_NOPE)))
    wk = wk.reshape(depth, MLA_KV_LORA, MLA_H * MLA_DQ).astype(BF16)
    wv = jnp.pad(wkv[..., MLA_NOPE:], ((0, 0), (0, 0), (0, 0), (0, MLA_DV - MLA_V)))
    wv = wv.reshape(depth, MLA_KV_LORA, MLA_H * MLA_DV).astype(BF16)
    j = jnp.arange(128)[:, None]
    cidx = jnp.arange(MLA_H * MLA_DQ)[None, :]
    p128 = ((cidx % MLA_DQ == MLA_NOPE + j) & (j < MLA_ROPE)).astype(BF16)
    vones = (jnp.arange(MLA_H * MLA_DV)[None, :] % MLA_DV == MLA_V).astype(F32)
    return wq, wk, wv, p128, vones


def _rope_tables(n_tok):
    pos = jnp.arange(n_tok)
    row = (pos // GRID_W).astype(F32)
    col = (pos % GRID_W).astype(F32)
    n_freq = MLA_ROPE // 4
    inv = ROPE_BASE ** (-jnp.arange(n_freq, dtype=F32) / n_freq)
    ang = jnp.concatenate([row[:, None] * inv[None], col[:, None] * inv[None]], axis=-1)
    cos, sin = jnp.cos(ang), jnp.sin(ang)
    c64 = jnp.concatenate([cos, cos], axis=-1)
    s64 = jnp.concatenate([-sin, sin], axis=-1)
    one, zero = jnp.ones_like(c64), jnp.zeros_like(c64)
    cq = jnp.concatenate([one, one, c64, one], axis=-1)
    sq = jnp.concatenate([zero, zero, s64, zero], axis=-1)
    ck = jnp.concatenate([c64, one], axis=-1)
    sk = jnp.concatenate([s64, zero], axis=-1)
    c2 = jnp.concatenate([c64, c64], axis=-1)
    s2 = jnp.concatenate([s64, s64], axis=-1)
    return (cq, sq, ck, sk), (c2, s2)


TM_IN = 2048
TN_IN = 768
TM_MERGE = 1024
TN_MERGE = 512


def kernel(x_prompt, x_sample, cache_mla_ckv, cache_mla_krope, state_ret, cache_nat_k, cache_nat_v,
           cache_swa_k, cache_swa_v, c, c_ctx, w_mod, b_mod, norm_pre, norm_post, w_in, mla_q_norm,
           mla_kv_norm, mla_w_q_up, mla_w_kv_up, ret_decay, ret_norm, nat_rpb, swa_sink, w_branch, w_out):
    depth = w_mod.shape[0]
    d = x_prompt.shape[-1]
    nb = x_sample.shape[0]
    s_lat = x_sample.shape[1]
    l_ctx = cache_mla_ckv.shape[2]

    w_mix, w_gate = _relayout_w_in(w_in)
    w_br = w_branch.astype(BF16)
    w_o = w_out.astype(BF16)
    wq, wk, wv, p128, vones = _mla_weights(mla_w_q_up, mla_w_kv_up)
    p64 = p128[:MLA_ROPE]
    mla_tabs, (cos2, sin2) = _rope_tables(s_lat)
    nat_bias = _nat_bias_tables(nat_rpb, s_lat // GRID_W)
    nat_k_c = cache_nat_k.reshape(nb, depth, l_ctx, NAT_H * NAT_D)
    nat_v_c = cache_nat_v.reshape(nb, depth, l_ctx, NAT_H * NAT_D)
    swa_k_c = cache_swa_k.reshape(nb, depth, l_ctx, SWA_KVH * SWA_D)
    swa_v_c = cache_swa_v.reshape(nb, depth, l_ctx, SWA_KVH * SWA_D)

    rows = 8
    conds = jnp.concatenate([c_ctx[None, :], c, jnp.zeros((rows - 1 - nb, d), F32)], axis=0)
    mod = _modulation(conds, w_mod, b_mod)

    def mod_parts(l, lo, hi):
        m = mod[l, lo:hi]
        return m[:, None, :d], m[:, None, d:2 * d], m[:, None, 2 * d:]

    def finish_layer(l, lo, hi, x, h, ogs):
        b, s, _ = x.shape
        gate = mod_parts(l, lo, hi)[2]
        y = _merge(h.reshape(b * s, d), [o.reshape(b * s, BR_W) for o in ogs], w_gate, w_br, l, TM_MERGE, TN_MERGE)
        nxt = None
        if l + 1 < depth:
            shift_n, scale_n, _ = mod_parts(l + 1, lo, hi)
            nxt = (norm_pre[l + 1], shift_n, scale_n)
        return _out_proj(y.reshape(b, s, d), w_o, l, x, gate, norm_post[l], nxt)

    def start_pass(x, lo, hi):
        shift, scale, _ = mod_parts(0, lo, hi)
        return _modnorm(x, norm_pre[0], shift, scale)

    def in_proj(x, h, l):
        b, s, _ = x.shape
        return _in_proj(h.reshape(b * s, d), w_mix, l, TM_IN, TN_IN).reshape(b, s, N_MIX)

    yp = x_prompt
    bc, sc, _ = yp.shape
    ckv_l, kr_l, st_l, nk_l, nv_l, sk_l, sv_l = [], [], [], [], [], [], []
    h = start_pass(yp, 0, 1)
    for l in range(depth):
        z = in_proj(yp, h, l)
        q, kf, vf, ckv = _mla_prep(z, mla_q_norm[l], mla_kv_norm[l], wq[l], wk[l], wv[l], p128, vones, None)
        og_mla, og_nat, og_swa = _ctx_attn(q, kf, vf, z, swa_sink[l])
        og_ret, st = _retention(z, ret_decay[l], ret_norm[l], None, 0, True)
        yp, h = finish_layer(l, 0, 1, yp, h, (og_mla, og_ret, og_nat, og_swa))
        ckv_l.append(ckv)
        kr_l.append(z[:, :, OFF_KR:OFF_KR + MLA_ROPE])
        st_l.append(st)
        nk_l.append(z[:, :, OFF_NK:OFF_NK + 512].reshape(bc, sc, NAT_H, NAT_D))
        nv_l.append(z[:, :, OFF_NV:OFF_NV + 512].reshape(bc, sc, NAT_H, NAT_D))
        sk_l.append(z[:, :, OFF_SK:OFF_SK + 128].reshape(bc, sc, SWA_KVH, SWA_D))
        sv_l.append(z[:, :, OFF_SV:OFF_SV + 128].reshape(bc, sc, SWA_KVH, SWA_D))

    ys = x_sample
    h = start_pass(ys, 1, 1 + nb)
    for l in range(depth):
        z = in_proj(ys, h, l)
        q, kf, vf, _ = _mla_prep(z, mla_q_norm[l], mla_kv_norm[l], wq[l], wk[l], wv[l], p128, vones, mla_tabs)
        kc, vc = _mla_expand(cache_mla_ckv, cache_mla_krope, l, wk[l], wv[l], p64, vones)
        og_mla = _mla_attn(q, kf, vf, z, OFF_GP // BR_W, ctx=(kc, vc))
        og_ret, _ = _retention(z, ret_decay[l], ret_norm[l], state_ret, l, False)
        og_nat = _nat(z, nat_k_c, nat_v_c, l, nat_bias)
        og_swa = _swa(z, swa_k_c, swa_v_c, l, swa_sink[l], cos2, sin2)
        ys, h = finish_layer(l, 1, 1 + nb, ys, h, (og_mla, og_ret, og_nat, og_swa))

    return (yp, ys, jnp.stack(ckv_l, axis=1), jnp.stack(kr_l, axis=1), jnp.stack(st_l, axis=1),
            jnp.stack(nk_l, axis=1), jnp.stack(nv_l, axis=1), jnp.stack(sk_l, axis=1), jnp.stack(sv_l, axis=1))
```
